```python
import math
import jax, jax.numpy as jnp
from jax import lax
import numpy as np

D_MODEL = 4096
BATCH = 4
SEQ = 2048
DEPTH = 2
DEC_BATCH = 32
DEC_SEQ = 8
PAST_LEN = 16384
PAGE_SIZE = 128

N_A_LAYERS = DEPTH // 2
N_B_LAYERS = DEPTH - N_A_LAYERS
H_A = 32
DK_A = 128
DV_A = 128
D_QK = H_A * DK_A
D_V = H_A * DV_A
C_CONV = 2 * D_QK + D_V
CONV_W = 4
DELTA_CHUNK = 64
A_PROJ = 2 * D_QK + 2 * D_V + 2 * H_A
H_B = 64
KV_HEADS_B = 8
GROUP_B = H_B // KV_HEADS_B
HEAD_DIM_B = 64
KV_WIDTH = KV_HEADS_B * HEAD_DIM_B
WINDOW = 128
ROT_DIM = HEAD_DIM_B // 4
ROPE_THETA = 500000.0
_FF_RAW = -(-8 * D_MODEL // 3)
D_FF = -(-_FF_RAW // 256) * 256
EPS = 1e-6

kernel_name = "yoco_gated_deltanet_swa_sinks_step"


def _rmsnorm(x, w):
    xf = x.astype(jnp.float32)
    y = xf * lax.rsqrt(jnp.mean(xf * xf, axis=-1, keepdims=True) + EPS)
    return (y * w.astype(jnp.float32)).astype(x.dtype)


def _l2norm(x):
    xf = x.astype(jnp.float32)
    return xf * lax.rsqrt(jnp.sum(xf * xf, axis=-1, keepdims=True) + EPS)


def _swiglu(xn, w_gu, w_down):
    gu = xn @ w_gu
    return (jax.nn.silu(gu[..., :D_FF]) * gu[..., D_FF:]) @ w_down


def _rope_partial(x, pos):
    half = ROT_DIM // 2
    inv_freq = jnp.power(ROPE_THETA, -jnp.arange(half, dtype=jnp.float32) * 2.0 / ROT_DIM)
    ang = pos[:, None] * inv_freq[None, :]
    cos = jnp.cos(ang)[:, None, :]
    sin = jnp.sin(ang)[:, None, :]
    xr = x[..., :ROT_DIM].astype(jnp.float32)
    x1, x2 = xr[..., :half], xr[..., half:]
    rot = jnp.concatenate([x1 * cos - x2 * sin, x2 * cos + x1 * sin], axis=-1)
    return jnp.concatenate([rot.astype(x.dtype), x[..., ROT_DIM:]], axis=-1)


def _gated_delta_rule(q, k, v, g, beta, h0):
    B, T, H, DK = q.shape
    DV = v.shape[-1]
    C = math.gcd(T, DELTA_CHUNK)
    N = T // C

    def blocks(a):
        a = a.astype(jnp.float32).reshape((B, N, C, H) + a.shape[3:])
        return jnp.moveaxis(jnp.moveaxis(a, 1, 0), 3, 2)

    causal = jnp.tril(jnp.ones((C, C), dtype=bool))
    strict = jnp.tril(jnp.ones((C, C), dtype=bool), k=-1)
    eye = jnp.eye(C, dtype=jnp.float32)

    def step(h, inp):
        qc, kc, vc, gc, bc = inp
        gcum = jnp.cumsum(gc, axis=-1)
        decay = jnp.exp(jnp.where(causal, gcum[..., :, None] - gcum[..., None, :], -jnp.inf))
        kk = jnp.einsum('bhcd,bhsd->bhcs', kc, kc)
        lower = jnp.where(strict, kk * decay * bc[..., :, None], 0.0) + eye
        rhs = vc * bc[..., None] - jnp.einsum('bhcd,bhde->bhce', kc * (bc * jnp.exp(gcum))[..., None], h)
        u = lax.linalg.triangular_solve(lower, rhs, left_side=True, lower=True)
        qk = jnp.einsum('bhcd,bhsd->bhcs', qc, kc) * decay
        o = (jnp.einsum('bhcd,bhde->bhce', qc * jnp.exp(gcum)[..., None], h)
             + jnp.einsum('bhcs,bhse->bhce', qk, u))
        g_last = gcum[..., -1:]
        h_new = (h * jnp.exp(g_last)[..., None]
                 + jnp.einsum('bhcd,bhce->bhde', kc * jnp.exp(g_last - gcum)[..., None], u))
        return h_new, o

    h_final, o = lax.scan(step, h0.astype(jnp.float32),
                          (blocks(q), blocks(k), blocks(v), blocks(g), blocks(beta)))
    o = jnp.moveaxis(jnp.moveaxis(o, 2, 3), 0, 1).reshape(B, T, H, DV)
    return o, h_final


def _delta_mixer(xn, conv_buf, h0, w_in, conv_w, a_log, dt_bias, o_norm_w, w_out):
    B, T, _ = xn.shape
    proj = xn @ w_in
    qkv = proj[..., :C_CONV]
    z = proj[..., C_CONV:C_CONV + D_V]
    b_raw = proj[..., C_CONV + D_V:C_CONV + D_V + H_A]
    a_raw = proj[..., C_CONV + D_V + H_A:]
    xp = jnp.concatenate([conv_buf.astype(qkv.dtype), qkv], axis=1)
    conv = sum(xp[:, i:i + T] * conv_w[i] for i in range(CONV_W))
    conv = jax.nn.silu(conv)
    new_buf = xp[:, T:]
    q = _l2norm(conv[..., :D_QK].reshape(B, T, H_A, DK_A)) * (DK_A ** -0.5)
    k = _l2norm(conv[..., D_QK:2 * D_QK].reshape(B, T, H_A, DK_A))
    v = conv[..., 2 * D_QK:].reshape(B, T, H_A, DV_A)
    beta = jax.nn.sigmoid(b_raw.astype(jnp.float32))
    g = -jnp.exp(a_log.astype(jnp.float32)) * jax.nn.softplus(a_raw.astype(jnp.float32) + dt_bias.astype(jnp.float32))
    o, h_final = _gated_delta_rule(q, k, v, g, beta, h0)
    o = _rmsnorm(o, o_norm_w) * jax.nn.silu(z.reshape(B, T, H_A, DV_A).astype(jnp.float32))
    y = o.reshape(B, T, D_V).astype(xn.dtype) @ w_out
    return y, new_buf, h_final


def _shared_kv(h, kv_norm, w_kv, k_norm, pos):
    B, T, _ = h.shape
    kv = _rmsnorm(h, kv_norm) @ w_kv
    k = kv[..., :KV_WIDTH].reshape(B, T, KV_HEADS_B, HEAD_DIM_B)
    v = kv[..., KV_WIDTH:].reshape(B, T, KV_HEADS_B, HEAD_DIM_B)
    k = _rope_partial(_rmsnorm(k, k_norm), pos)
    return k, v


def _sink_attention(q, k, v, mask, sinks):
    s = jnp.einsum('...qhgd,...shd->...hgqs', q, k, preferred_element_type=jnp.float32) * (HEAD_DIM_B ** -0.5)
    s = jnp.where(mask, s, -jnp.inf)
    sink = sinks.astype(jnp.float32).reshape(KV_HEADS_B, GROUP_B, 1, 1)
    m = jnp.maximum(jnp.max(s, axis=-1, keepdims=True), sink)
    p = jnp.exp(s - m)
    denom = jnp.sum(p, axis=-1, keepdims=True) + jnp.exp(sink - m)
    o = jnp.einsum('...hgqs,...shd->...qhgd', p / denom, v.astype(jnp.float32))
    return o.astype(q.dtype)


def _window_attn_prompt(q, k, v, sinks):
    B, T = q.shape[:2]
    NB = T // WINDOW
    qb = q.reshape(B, NB, WINDOW, KV_HEADS_B, GROUP_B, HEAD_DIM_B)

    def with_prev(a):
        ab = a.reshape(B, NB, WINDOW, KV_HEADS_B, HEAD_DIM_B)
        prev = jnp.concatenate([jnp.zeros_like(ab[:, :1]), ab[:, :-1]], axis=1)
        return jnp.concatenate([prev, ab], axis=2)

    kb, vb = with_prev(k), with_prev(v)
    blk = jnp.arange(NB)[:, None] * WINDOW
    q_pos = blk + jnp.arange(WINDOW)[None, :]
    k_pos = blk - WINDOW + jnp.arange(2 * WINDOW)[None, :]
    d = q_pos[:, :, None] - k_pos[:, None, :]
    mask = (d >= 0) & (d < WINDOW) & (k_pos[:, None, :] >= 0)
    o = _sink_attention(qb, kb, vb, mask[None, :, None, None], sinks)
    return o.reshape(B, T, H_B * HEAD_DIM_B)


def _window_attn_sample(q, k_ctx, v_ctx, sinks, pos_start):
    T = q.shape[1]
    W = k_ctx.shape[1] - T
    q_pos = pos_start + jnp.arange(T)
    k_pos = pos_start - W + jnp.arange(W + T)
    d = q_pos[:, None] - k_pos[None, :]
    mask = (d >= 0) & (d < WINDOW)
    o = _sink_attention(q, k_ctx, v_ctx, mask, sinks)
    return o.reshape(q.shape[0], T, H_B * HEAD_DIM_B)


def _trunk(x, ssm0, conv0, win_k0, win_v0, pos_start, is_prompt, w):
    B, T, _ = x.shape
    pos = pos_start + jnp.arange(T, dtype=jnp.float32)
    h = x
    ssm_out, conv_out = [], []
    for l in range(N_A_LAYERS):
        y, cbuf, hs = _delta_mixer(_rmsnorm(h, w['norm_mix'][l]), conv0[l], ssm0[l], w['a_w_in'][l],
                                   w['a_conv_w'][l], w['a_log'][l], w['a_dt_bias'][l],
                                   w['a_o_norm'][l], w['a_w_out'][l])
        h = h + y
        h = h + _swiglu(_rmsnorm(h, w['norm_ffn'][l]), w['ffn_w_gu'][l], w['ffn_w_down'][l])
        ssm_out.append(hs)
        conv_out.append(cbuf)
    k, v = _shared_kv(h, w['kv_norm'], w['w_kv'], w['k_norm'], pos)
    if is_prompt:
        wb = min(WINDOW, T)
        new_k, new_v = k[:, T - wb:], v[:, T - wb:]
    else:
        k_ctx = jnp.concatenate([win_k0.astype(k.dtype), k], axis=1)
        v_ctx = jnp.concatenate([win_v0.astype(v.dtype), v], axis=1)
        wb = win_k0.shape[1]
        new_k, new_v = k_ctx[:, T:], v_ctx[:, T:]
        assert new_k.shape[1] == wb
    for j in range(N_B_LAYERS):
        l = N_A_LAYERS + j
        hn = _rmsnorm(h, w['norm_mix'][l])
        q = (hn @ w['b_w_q'][j]).reshape(B, T, H_B, HEAD_DIM_B)
        q = _rope_partial(_rmsnorm(q, w['b_q_norm'][j]), pos).reshape(B, T, KV_HEADS_B, GROUP_B, HEAD_DIM_B)
        if is_prompt:
            o = _window_attn_prompt(q, k, v, w['b_sinks'][j])
        else:
            o = _window_attn_sample(q, k_ctx, v_ctx, w['b_sinks'][j], pos_start)
        h = h + o @ w['b_w_o'][j]
        h = h + _swiglu(_rmsnorm(h, w['norm_ffn'][l]), w['ffn_w_gu'][l], w['ffn_w_down'][l])
    return h, jnp.stack(ssm_out), jnp.stack(conv_out), new_k, new_v


def setup_inputs(seed: int = 0) -> dict:
    key = jax.random.key(seed)
    ks = jax.random.split(key, 24)
    f32 = jnp.float32

    def nrm(k, shape, scale):
        return jax.random.normal(k, shape, f32) * scale

    w_buf = min(WINDOW, PAST_LEN)
    return {
        "x_prompt": nrm(ks[0], (BATCH, SEQ, D_MODEL), 1.0),
        "x_sample": nrm(ks[1], (DEC_BATCH, DEC_SEQ, D_MODEL), 1.0),
        "state_ssm": nrm(ks[2], (N_A_LAYERS, DEC_BATCH, H_A, DK_A, DV_A), 0.05),
        "state_conv": nrm(ks[3], (N_A_LAYERS, DEC_BATCH, CONV_W - 1, C_CONV), 1.0),
        "cache_win_k": nrm(ks[4], (DEC_BATCH, w_buf, KV_HEADS_B, HEAD_DIM_B), 1.0),
        "cache_win_v": nrm(ks[5], (DEC_BATCH, w_buf, KV_HEADS_B, HEAD_DIM_B), 1.0),
        "norm_mix": 1.0 + nrm(ks[6], (DEPTH, D_MODEL), 0.02),
        "norm_ffn": 1.0 + nrm(ks[7], (DEPTH, D_MODEL), 0.02),
        "a_w_in": nrm(ks[8], (N_A_LAYERS, D_MODEL, A_PROJ), D_MODEL ** -0.5),
        "a_conv_w": nrm(ks[9], (N_A_LAYERS, CONV_W, C_CONV), CONV_W ** -0.5),
        "a_log": jnp.log(jax.random.uniform(ks[10], (N_A_LAYERS, H_A), f32, 1.0, 16.0)),
        "a_dt_bias": nrm(ks[11], (N_A_LAYERS, H_A), 0.1),
        "a_o_norm": 1.0 + nrm(ks[12], (N_A_LAYERS, DV_A), 0.02),
        "a_w_out": nrm(ks[13], (N_A_LAYERS, D_V, D_MODEL), D_V ** -0.5),
        "kv_norm": 1.0 + nrm(ks[14], (D_MODEL,), 0.02),
        "w_kv": nrm(ks[15], (D_MODEL, 2 * KV_WIDTH), D_MODEL ** -0.5),
        "k_norm": 1.0 + nrm(ks[16], (HEAD_DIM_B,), 0.02),
        "b_w_q": nrm(ks[17], (N_B_LAYERS, D_MODEL, H_B * HEAD_DIM_B), D_MODEL ** -0.5),
        "b_q_norm": 1.0 + nrm(ks[18], (N_B_LAYERS, HEAD_DIM_B), 0.02),
        "b_sinks": nrm(ks[19], (N_B_LAYERS, H_B), 1.0),
        "b_w_o": nrm(ks[20], (N_B_LAYERS, H_B * HEAD_DIM_B, D_MODEL), (H_B * HEAD_DIM_B) ** -0.5),
        "ffn_w_gu": nrm(ks[21], (DEPTH, D_MODEL, 2 * D_FF), D_MODEL ** -0.5),
        "ffn_w_down": nrm(ks[22], (DEPTH, D_FF, D_MODEL), D_FF ** -0.5),
    }


def reference(x_prompt, x_sample, state_ssm, state_conv, cache_win_k, cache_win_v,
              norm_mix, norm_ffn, a_w_in, a_conv_w, a_log, a_dt_bias, a_o_norm, a_w_out,
              kv_norm, w_kv, k_norm, b_w_q, b_q_norm, b_sinks, b_w_o, ffn_w_gu, ffn_w_down):
    w = dict(norm_mix=norm_mix, norm_ffn=norm_ffn, a_w_in=a_w_in, a_conv_w=a_conv_w,
             a_log=a_log, a_dt_bias=a_dt_bias, a_o_norm=a_o_norm, a_w_out=a_w_out,
             kv_norm=kv_norm, w_kv=w_kv, k_norm=k_norm, b_w_q=b_w_q, b_q_norm=b_q_norm,
             b_sinks=b_sinks, b_w_o=b_w_o, ffn_w_gu=ffn_w_gu, ffn_w_down=ffn_w_down)
    bp = x_prompt.shape[0]
    ssm0_p = jnp.zeros((N_A_LAYERS, bp, H_A, DK_A, DV_A), jnp.float32)
    conv0_p = jnp.zeros((N_A_LAYERS, bp, CONV_W - 1, C_CONV), x_prompt.dtype)
    y_prompt, ssm_p, conv_p, wk_p, wv_p = _trunk(x_prompt, ssm0_p, conv0_p, None, None, 0, True, w)
    y_sample, ssm_s, conv_s, wk_s, wv_s = _trunk(x_sample, state_ssm, state_conv, cache_win_k,
                                                 cache_win_v, PAST_LEN, False, w)
    return (y_prompt, y_sample, ssm_p, conv_p, wk_p, wv_p, ssm_s, conv_s, wk_s, wv_s)
```

```python
import functools
import math

import jax
import jax.numpy as jnp
from jax import lax
from jax.experimental import pallas as pl
from jax.experimental.pallas import tpu as pltpu

F32 = jnp.float32
BF16 = jnp.bfloat16

EPS = 1e-6
WINDOW = 128
ROPE_THETA = 500000.0
CONV_W = 4
DELTA_CHUNK = 64
LANES = 128
CONV_PAD = 8
VMEM_LIMIT = 56 * 1024 * 1024


def _cparams(sem):
    return pltpu.CompilerParams(dimension_semantics=sem, vmem_limit_bytes=VMEM_LIMIT)


def _sigmoid(x):
    return 1.0 / (1.0 + jnp.exp(-x))


def _bdot(a, b):
    return jnp.dot(a.astype(BF16), b.astype(BF16), preferred_element_type=F32)


def _bdot_nt(a, b):
    return lax.dot_general(a.astype(BF16), b.astype(BF16), (((1,), (1,)), ((), ())),
                           preferred_element_type=F32)


def _bdot_tn(a, b):
    return lax.dot_general(a.astype(BF16), b.astype(BF16), (((0,), (0,)), ((), ())),
                           preferred_element_type=F32)


def _norm_first_kernel(xp_ref, xs_ref, w_ref, xn_ref, xcat_ref, *, n_prompt_blocks):
    i = pl.program_id(0)

    def emit(x):
        r = lax.rsqrt(jnp.mean(x * x, axis=-1, keepdims=True) + EPS)
        xn_ref[...] = (x * r * w_ref[...]).astype(xn_ref.dtype)
        xcat_ref[...] = x

    @pl.when(i < n_prompt_blocks)
    def _():
        emit(xp_ref[...])

    @pl.when(i >= n_prompt_blocks)
    def _():
        emit(xs_ref[...])


def _norm_first(xp, xs, w, tm):
    mp, d = xp.shape
    ms = xs.shape[0]
    assert mp % tm == 0 and ms == tm
    npb = mp // tm
    m = mp + ms
    return pl.pallas_call(
        functools.partial(_norm_first_kernel, n_prompt_blocks=npb),
        grid=(npb + 1,),
        in_specs=[pl.BlockSpec((tm, d), lambda i: (jnp.minimum(i, npb - 1), 0)),
                  pl.BlockSpec((tm, d), lambda i: (0, 0)),
                  pl.BlockSpec((1, d), lambda i: (0, 0))],
        out_specs=[pl.BlockSpec((tm, d), lambda i: (i, 0)),
                   pl.BlockSpec((tm, d), lambda i: (i, 0))],
        out_shape=[jax.ShapeDtypeStruct((m, d), BF16), jax.ShapeDtypeStruct((m, d), F32)],
        compiler_params=_cparams(("arbitrary",)),
        name="norm_first",
    )(xp, xs, w.reshape(1, d))


def _norm_kernel(x_ref, w_ref, *o_refs):
    x = x_ref[...]
    y = x * lax.rsqrt(jnp.mean(x * x, axis=-1, keepdims=True) + EPS)
    for i, o_ref in enumerate(o_refs):
        o_ref[...] = (y * w_ref[i:i + 1, :]).astype(o_ref.dtype)


def _norm(x, ws, tm):
    m, d = x.shape
    nw = ws.shape[0]
    assert m % tm == 0
    return pl.pallas_call(
        _norm_kernel,
        grid=(m // tm,),
        in_specs=[pl.BlockSpec((tm, d), lambda i: (i, 0)),
                  pl.BlockSpec((nw, d), lambda i: (0, 0))],
        out_specs=[pl.BlockSpec((tm, d), lambda i: (i, 0)) for _ in range(nw)],
        out_shape=[jax.ShapeDtypeStruct((m, d), BF16) for _ in range(nw)],
        compiler_params=_cparams(("arbitrary",)),
        name="norm",
    )(x, ws)


CAST_ROWS = 512


def _cast_weight(w_ref, wbf_ref):
    k = w_ref.shape[0]
    step = CAST_ROWS if k % CAST_ROWS == 0 else k
    for c in range(k // step):
        wbf_ref[c * step:(c + 1) * step, :] = w_ref[c * step:(c + 1) * step, :].astype(BF16)


def _mm_wres_kernel(*refs, has_res):
    if has_res:
        x_ref, w_ref, res_ref, o_ref, wbf_ref = refs
    else:
        x_ref, w_ref, o_ref, wbf_ref = refs

    @pl.when(pl.program_id(1) == 0)
    def _():
        _cast_weight(w_ref, wbf_ref)

    acc = jnp.dot(x_ref[...], wbf_ref[...], preferred_element_type=F32)
    if has_res:
        acc = acc + res_ref[...]
    o_ref[...] = acc.astype(o_ref.dtype)


def _mm_wres(x, w, *, n_out, tm, tn, res=None, out_dtype=F32):
    m, k = x.shape
    assert w.shape[0] == k and m % tm == 0 and n_out % tn == 0
    in_specs = [pl.BlockSpec((tm, k), lambda j, i: (i, 0)),
                pl.BlockSpec((k, tn), lambda j, i: (0, j))]
    args = [x, w]
    if res is not None:
        in_specs.append(pl.BlockSpec((tm, tn), lambda j, i: (i, j)))
        args.append(res)
    return pl.pallas_call(
        functools.partial(_mm_wres_kernel, has_res=res is not None),
        grid=(n_out // tn, m // tm),
        in_specs=in_specs,
        out_specs=pl.BlockSpec((tm, tn), lambda j, i: (i, j)),
        out_shape=jax.ShapeDtypeStruct((m, n_out), out_dtype),
        scratch_shapes=[pltpu.VMEM((k, tn), BF16)],
        compiler_params=_cparams(("arbitrary", "arbitrary")),
        name="mm_wres",
    )(*args)


def _ffn_up_kernel(x_ref, wg_ref, wu_ref, o_ref, wgb_ref, wub_ref):
    @pl.when(pl.program_id(1) == 0)
    def _():
        _cast_weight(wg_ref, wgb_ref)
        _cast_weight(wu_ref, wub_ref)

    x = x_ref[...]
    g = jnp.dot(x, wgb_ref[...], preferred_element_type=F32)
    u = jnp.dot(x, wub_ref[...], preferred_element_type=F32)
    o_ref[...] = (g * _sigmoid(g) * u).astype(o_ref.dtype)


def _ffn_up(x, w_gu, *, tm, tn):
    m, k = x.shape
    f = w_gu.shape[1] // 2
    assert m % tm == 0 and f % tn == 0
    nj = f // tn
    return pl.pallas_call(
        _ffn_up_kernel,
        grid=(nj, m // tm),
        in_specs=[pl.BlockSpec((tm, k), lambda j, i: (i, 0)),
                  pl.BlockSpec((k, tn), lambda j, i: (0, j)),
                  pl.BlockSpec((k, tn), lambda j, i: (0, nj + j))],
        out_specs=pl.BlockSpec((tm, tn), lambda j, i: (i, j)),
        out_shape=jax.ShapeDtypeStruct((m, f), BF16),
        scratch_shapes=[pltpu.VMEM((k, tn), BF16), pltpu.VMEM((k, tn), BF16)],
        compiler_params=_cparams(("arbitrary", "arbitrary")),
        name="ffn_up",
    )(x, w_gu, w_gu)


def _mm_ksplit_kernel(x_ref, w_ref, res_ref, o_ref):
    kk = pl.program_id(2)
    part = jnp.dot(x_ref[...], w_ref[...].astype(BF16), preferred_element_type=F32)

    @pl.when(kk == 0)
    def _():
        o_ref[...] = res_ref[...] + part

    @pl.when(kk > 0)
    def _():
        o_ref[...] += part


def _mm_ksplit(x, w, res, *, tm, tn, tk):
    m, k = x.shape
    n = w.shape[1]
    assert m % tm == 0 and n % tn == 0 and k % tk == 0
    return pl.pallas_call(
        _mm_ksplit_kernel,
        grid=(n // tn, m // tm, k // tk),
        in_specs=[pl.BlockSpec((tm, tk), lambda j, i, kk: (i, kk)),
                  pl.BlockSpec((tk, tn), lambda j, i, kk: (kk, j)),
                  pl.BlockSpec((tm, tn), lambda j, i, kk: (i, j))],
        out_specs=pl.BlockSpec((tm, tn), lambda j, i, kk: (i, j)),
        out_shape=jax.ShapeDtypeStruct((m, n), F32),
        compiler_params=_cparams(("arbitrary", "arbitrary", "arbitrary")),
        name="mm_ksplit",
    )(x, w, res)


def _gates_kernel(x_ref, wt_ref, alog_ref, dtb_ref, o_ref, *, n_heads, n_prompt_blocks, c_prompt, c_sample):
    tm = x_ref.shape[0]
    ba = lax.dot_general(wt_ref[...], x_ref[...], (((1,), (1,)), ((), ())), preferred_element_type=F32)
    beta = _sigmoid(ba[:n_heads])
    a = ba[n_heads:] + dtb_ref[...]
    softplus = jnp.maximum(a, 0.0) + jnp.log(1.0 + jnp.exp(-jnp.abs(a)))
    g = -jnp.exp(alog_ref[...]) * softplus
    shift = jnp.where(pl.program_id(0) < n_prompt_blocks,
                      int(math.log2(c_prompt)), int(math.log2(c_sample)))
    jj = lax.broadcasted_iota(jnp.int32, (tm, tm), 0)
    ii = lax.broadcasted_iota(jnp.int32, (tm, tm), 1)
    same = lax.shift_right_logical(jj, shift) == lax.shift_right_logical(ii, shift)
    cum_m = jnp.where(same & (jj <= ii), 1.0, 0.0).astype(F32)
    tot_m = jnp.where(same, 1.0, 0.0).astype(F32)
    gcum = jnp.dot(g, cum_m, preferred_element_type=F32, precision=lax.Precision.HIGHEST)
    glast = jnp.dot(g, tot_m, preferred_element_type=F32, precision=lax.Precision.HIGHEST)
    o_ref[0] = beta
    o_ref[1] = jnp.exp(gcum)
    o_ref[2] = jnp.exp(glast - gcum)
    o_ref[3] = gcum


def _gates(xn, w_ba_t, a_log, dt_bias, *, tm, n_prompt_rows, c_prompt, c_sample):
    m, k = xn.shape
    h = a_log.shape[0]
    assert m % tm == 0 and n_prompt_rows % tm == 0 and tm % c_prompt == 0 and tm % c_sample == 0
    return pl.pallas_call(
        functools.partial(_gates_kernel, n_heads=h, n_prompt_blocks=n_prompt_rows // tm,
                          c_prompt=c_prompt, c_sample=c_sample),
        grid=(m // tm,),
        in_specs=[pl.BlockSpec((tm, k), lambda i: (i, 0)),
                  pl.BlockSpec((2 * h, k), lambda i: (0, 0)),
                  pl.BlockSpec((h, 1), lambda i: (0, 0)),
                  pl.BlockSpec((h, 1), lambda i: (0, 0))],
        out_specs=pl.BlockSpec((4, h, tm), lambda i: (0, 0, i)),
        out_shape=jax.ShapeDtypeStruct((4, h, m), F32),
        compiler_params=_cparams(("arbitrary",)),
        name="gates",
    )(xn, w_ba_t, a_log.reshape(h, 1), dt_bias.reshape(h, 1))


def _delta_kernel(q_ref, k_ref, v_ref, z_ref, cwq_ref, cwk_ref, cwv_ref, csq_ref, csk_ref, csv_ref,
                  gcol_ref, grow_ref, h0_ref, onw_ref,
                  o_ref, hout_ref, cqo_ref, cko_ref, cvo_ref,
                  h_scr, xq_scr, xk_scr, xv_scr, aq_scr, ak_scr, av_scr,
                  *, C, nb, nc, hb, dk, zero_init):
    t = pl.program_id(2)
    n_t = pl.num_programs(2)
    ttb = nc * C
    hist0 = CONV_PAD - (CONV_W - 1)

    @pl.when(t == 0)
    def _():
        if zero_init:
            h_scr[...] = jnp.zeros(h_scr.shape, F32)
        else:
            h_scr[...] = h0_ref[...]
        for xs, cs in ((xq_scr, csq_ref), (xk_scr, csk_ref), (xv_scr, csv_ref)):
            for bi in range(nb):
                xs[bi, hist0:CONV_PAD, :] = cs[bi]

    for raw_ref, xs, cw_ref, act in ((q_ref, xq_scr, cwq_ref, aq_scr), (k_ref, xk_scr, cwk_ref, ak_scr),
                                     (v_ref, xv_scr, cwv_ref, av_scr)):
        for bi in range(nb):
            xs[bi, CONV_PAD:CONV_PAD + ttb, :] = raw_ref[bi * ttb:(bi + 1) * ttb, :]
            acc = xs[bi, hist0:hist0 + ttb, :] * cw_ref[0:1, :]
            for i in range(1, CONV_W):
                acc = acc + xs[bi, hist0 + i:hist0 + i + ttb, :] * cw_ref[i:i + 1, :]
            act[bi * ttb:(bi + 1) * ttb, :] = acc * _sigmoid(acc)
            xs[bi, hist0:CONV_PAD, :] = xs[bi, CONV_PAD + ttb - (CONV_W - 1):CONV_PAD + ttb, :]

    @pl.when(t == n_t - 1)
    def _():
        for xs, co in ((xq_scr, cqo_ref), (xk_scr, cko_ref), (xv_scr, cvo_ref)):
            for bi in range(nb):
                co[bi] = xs[bi, hist0:CONV_PAD, :]

    ii = lax.broadcasted_iota(jnp.int32, (C, C), 0)
    jj = lax.broadcasted_iota(jnp.int32, (C, C), 1)
    n_levels = int(math.log2(C))
    assert 2 ** n_levels == C

    def chunk_head(bi, r0, ci, j):
        rows = pl.ds(r0, C)
        cols = slice(j * dk, (j + 1) * dk)
        q = aq_scr[rows, cols]
        k = ak_scr[rows, cols]
        v = av_scr[rows, cols]
        qn = q * (lax.rsqrt(jnp.sum(q * q, axis=-1, keepdims=True) + EPS) * (dk ** -0.5))
        kn = k * lax.rsqrt(jnp.sum(k * k, axis=-1, keepdims=True) + EPS)
        gcb = gcol_ref[0, rows, :]
        beta = gcb[:, j:j + 1]
        eg = gcb[:, hb + j:hb + j + 1]
        ekl = gcb[:, 2 * hb + j:2 * hb + j + 1]
        gc = gcb[:, 3 * hb + j:3 * hb + j + 1]
        gr = grow_ref[0, ci, j:j + 1, :]
        dmat = jnp.where(ii >= jj, jnp.exp(gc - gr), 0.0)
        s = _bdot_nt(jnp.concatenate([kn, qn], axis=0), kn)
        qk = s[C:] * dmat
        p_w = jnp.where(ii > jj, s[:C] * dmat * (-beta), 0.0)
        t_m = p_w
        for _ in range(n_levels - 1):
            p_w = _bdot(p_w, p_w)
            t_m = t_m + p_w + _bdot(t_m, p_w)
        kb = kn * beta
        rhs = jnp.concatenate([kb * eg, v * beta], axis=1)
        wu = rhs + _bdot(t_m, rhs)
        h = h_scr[bi, j]
        wq = _bdot(jnp.concatenate([wu[:, :dk], qn * eg], axis=0), h)
        u = wu[:, dk:] - wq[:C]
        o = wq[C:] + _bdot(qk, u)
        egl = eg[C - 1:C, :]
        h_scr[bi, j] = h * egl + _bdot_tn(kn * ekl, u)
        on = o * lax.rsqrt(jnp.mean(o * o, axis=-1, keepdims=True) + EPS) * onw_ref[...]
        zz = z_ref[rows, cols]
        o_ref[rows, cols] = (on * (zz * _sigmoid(zz))).astype(o_ref.dtype)

    for bi in range(nb):
        if nc == 1:
            for j in range(hb):
                chunk_head(bi, bi * ttb, bi, j)
        else:
            def body(c, carry, bi=bi):
                r0 = pl.multiple_of(bi * ttb + c * C, C)
                for j in range(hb):
                    chunk_head(bi, r0, bi * nc + c, j)
                return carry
            lax.fori_loop(0, nc, body, 0)

    @pl.when(t == n_t - 1)
    def _():
        hout_ref[...] = h_scr[...]


def _delta_rule(proj, gates, conv_w, conv0, h0, o_norm_w, *, row0, batch, seq, n_heads, dk,
                C, nb, nc, hb, out_rows):
    ttb = nc * C
    tt = nb * ttb
    w = hb * dk
    ng = n_heads // hb
    n_t = seq // ttb
    assert seq % ttb == 0 and batch % nb == 0 and n_heads % hb == 0 and row0 % tt == 0
    zero_init = h0 is None
    d_qk = n_heads * dk
    cb = d_qk // w
    rb0 = row0 // tt

    rows = batch * seq
    g5 = gates.reshape(4, ng, hb, rows)
    gcol = jnp.transpose(g5, (1, 3, 0, 2)).reshape(ng, rows, 4 * hb)
    grow = jnp.transpose(g5[3].reshape(ng, hb, rows // C, C), (0, 2, 1, 3))

    def tok_map(part):
        return lambda b, g, t: (rb0 + b * n_t + t, part * cb + g)

    if zero_init:
        h0_arg = jnp.zeros((nb, hb, dk, dk), F32)
        h0_spec = pl.BlockSpec((nb, hb, dk, dk), lambda b, g, t: (0, 0, 0, 0))
    else:
        h0_arg = h0
        h0_spec = pl.BlockSpec((nb, hb, dk, dk), lambda b, g, t: (b, g, 0, 0))
    cs_spec = [pl.BlockSpec((nb, CONV_W - 1, w), (lambda b, g, t, p=p: (b, 0, p * cb + g))) for p in range(3)]
    cw_spec = [pl.BlockSpec((CONV_W, w), (lambda b, g, t, p=p: (0, p * cb + g))) for p in range(3)]
    kern = functools.partial(_delta_kernel, C=C, nb=nb, nc=nc, hb=hb, dk=dk, zero_init=zero_init)
    o, h_fin, cq, ck, cv = pl.pallas_call(
        kern,
        grid=(batch // nb, ng, n_t),
        in_specs=[pl.BlockSpec((tt, w), tok_map(0)), pl.BlockSpec((tt, w), tok_map(1)),
                  pl.BlockSpec((tt, w), tok_map(2)), pl.BlockSpec((tt, w), tok_map(3)),
                  *cw_spec, *cs_spec,
                  pl.BlockSpec((1, tt, 4 * hb), lambda b, g, t: (g, b * n_t + t, 0)),
                  pl.BlockSpec((1, nb * nc, hb, C), lambda b, g, t: (g, b * n_t + t, 0, 0)),
                  h0_spec,
                  pl.BlockSpec((1, dk), lambda b, g, t: (0, 0))],
        out_specs=[pl.BlockSpec((tt, w), lambda b, g, t: (rb0 + b * n_t + t, g)),
                   pl.BlockSpec((nb, hb, dk, dk), lambda b, g, t: (b, g, 0, 0)),
                   *[pl.BlockSpec((nb, CONV_W - 1, w), lambda b, g, t: (b, 0, g)) for _ in range(3)]],
        out_shape=[jax.ShapeDtypeStruct((out_rows, d_qk), BF16),
                   jax.ShapeDtypeStruct((batch, n_heads, dk, dk), F32),
                   *[jax.ShapeDtypeStruct((batch, CONV_W - 1, d_qk), F32) for _ in range(3)]],
        scratch_shapes=[pltpu.VMEM((nb, hb, dk, dk), F32),
                        *[pltpu.VMEM((nb, CONV_PAD + ttb, w), F32) for _ in range(3)],
                        *[pltpu.VMEM((tt, w), F32) for _ in range(3)]],
        compiler_params=_cparams(("arbitrary", "arbitrary", "arbitrary")),
        name="delta_rule",
    )(proj, proj, proj, proj, conv_w, conv_w, conv_w, conv0, conv0, conv0, gcol, grow, h0_arg,
      o_norm_w.reshape(1, dk))
    return o, h_fin, jnp.concatenate([cq, ck, cv], axis=-1)


def _group_mean_sq(x, gmat_ref, inv_n):
    x2 = x * x
    hi = x2.astype(BF16)
    lo = (x2 - hi.astype(F32)).astype(BF16)
    g = gmat_ref[...]
    return (jnp.dot(hi, g, preferred_element_type=F32) + jnp.dot(lo, g, preferred_element_type=F32)) * inv_n


def _rope(y, cos_ref, sa_ref, sb_ref, half):
    w = y.shape[1]
    reps = w // cos_ref.shape[1]
    tile = lambda r: jnp.concatenate([r[...]] * reps, axis=1)
    return (y * tile(cos_ref) + pltpu.roll(y, w - half, axis=1) * tile(sa_ref)
            + pltpu.roll(y, half, axis=1) * tile(sb_ref))


def _q_post_kernel(x_ref, w_ref, gmat_ref, cos_ref, sa_ref, sb_ref, o_ref, *, hd, half, scale):
    x = x_ref[...]
    y = x * lax.rsqrt(_group_mean_sq(x, gmat_ref, 1.0 / hd) + EPS) * w_ref[...]
    o_ref[...] = (_rope(y, cos_ref, sa_ref, sb_ref, half) * scale).astype(o_ref.dtype)


def _kv_post_kernel(k_ref, v_ref, w_ref, gmat_ref, cos_ref, sa_ref, sb_ref, kf_ref, kb_ref, vb_ref,
                    *, hd, half):
    x = k_ref[...]
    y = x * lax.rsqrt(_group_mean_sq(x, gmat_ref, 1.0 / hd) + EPS) * w_ref[...]
    kf = _rope(y, cos_ref, sa_ref, sb_ref, half)
    kf_ref[...] = kf
    v = v_ref[...]
    for hh in range(kb_ref.shape[0]):
        kb_ref[hh] = kf[:, hh * hd:(hh + 1) * hd].astype(BF16)
        vb_ref[hh] = v[:, hh * hd:(hh + 1) * hd].astype(BF16)


def _rope_tables(pos, hd, rot_dim):
    half = rot_dim // 2
    inv_freq = jnp.power(ROPE_THETA, -jnp.arange(half, dtype=F32) * 2.0 / rot_dim)
    ang = pos[:, None] * inv_freq[None, :]
    cos, sin = jnp.cos(ang), jnp.sin(ang)
    m = pos.shape[0]
    ones = jnp.ones((m, hd - rot_dim), F32)
    zeros = jnp.zeros((m, hd - rot_dim), F32)
    zh = jnp.zeros((m, half), F32)
    c = jnp.concatenate([cos, cos, ones], axis=1)
    sa = jnp.concatenate([-sin, zh, zeros], axis=1)
    sb = jnp.concatenate([zh, sin, zeros], axis=1)
    reps = LANES // hd
    return tuple(jnp.tile(a, (1, reps)) for a in (c, sa, sb))


def _group_matrix(width, hd):
    r = jnp.arange(width) // hd
    return (r[:, None] == r[None, :]).astype(BF16)


def _softmax_sink_pv(s, mask, sink, v):
    s = jnp.where(mask, s, -jnp.inf)
    m = jnp.maximum(jnp.max(s, axis=-1, keepdims=True), sink)
    p = jnp.exp(s - m)
    denom = jnp.sum(p, axis=-1, keepdims=True) + jnp.exp(sink - m)
    return jnp.dot((p / denom).astype(BF16), v, preferred_element_type=F32)


def _attn_prompt_kernel(sink_ref, q_ref, kp_ref, kc_ref, vp_ref, vc_ref, o_ref, *, group, hd, win):
    nb = pl.program_id(1)
    kvh = pl.program_id(2)
    kk = jnp.concatenate([kp_ref[0], kc_ref[0]], axis=0)
    vv = jnp.concatenate([vp_ref[0], vc_ref[0]], axis=0)
    qi = lax.broadcasted_iota(jnp.int32, (win, 2 * win), 0)
    kj = lax.broadcasted_iota(jnp.int32, (win, 2 * win), 1)
    mask = (kj > qi) & (kj <= qi + win) & ((kj >= win) | (nb > 0))
    for g in range(group):
        q = q_ref[:, g * hd:(g + 1) * hd]
        s = lax.dot_general(q, kk, (((1,), (1,)), ((), ())), preferred_element_type=F32)
        sink = sink_ref[kvh * group + g]
        o_ref[:, g * hd:(g + 1) * hd] = _softmax_sink_pv(s, mask, sink, vv).astype(o_ref.dtype)


def _attn_prompt(q, kb, vb, sinks, *, batch, seq, kv_heads, group, hd, out_rows):
    win = WINDOW
    nblk = seq // win
    gw = group * hd
    prev = lambda b, n, h: (h, b * nblk + jnp.maximum(n - 1, 0), 0)
    cur = lambda b, n, h: (h, b * nblk + n, 0)
    return pl.pallas_call(
        functools.partial(_attn_prompt_kernel, group=group, hd=hd, win=win),
        grid=(batch, nblk, kv_heads),
        in_specs=[pl.BlockSpec(memory_space=pltpu.SMEM),
                  pl.BlockSpec((win, gw), lambda b, n, h: (b * nblk + n, h)),
                  pl.BlockSpec((1, win, hd), prev), pl.BlockSpec((1, win, hd), cur),
                  pl.BlockSpec((1, win, hd), prev), pl.BlockSpec((1, win, hd), cur)],
        out_specs=pl.BlockSpec((win, gw), lambda b, n, h: (b * nblk + n, h)),
        out_shape=jax.ShapeDtypeStruct((out_rows, kv_heads * gw), BF16),
        compiler_params=_cparams(("arbitrary", "arbitrary", "arbitrary")),
        name="attn_prompt",
    )(sinks, q, kb, kb, vb, vb)


def _attn_sample_kernel(sink_ref, q_ref, kn_ref, vn_ref, kc_ref, vc_ref, prev_ref, o_ref,
                        *, nbs, tq, kv_heads, group, hd, win):
    del prev_ref
    nctx = 2 * win
    r = lax.broadcasted_iota(jnp.int32, (group * tq, nctx), 0)
    kj = lax.broadcasted_iota(jnp.int32, (group * tq, nctx), 1)
    tpos = r % tq
    mask = ((kj < win) & (kj > tpos)) | ((kj >= win) & (kj - win <= tpos))
    pad = jnp.zeros((nctx - win - tq, hd), F32)
    for bi in range(nbs):
        qb = q_ref[bi * tq:(bi + 1) * tq, :].astype(F32)
        for h in range(kv_heads):
            cs = slice(h * hd, (h + 1) * hd)
            kctx = jnp.concatenate([kc_ref[bi][:, cs], kn_ref[bi * tq:(bi + 1) * tq, cs], pad], axis=0)
            vctx = jnp.concatenate([vc_ref[bi][:, cs], vn_ref[bi * tq:(bi + 1) * tq, cs], pad], axis=0)
            qg = jnp.concatenate([qb[:, (h * group + g) * hd:(h * group + g + 1) * hd]
                                  for g in range(group)], axis=0)
            s = _bdot_nt(qg, kctx)
            sink = jnp.concatenate([jnp.full((tq, 1), sink_ref[h * group + g], F32) for g in range(group)],
                                   axis=0)
            o = _softmax_sink_pv(s, mask, sink, vctx.astype(BF16))
            for g in range(group):
                hh = h * group + g
                o_ref[bi * tq:(bi + 1) * tq, hh * hd:(hh + 1) * hd] = o[g * tq:(g + 1) * tq, :].astype(o_ref.dtype)


def _attn_sample(q, kf, kv, cache_k, cache_v, sinks, o_prev, *, row0, batch, tq, kv_heads, group, hd, nbs):
    win = cache_k.shape[1]
    assert win == WINDOW and batch % nbs == 0 and row0 % (nbs * tq) == 0
    rows = nbs * tq
    rb0 = row0 // rows
    kvw = kv_heads * hd
    return pl.pallas_call(
        functools.partial(_attn_sample_kernel, nbs=nbs, tq=tq, kv_heads=kv_heads, group=group, hd=hd, win=win),
        grid=(batch // nbs,),
        in_specs=[pl.BlockSpec(memory_space=pltpu.SMEM),
                  pl.BlockSpec((rows, q.shape[1]), lambda b: (rb0 + b, 0)),
                  pl.BlockSpec((rows, kvw), lambda b: (rb0 + b, 0)),
                  pl.BlockSpec((rows, kvw), lambda b: (rb0 + b, 1)),
                  pl.BlockSpec((nbs, win, kvw), lambda b: (b, 0, 0)),
                  pl.BlockSpec((nbs, win, kvw), lambda b: (b, 0, 0)),
                  pl.BlockSpec(memory_space=pl.ANY)],
        out_specs=pl.BlockSpec((rows, q.shape[1]), lambda b: (rb0 + b, 0)),
        out_shape=jax.ShapeDtypeStruct(o_prev.shape, o_prev.dtype),
        input_output_aliases={6: 0},
        compiler_params=_cparams(("arbitrary",)),
        name="attn_sample",
    )(sinks, q, kf, kv, cache_k, cache_v, o_prev)


PAST_LEN = 16384
TM_NORM = 256
M_STEPS = 8


def _tiles(m, f):
    tm = m // M_STEPS
    assert m % M_STEPS == 0 and tm % 16 == 0
    tm_down = tm // 2 if tm % 32 == 0 else tm
    tk_down = f // 2
    assert tk_down % LANES == 0
    return dict(tm=tm, tn=512, tn_ffn=256, tm_down=tm_down, tn_down=512, tk_down=tk_down)


def _write_rows_kernel(src_ref, dst_ref, o_ref):
    del dst_ref
    o_ref[...] = src_ref[...]


def _merge_rows(dst, src, row0, tm):
    n = src.shape[1]
    rows = src.shape[0] - row0
    assert rows % tm == 0 and row0 % tm == 0
    return pl.pallas_call(
        _write_rows_kernel,
        grid=(rows // tm,),
        in_specs=[pl.BlockSpec((tm, n), lambda i: (row0 // tm + i, 0)),
                  pl.BlockSpec(memory_space=pl.ANY)],
        out_specs=pl.BlockSpec((tm, n), lambda i: (row0 // tm + i, 0)),
        out_shape=jax.ShapeDtypeStruct(dst.shape, dst.dtype),
        input_output_aliases={1: 0},
        compiler_params=_cparams(("arbitrary",)),
        name="merge_rows",
    )(src, dst)


def kernel(x_prompt, x_sample, state_ssm, state_conv, cache_win_k, cache_win_v, norm_mix, norm_ffn,
           a_w_in, a_conv_w, a_log, a_dt_bias, a_o_norm, a_w_out, kv_norm, w_kv, k_norm, b_w_q,
           b_q_norm, b_sinks, b_w_o, ffn_w_gu, ffn_w_down):
    bp, sp, d = x_prompt.shape
    bs, ss, _ = x_sample.shape
    mp, ms = bp * sp, bs * ss
    m = mp + ms
    n_a = a_w_in.shape[0]
    assert n_a == 1 and b_w_q.shape[0] == 1, "layer pattern is one delta layer then one attention layer"
    h_a = a_log.shape[1]
    dk = a_o_norm.shape[1]
    d_qk = h_a * dk
    kv_heads, hd = cache_win_k.shape[2], cache_win_k.shape[3]
    kvw = kv_heads * hd
    h_b = b_sinks.shape[1]
    group = h_b // kv_heads
    rot_dim = hd // 4
    tl = _tiles(m, ffn_w_down.shape[1])
    tm_mm = tl["tm"]

    def mm(xb, w, n_out, res=None):
        return _mm_wres(xb, w, n_out=n_out, tm=tm_mm, tn=min(tl["tn"], n_out), res=res)

    def ffn(hin, layer):
        (hn,) = _norm(hin, norm_ffn[layer:layer + 1], TM_NORM)
        act = _ffn_up(hn, ffn_w_gu[layer], tm=tm_mm, tn=tl["tn_ffn"])
        return _mm_ksplit(act, ffn_w_down[layer], hin, tm=tl["tm_down"], tn=min(tl["tn_down"], d),
                          tk=tl["tk_down"])

    xn, x = _norm_first(x_prompt.reshape(mp, d), x_sample.reshape(ms, d), norm_mix[0], TM_NORM)
    w_in = a_w_in[0]
    proj = mm(xn, w_in, 4 * d_qk)
    w_ba_t = jnp.transpose(w_in[:, 4 * d_qk:]).astype(BF16)
    c_s = math.gcd(ss, DELTA_CHUNK)
    gates = _gates(xn, w_ba_t, a_log[0], a_dt_bias[0], tm=TM_NORM, n_prompt_rows=mp,
                   c_prompt=DELTA_CHUNK, c_sample=c_s)
    conv_w = a_conv_w[0]
    conv0_p = jnp.zeros((bp, CONV_W - 1, 3 * d_qk), F32)
    o_p, ssm_p, conv_p = _delta_rule(proj, gates[:, :, :mp], conv_w, conv0_p, None, a_o_norm[0],
                                     row0=0, batch=bp, seq=sp, n_heads=h_a, dk=dk,
                                     C=DELTA_CHUNK, nb=1, nc=2, hb=8, out_rows=m)
    o_s, ssm_s, conv_s = _delta_rule(proj, gates[:, :, mp:], conv_w, state_conv[0], state_ssm[0], a_o_norm[0],
                                     row0=mp, batch=bs, seq=ss, n_heads=h_a, dk=dk,
                                     C=c_s, nb=2, nc=1, hb=8, out_rows=m)
    o_a = _merge_rows(o_p, o_s, mp, TM_NORM)
    h1 = mm(o_a, a_w_out[0], d, res=x)
    h2 = ffn(h1, 0)

    hkv, hq = _norm(h2, jnp.stack([kv_norm, norm_mix[1]]), TM_NORM)
    kv = mm(hkv, w_kv, 2 * kvw)
    pos = jnp.concatenate([jnp.tile(jnp.arange(sp, dtype=F32), bp),
                           jnp.tile(PAST_LEN + jnp.arange(ss, dtype=F32), bs)])
    cos_t, sa_t, sb_t = _rope_tables(pos, hd, rot_dim)
    gmat = _group_matrix(kvw, hd)
    kf, kb, vb = pl.pallas_call(
        functools.partial(_kv_post_kernel, hd=hd, half=rot_dim // 2),
        grid=(m // TM_NORM,),
        in_specs=[pl.BlockSpec((TM_NORM, kvw), lambda i: (i, 0)),
                  pl.BlockSpec((TM_NORM, kvw), lambda i: (i, 1)),
                  pl.BlockSpec((1, kvw), lambda i: (0, 0)),
                  pl.BlockSpec((kvw, kvw), lambda i: (0, 0)),
                  *[pl.BlockSpec((TM_NORM, LANES), lambda i: (i, 0)) for _ in range(3)]],
        out_specs=[pl.BlockSpec((TM_NORM, kvw), lambda i: (i, 0)),
                   pl.BlockSpec((kv_heads, TM_NORM, hd), lambda i: (0, i, 0)),
                   pl.BlockSpec((kv_heads, TM_NORM, hd), lambda i: (0, i, 0))],
        out_shape=[jax.ShapeDtypeStruct((m, kvw), F32),
                   jax.ShapeDtypeStruct((kv_heads, m, hd), BF16),
                   jax.ShapeDtypeStruct((kv_heads, m, hd), BF16)],
        compiler_params=_cparams(("arbitrary",)),
        name="kv_post",
    )(kv, kv, jnp.tile(k_norm, kv_heads).reshape(1, kvw), gmat, cos_t, sa_t, sb_t)

    qraw = mm(hq, b_w_q[0], h_b * hd)
    qw = kvw
    q = pl.pallas_call(
        functools.partial(_q_post_kernel, hd=hd, half=rot_dim // 2, scale=hd ** -0.5),
        grid=(m // TM_NORM, h_b * hd // qw),
        in_specs=[pl.BlockSpec((TM_NORM, qw), lambda i, j: (i, j)),
                  pl.BlockSpec((1, qw), lambda i, j: (0, 0)),
                  pl.BlockSpec((qw, qw), lambda i, j: (0, 0)),
                  *[pl.BlockSpec((TM_NORM, LANES), lambda i, j: (i, 0)) for _ in range(3)]],
        out_specs=pl.BlockSpec((TM_NORM, qw), lambda i, j: (i, j)),
        out_shape=jax.ShapeDtypeStruct((m, h_b * hd), BF16),
        compiler_params=_cparams(("arbitrary", "arbitrary")),
        name="q_post",
    )(qraw, jnp.tile(b_q_norm[0], qw // hd).reshape(1, qw), gmat, cos_t, sa_t, sb_t)
    sinks = b_sinks[0]
    o_b = _attn_prompt(q, kb, vb, sinks, batch=bp, seq=sp, kv_heads=kv_heads, group=group, hd=hd, out_rows=m)
    o_b = _attn_sample(q, kf, kv, cache_win_k.reshape(bs, WINDOW, kvw), cache_win_v.reshape(bs, WINDOW, kvw),
                       sinks, o_b, row0=mp, batch=bs, tq=ss, kv_heads=kv_heads, group=group, hd=hd, nbs=2)
    h3 = mm(o_b, b_w_o[0], d, res=h2)
    h4 = ffn(h3, 1)

    wb = min(WINDOW, sp)
    kf_p = kf[:mp].reshape(bp, sp, kv_heads, hd)[:, sp - wb:]
    vf_p = kv[:mp, kvw:].reshape(bp, sp, kv_heads, hd)[:, sp - wb:]
    kf_s = kf[mp:].reshape(bs, ss, kv_heads, hd)
    vf_s = kv[mp:, kvw:].reshape(bs, ss, kv_heads, hd)
    wk_s = jnp.concatenate([cache_win_k[:, ss:], kf_s], axis=1)
    wv_s = jnp.concatenate([cache_win_v[:, ss:], vf_s], axis=1)
    return (h4[:mp].reshape(bp, sp, d), h4[mp:].reshape(bs, ss, d),
            ssm_p[None], conv_p[None], kf_p, vf_p, ssm_s[None], conv_s[None], wk_s, wv_s)
```

```python
import functools
import math

import jax
import jax.numpy as jnp
from jax import lax
from jax.experimental import pallas as pl
from jax.experimental.pallas import tpu as pltpu

F32 = jnp.float32
BF16 = jnp.bfloat16

EPS = 1e-6
WINDOW = 128
ROPE_THETA = 500000.0
CONV_W = 4
DELTA_CHUNK = 64
LANES = 128
CONV_PAD = 8
VMEM_LIMIT = 56 * 1024 * 1024


def _cparams(sem):
    return pltpu.CompilerParams(dimension_semantics=sem, vmem_limit_bytes=VMEM_LIMIT)


def _sigmoid(x):
    return 1.0 / (1.0 + jnp.exp(-x))


def _bdot(a, b):
    return jnp.dot(a.astype(BF16), b.astype(BF16), preferred_element_type=F32)


def _bdot_nt(a, b):
    return lax.dot_general(a.astype(BF16), b.astype(BF16), (((1,), (1,)), ((), ())),
                           preferred_element_type=F32)


def _bdot_tn(a, b):
    return lax.dot_general(a.astype(BF16), b.astype(BF16), (((0,), (0,)), ((), ())),
                           preferred_element_type=F32)


def _norm_first_kernel(xp_ref, xs_ref, w_ref, xn_ref, xcat_ref, *, n_prompt_blocks):
    i = pl.program_id(0)

    def emit(x):
        r = lax.rsqrt(jnp.mean(x * x, axis=-1, keepdims=True) + EPS)
        xn_ref[...] = (x * r * w_ref[...]).astype(xn_ref.dtype)
        xcat_ref[...] = x

    @pl.when(i < n_prompt_blocks)
    def _():
        emit(xp_ref[...])

    @pl.when(i >= n_prompt_blocks)
    def _():
        emit(xs_ref[...])


def _norm_first(xp, xs, w, tm):
    mp, d = xp.shape
    ms = xs.shape[0]
    assert mp % tm == 0 and ms == tm
    npb = mp // tm
    m = mp + ms
    return pl.pallas_call(
        functools.partial(_norm_first_kernel, n_prompt_blocks=npb),
        grid=(npb + 1,),
        in_specs=[pl.BlockSpec((tm, d), lambda i: (jnp.minimum(i, npb - 1), 0)),
                  pl.BlockSpec((tm, d), lambda i: (0, 0)),
                  pl.BlockSpec((1, d), lambda i: (0, 0))],
        out_specs=[pl.BlockSpec((tm, d), lambda i: (i, 0)),
                   pl.BlockSpec((tm, d), lambda i: (i, 0))],
        out_shape=[jax.ShapeDtypeStruct((m, d), BF16), jax.ShapeDtypeStruct((m, d), F32)],
        compiler_params=_cparams(("arbitrary",)),
        name="norm_first",
    )(xp, xs, w.reshape(1, d))


def _norm_kernel(x_ref, w_ref, *o_refs):
    x = x_ref[...]
    y = x * lax.rsqrt(jnp.mean(x * x, axis=-1, keepdims=True) + EPS)
    for i, o_ref in enumerate(o_refs):
        o_ref[...] = (y * w_ref[i:i + 1, :]).astype(o_ref.dtype)


def _norm(x, ws, tm):
    m, d = x.shape
    nw = ws.shape[0]
    assert m % tm == 0
    return pl.pallas_call(
        _norm_kernel,
        grid=(m // tm,),
        in_specs=[pl.BlockSpec((tm, d), lambda i: (i, 0)),
                  pl.BlockSpec((nw, d), lambda i: (0, 0))],
        out_specs=[pl.BlockSpec((tm, d), lambda i: (i, 0)) for _ in range(nw)],
        out_shape=[jax.ShapeDtypeStruct((m, d), BF16) for _ in range(nw)],
        compiler_params=_cparams(("arbitrary",)),
        name="norm",
    )(x, ws)


def _first_rows(j, i):
    return jnp.where(j == 0, 0, i)


def _mm_stream_kernel(*refs, has_res):
    if has_res:
        x_ref, w_ref, res_ref, o_ref, wbf_ref = refs
    else:
        x_ref, w_ref, o_ref, wbf_ref = refs
    j = pl.program_id(0)
    i = pl.program_id(1)
    rows_c = w_ref.shape[0]
    slot = j % 2
    wbf_ref[slot, pl.ds(pl.multiple_of(i * rows_c, rows_c), rows_c), :] = w_ref[...].astype(BF16)

    @pl.when(j > 0)
    def _():
        acc = jnp.dot(x_ref[...], wbf_ref[1 - slot], preferred_element_type=F32)
        if has_res:
            acc = acc + res_ref[...]
        o_ref[...] = acc.astype(o_ref.dtype)


def _mm_stream(x, w, layer, *, n_out, tn, n_m, kblock=0, n_kblocks=1, res=None, out_dtype=F32):
    m = x.shape[0]
    k = x.shape[1] // n_kblocks
    tm, rows_c, nj = m // n_m, k // n_m, n_out // tn
    assert m % n_m == 0 and tm % 16 == 0 and k % n_m == 0 and rows_c % 16 == 0 and n_out % tn == 0
    assert w.shape[1] == k * n_kblocks == x.shape[1] and k % LANES == 0
    in_specs = [pl.BlockSpec((tm, k), lambda j, i: (_first_rows(j, i), kblock)),
                pl.BlockSpec((None, rows_c, tn), lambda j, i: (layer, kblock * n_m + i, jnp.minimum(j, nj - 1)))]
    args = [x, w]
    out_map = lambda j, i: (_first_rows(j, i), jnp.maximum(j - 1, 0))
    if res is not None:
        in_specs.append(pl.BlockSpec((tm, tn), out_map))
        args.append(res)
    return pl.pallas_call(
        functools.partial(_mm_stream_kernel, has_res=res is not None),
        grid=(nj + 1, n_m),
        in_specs=in_specs,
        out_specs=pl.BlockSpec((tm, tn), out_map),
        out_shape=jax.ShapeDtypeStruct((m, n_out), out_dtype),
        scratch_shapes=[pltpu.VMEM((2, k, tn), BF16)],
        compiler_params=_cparams(("arbitrary", "arbitrary")),
        name="mm_stream",
    )(*args)


FFN_PAIRS = 2


def _ffn_up_kernel(x_ref, *refs, tn):
    w_refs, (o_ref, wbf_ref) = refs[:2 * FFN_PAIRS], refs[2 * FFN_PAIRS:]
    j = pl.program_id(0)
    i = pl.program_id(1)
    rows_c = w_refs[0].shape[0]
    slot = j % 2
    rows = pl.ds(pl.multiple_of(i * rows_c, rows_c), rows_c)
    for c, w_ref in enumerate(w_refs):
        wbf_ref[slot, rows, c * tn:(c + 1) * tn] = w_ref[...].astype(BF16)

    @pl.when(j > 0)
    def _():
        gu = jnp.dot(x_ref[...], wbf_ref[1 - slot], preferred_element_type=F32)
        g = gu[:, :FFN_PAIRS * tn]
        o_ref[...] = (g * _sigmoid(g) * gu[:, FFN_PAIRS * tn:]).astype(o_ref.dtype)


def _ffn_up(x, w_gu, layer, *, n_m, tn):
    m, k = x.shape
    f = w_gu.shape[2] // 2
    tm, rows_c = m // n_m, k // n_m
    assert m % n_m == 0 and tm % 16 == 0 and k % n_m == 0 and rows_c % 16 == 0 and f % tn == 0
    n_pairs = f // tn
    nj = pl.cdiv(n_pairs, FFN_PAIRS)

    def w_map(c):
        g, part = c % FFN_PAIRS, c // FFN_PAIRS

        def index(j, i):
            pair = jnp.minimum(jnp.minimum(j, nj - 1) * FFN_PAIRS + g, n_pairs - 1)
            return layer, i, part * n_pairs + pair
        return index

    out_map = lambda j, i: (_first_rows(j, i), jnp.maximum(j - 1, 0))
    return pl.pallas_call(
        functools.partial(_ffn_up_kernel, tn=tn),
        grid=(nj + 1, n_m),
        in_specs=[pl.BlockSpec((tm, k), lambda j, i: (_first_rows(j, i), 0)),
                  *[pl.BlockSpec((None, rows_c, tn), w_map(c)) for c in range(2 * FFN_PAIRS)]],
        out_specs=pl.BlockSpec((tm, FFN_PAIRS * tn), out_map),
        out_shape=jax.ShapeDtypeStruct((m, f), BF16),
        scratch_shapes=[pltpu.VMEM((2, k, 2 * FFN_PAIRS * tn), BF16)],
        compiler_params=_cparams(("arbitrary", "arbitrary")),
        name="ffn_up",
    )(x, *([w_gu] * (2 * FFN_PAIRS)))


def _gates_kernel(x_ref, w_ref, alog_ref, dtb_ref, o_ref, *, n_heads, n_prompt_blocks, c_prompt, c_sample):
    tm = x_ref.shape[0]
    ba = jnp.dot(x_ref[...], w_ref[...], preferred_element_type=F32)
    a = ba + dtb_ref[...]
    softplus = jnp.maximum(a, 0.0) + jnp.log(1.0 + jnp.exp(-jnp.abs(a)))
    lane = lax.broadcasted_iota(jnp.int32, ba.shape, 1)
    bg = jnp.where(lane < n_heads, _sigmoid(ba), -jnp.exp(alog_ref[...]) * softplus).T
    beta = bg[:n_heads]
    g = bg[n_heads:2 * n_heads]
    shift = jnp.where(pl.program_id(0) < n_prompt_blocks,
                      int(math.log2(c_prompt)), int(math.log2(c_sample)))
    jj = lax.broadcasted_iota(jnp.int32, (tm, tm), 0)
    ii = lax.broadcasted_iota(jnp.int32, (tm, tm), 1)
    same = lax.shift_right_logical(jj, shift) == lax.shift_right_logical(ii, shift)
    cum_m = jnp.where(same & (jj <= ii), 1.0, 0.0).astype(F32)
    tot_m = jnp.where(same, 1.0, 0.0).astype(F32)
    gcum = jnp.dot(g, cum_m, preferred_element_type=F32, precision=lax.Precision.HIGHEST)
    glast = jnp.dot(g, tot_m, preferred_element_type=F32, precision=lax.Precision.HIGHEST)
    o_ref[0] = beta
    o_ref[1] = jnp.exp(gcum)
    o_ref[2] = jnp.exp(glast - gcum)
    o_ref[3] = gcum


def _gates(xn, w_ba, a_log, dt_bias, *, tm, n_prompt_rows, c_prompt, c_sample):
    m, k = xn.shape
    h = a_log.shape[0]
    assert m % tm == 0 and n_prompt_rows % tm == 0 and tm % c_prompt == 0 and tm % c_sample == 0
    assert w_ba.shape == (k, LANES) and 2 * h <= LANES
    return pl.pallas_call(
        functools.partial(_gates_kernel, n_heads=h, n_prompt_blocks=n_prompt_rows // tm,
                          c_prompt=c_prompt, c_sample=c_sample),
        grid=(m // tm,),
        in_specs=[pl.BlockSpec((tm, k), lambda i: (i, 0)),
                  pl.BlockSpec((k, LANES), lambda i: (0, 0)),
                  pl.BlockSpec((1, LANES), lambda i: (0, 0)),
                  pl.BlockSpec((1, LANES), lambda i: (0, 0))],
        out_specs=pl.BlockSpec((4, h, tm), lambda i: (0, 0, i)),
        out_shape=jax.ShapeDtypeStruct((4, h, m), F32),
        compiler_params=_cparams(("arbitrary",)),
        name="gates",
    )(xn, w_ba, jnp.pad(a_log, (h, LANES - 2 * h)).reshape(1, LANES),
      jnp.pad(dt_bias, (h, LANES - 2 * h)).reshape(1, LANES))


def _delta_kernel(q_ref, k_ref, v_ref, z_ref, cwq_ref, cwk_ref, cwv_ref, csq_ref, csk_ref, csv_ref,
                  gcol_ref, grow_ref, h0_ref, onw_ref,
                  o_ref, hout_ref, cqo_ref, cko_ref, cvo_ref,
                  h_scr, xq_scr, xk_scr, xv_scr, aq_scr, ak_scr, av_scr,
                  *, C, nb, nc, hb, dk, zero_init):
    t = pl.program_id(2)
    n_t = pl.num_programs(2)
    ttb = nc * C
    hist0 = CONV_PAD - (CONV_W - 1)

    @pl.when(t == 0)
    def _():
        if zero_init:
            h_scr[...] = jnp.zeros(h_scr.shape, F32)
        else:
            h_scr[...] = h0_ref[...]
        for xs, cs in ((xq_scr, csq_ref), (xk_scr, csk_ref), (xv_scr, csv_ref)):
            for bi in range(nb):
                xs[bi, hist0:CONV_PAD, :] = cs[bi]

    for raw_ref, xs, cw_ref, act in ((q_ref, xq_scr, cwq_ref, aq_scr), (k_ref, xk_scr, cwk_ref, ak_scr),
                                     (v_ref, xv_scr, cwv_ref, av_scr)):
        for bi in range(nb):
            xs[bi, CONV_PAD:CONV_PAD + ttb, :] = raw_ref[bi * ttb:(bi + 1) * ttb, :]
            acc = xs[bi, hist0:hist0 + ttb, :] * cw_ref[0:1, :]
            for i in range(1, CONV_W):
                acc = acc + xs[bi, hist0 + i:hist0 + i + ttb, :] * cw_ref[i:i + 1, :]
            act[bi * ttb:(bi + 1) * ttb, :] = acc * _sigmoid(acc)
            xs[bi, hist0:CONV_PAD, :] = xs[bi, CONV_PAD + ttb - (CONV_W - 1):CONV_PAD + ttb, :]

    @pl.when(t == n_t - 1)
    def _():
        for xs, co in ((xq_scr, cqo_ref), (xk_scr, cko_ref), (xv_scr, cvo_ref)):
            for bi in range(nb):
                co[bi] = xs[bi, hist0:CONV_PAD, :]

    ii = lax.broadcasted_iota(jnp.int32, (C, C), 0)
    jj = lax.broadcasted_iota(jnp.int32, (C, C), 1)
    n_levels = int(math.log2(C))
    assert 2 ** n_levels == C

    items = [(bi, c, j) for bi in range(nb) for c in range(nc) for j in range(hb)]

    def prepare(bi, c, j):
        rows = slice(bi * ttb + c * C, bi * ttb + (c + 1) * C)
        cols = slice(j * dk, (j + 1) * dk)
        q = aq_scr[rows, cols]
        k = ak_scr[rows, cols]
        qn = q * (lax.rsqrt(jnp.sum(q * q, axis=-1, keepdims=True) + EPS) * (dk ** -0.5))
        kn = k * lax.rsqrt(jnp.sum(k * k, axis=-1, keepdims=True) + EPS)
        gcb = gcol_ref[0, rows, :]
        beta = gcb[:, j:j + 1]
        eg = gcb[:, hb + j:hb + j + 1]
        ekl = gcb[:, 2 * hb + j:2 * hb + j + 1]
        gc = gcb[:, 3 * hb + j:3 * hb + j + 1]
        gr = grow_ref[0, bi * nc + c, j:j + 1, :]
        dmat = jnp.where(ii >= jj, jnp.exp(gc - gr), 0.0)
        rhs = jnp.concatenate([kn * (beta * eg), av_scr[rows, cols] * beta], axis=1)
        return dict(rows=rows, cols=cols, qn=qn, kn=kn, beta=beta, eg=eg, dmat=dmat, rhs=rhs,
                    qe=qn * eg, kd=kn * ekl)

    st = [prepare(*it) for it in items]
    for e in st:
        e["s"] = _bdot_nt(jnp.concatenate([e["kn"], e["qn"]], axis=0), e["kn"])
    for e in st:
        s = e.pop("s")
        e["qk"] = s[C:] * e["dmat"]
        e["p"] = jnp.where(ii > jj, s[:C] * e["dmat"] * (-e["beta"]), 0.0)
        e["t"] = e["p"]
    for e in st:
        e["p"] = _bdot(e["p"], e["p"])
    for lvl in range(n_levels - 1):
        last = lvl == n_levels - 2
        for e in st:
            e["tp"] = _bdot(e["t"], e["p"])
            if not last:
                e["p2"] = _bdot(e["p"], e["p"])
        for e in st:
            e["t"] = e["t"] + e["p"] + e.pop("tp")
            if not last:
                e["p"] = e.pop("p2")
    for e in st:
        e["wu"] = e["rhs"] + _bdot(e["t"], e["rhs"])
    h_cur = {(bi, j): h_scr[bi, j] for bi in range(nb) for j in range(hb)}
    for c in range(nc):
        sel = [(it, e) for it, e in zip(items, st) if it[1] == c]
        for (bi, _, j), e in sel:
            e["wq"] = _bdot(jnp.concatenate([e["wu"][:, :dk], e["qe"]], axis=0), h_cur[bi, j])
        for _, e in sel:
            e["u"] = e["wu"][:, dk:] - e["wq"][:C]
        for (bi, _, j), e in sel:
            e["o"] = e["wq"][C:] + _bdot(e["qk"], e["u"])
            egl = e["eg"][C - 1:C, :]
            h_cur[bi, j] = h_cur[bi, j] * egl + _bdot_tn(e["kd"], e["u"])
        for _, e in sel:
            o = e["o"]
            on = o * lax.rsqrt(jnp.mean(o * o, axis=-1, keepdims=True) + EPS) * onw_ref[...]
            zz = z_ref[e["rows"], e["cols"]]
            o_ref[e["rows"], e["cols"]] = (on * (zz * _sigmoid(zz))).astype(o_ref.dtype)
    for (bi, j), hv in h_cur.items():
        h_scr[bi, j] = hv

    @pl.when(t == n_t - 1)
    def _():
        hout_ref[...] = h_scr[...]


def _delta_rule(proj, gates, conv_w, conv0, h0, o_norm_w, *, row0, batch, seq, n_heads, dk,
                C, nb, nc, hb, out_rows):
    ttb = nc * C
    tt = nb * ttb
    w = hb * dk
    ng = n_heads // hb
    n_t = seq // ttb
    assert seq % ttb == 0 and batch % nb == 0 and n_heads % hb == 0 and row0 % tt == 0
    zero_init = h0 is None
    d_qk = n_heads * dk
    cb = d_qk // w
    rb0 = row0 // tt

    rows = batch * seq
    g5 = gates.reshape(4, ng, hb, rows)
    gcol = jnp.transpose(g5, (1, 3, 0, 2)).reshape(ng, rows, 4 * hb)
    grow = jnp.transpose(g5[3].reshape(ng, hb, rows // C, C), (0, 2, 1, 3))

    def tok_map(part):
        return lambda b, g, t: (rb0 + b * n_t + t, part * cb + g)

    if zero_init:
        h0_arg = jnp.zeros((nb, hb, dk, dk), F32)
        h0_spec = pl.BlockSpec((nb, hb, dk, dk), lambda b, g, t: (0, 0, 0, 0))
    else:
        h0_arg = h0
        h0_spec = pl.BlockSpec((nb, hb, dk, dk), lambda b, g, t: (b, g, 0, 0))
    cs_spec = [pl.BlockSpec((nb, CONV_W - 1, w), (lambda b, g, t, p=p: (b, 0, p * cb + g))) for p in range(3)]
    cw_spec = [pl.BlockSpec((CONV_W, w), (lambda b, g, t, p=p: (0, p * cb + g))) for p in range(3)]
    kern = functools.partial(_delta_kernel, C=C, nb=nb, nc=nc, hb=hb, dk=dk, zero_init=zero_init)
    o, h_fin, cq, ck, cv = pl.pallas_call(
        kern,
        grid=(batch // nb, ng, n_t),
        in_specs=[pl.BlockSpec((tt, w), tok_map(0)), pl.BlockSpec((tt, w), tok_map(1)),
                  pl.BlockSpec((tt, w), tok_map(2)), pl.BlockSpec((tt, w), tok_map(3)),
                  *cw_spec, *cs_spec,
                  pl.BlockSpec((1, tt, 4 * hb), lambda b, g, t: (g, b * n_t + t, 0)),
                  pl.BlockSpec((1, nb * nc, hb, C), lambda b, g, t: (g, b * n_t + t, 0, 0)),
                  h0_spec,
                  pl.BlockSpec((1, dk), lambda b, g, t: (0, 0))],
        out_specs=[pl.BlockSpec((tt, w), lambda b, g, t: (rb0 + b * n_t + t, g)),
                   pl.BlockSpec((nb, hb, dk, dk), lambda b, g, t: (b, g, 0, 0)),
                   *[pl.BlockSpec((nb, CONV_W - 1, w), lambda b, g, t: (b, 0, g)) for _ in range(3)]],
        out_shape=[jax.ShapeDtypeStruct((out_rows, d_qk), BF16),
                   jax.ShapeDtypeStruct((batch, n_heads, dk, dk), F32),
                   *[jax.ShapeDtypeStruct((batch, CONV_W - 1, d_qk), F32) for _ in range(3)]],
        scratch_shapes=[pltpu.VMEM((nb, hb, dk, dk), F32),
                        *[pltpu.VMEM((nb, CONV_PAD + ttb, w), F32) for _ in range(3)],
                        *[pltpu.VMEM((tt, w), F32) for _ in range(3)]],
        compiler_params=_cparams(("arbitrary", "arbitrary", "arbitrary")),
        name="delta_rule",
    )(proj, proj, proj, proj, conv_w, conv_w, conv_w, conv0, conv0, conv0, gcol, grow, h0_arg,
      o_norm_w.reshape(1, dk))
    return o, h_fin, jnp.concatenate([cq, ck, cv], axis=-1)


def _group_mean_sq(x, gmat_ref, inv_n):
    x2 = x * x
    hi = x2.astype(BF16)
    lo = (x2 - hi.astype(F32)).astype(BF16)
    g = gmat_ref[...]
    return (jnp.dot(hi, g, preferred_element_type=F32) + jnp.dot(lo, g, preferred_element_type=F32)) * inv_n


def _rope(y, cos_ref, sa_ref, sb_ref, half):
    w = y.shape[1]
    reps = w // cos_ref.shape[1]
    tile = lambda r: jnp.concatenate([r[...]] * reps, axis=1)
    return (y * tile(cos_ref) + pltpu.roll(y, w - half, axis=1) * tile(sa_ref)
            + pltpu.roll(y, half, axis=1) * tile(sb_ref))


def _q_post_kernel(x_ref, w_ref, gmat_ref, cos_ref, sa_ref, sb_ref, o_ref, *, hd, half, scale):
    x = x_ref[...]
    y = x * lax.rsqrt(_group_mean_sq(x, gmat_ref, 1.0 / hd) + EPS) * w_ref[...]
    o_ref[...] = (_rope(y, cos_ref, sa_ref, sb_ref, half) * scale).astype(o_ref.dtype)


def _kv_post_kernel(k_ref, v_ref, w_ref, gmat_ref, cos_ref, sa_ref, sb_ref, kf_ref, kb_ref, vb_ref,
                    *, hd, half):
    x = k_ref[...]
    y = x * lax.rsqrt(_group_mean_sq(x, gmat_ref, 1.0 / hd) + EPS) * w_ref[...]
    kf = _rope(y, cos_ref, sa_ref, sb_ref, half)
    kf_ref[...] = kf
    v = v_ref[...]
    for hh in range(kb_ref.shape[0]):
        kb_ref[hh] = kf[:, hh * hd:(hh + 1) * hd].astype(BF16)
        vb_ref[hh] = v[:, hh * hd:(hh + 1) * hd].astype(BF16)


def _rope_tables(pos, hd, rot_dim):
    half = rot_dim // 2
    inv_freq = jnp.power(ROPE_THETA, -jnp.arange(half, dtype=F32) * 2.0 / rot_dim)
    ang = pos[:, None] * inv_freq[None, :]
    cos, sin = jnp.cos(ang), jnp.sin(ang)
    m = pos.shape[0]
    ones = jnp.ones((m, hd - rot_dim), F32)
    zeros = jnp.zeros((m, hd - rot_dim), F32)
    zh = jnp.zeros((m, half), F32)
    c = jnp.concatenate([cos, cos, ones], axis=1)
    sa = jnp.concatenate([-sin, zh, zeros], axis=1)
    sb = jnp.concatenate([zh, sin, zeros], axis=1)
    reps = LANES // hd
    return tuple(jnp.tile(a, (1, reps)) for a in (c, sa, sb))


def _group_matrix(width, hd):
    r = jnp.arange(width) // hd
    return (r[:, None] == r[None, :]).astype(BF16)


def _softmax_sink(s, mask, sink):
    s = jnp.where(mask, s, -jnp.inf)
    m = jnp.maximum(jnp.max(s, axis=-1, keepdims=True), sink)
    p = jnp.exp(s - m)
    denom = jnp.sum(p, axis=-1, keepdims=True) + jnp.exp(sink - m)
    return (p / denom).astype(BF16)


def _attn_prompt_kernel(sink_ref, q_ref, kp_ref, kc_ref, vp_ref, vc_ref, o_ref, *, group, hd, win):
    nb = pl.program_id(1)
    kvh = pl.program_id(2)
    kk = jnp.concatenate([kp_ref[0], kc_ref[0]], axis=0)
    vv = jnp.concatenate([vp_ref[0], vc_ref[0]], axis=0)
    qi = lax.broadcasted_iota(jnp.int32, (win, 2 * win), 0)
    kj = lax.broadcasted_iota(jnp.int32, (win, 2 * win), 1)
    mask = (kj > qi) & (kj <= qi + win) & ((kj >= win) | (nb > 0))
    ss = [lax.dot_general(q_ref[:, g * hd:(g + 1) * hd], kk, (((1,), (1,)), ((), ())),
                          preferred_element_type=F32) for g in range(group)]
    ps = [_softmax_sink(s, mask, sink_ref[kvh * group + g]) for g, s in enumerate(ss)]
    for g, p in enumerate(ps):
        o_ref[:, g * hd:(g + 1) * hd] = jnp.dot(p, vv, preferred_element_type=F32).astype(o_ref.dtype)


def _attn_prompt(q, kb, vb, sinks, *, batch, seq, kv_heads, group, hd, out_rows):
    win = WINDOW
    nblk = seq // win
    gw = group * hd
    prev = lambda b, n, h: (h, b * nblk + jnp.maximum(n - 1, 0), 0)
    cur = lambda b, n, h: (h, b * nblk + n, 0)
    return pl.pallas_call(
        functools.partial(_attn_prompt_kernel, group=group, hd=hd, win=win),
        grid=(batch, nblk, kv_heads),
        in_specs=[pl.BlockSpec(memory_space=pltpu.SMEM),
                  pl.BlockSpec((win, gw), lambda b, n, h: (b * nblk + n, h)),
                  pl.BlockSpec((1, win, hd), prev), pl.BlockSpec((1, win, hd), cur),
                  pl.BlockSpec((1, win, hd), prev), pl.BlockSpec((1, win, hd), cur)],
        out_specs=pl.BlockSpec((win, gw), lambda b, n, h: (b * nblk + n, h)),
        out_shape=jax.ShapeDtypeStruct((out_rows, kv_heads * gw), BF16),
        compiler_params=_cparams(("arbitrary", "arbitrary", "arbitrary")),
        name="attn_prompt",
    )(sinks, q, kb, kb, vb, vb)


def _attn_sample_kernel(sink_ref, q_ref, kn_ref, vn_ref, kc_ref, vc_ref, prev_ref, o_ref,
                        *, nbs, tq, kv_heads, group, hd, win):
    del prev_ref
    nctx = 2 * win
    r = lax.broadcasted_iota(jnp.int32, (group * tq, nctx), 0)
    kj = lax.broadcasted_iota(jnp.int32, (group * tq, nctx), 1)
    tpos = r % tq
    mask = ((kj < win) & (kj > tpos)) | ((kj >= win) & (kj - win <= tpos))
    pad = jnp.zeros((nctx - win - tq, hd), F32)
    items = [(bi, h) for bi in range(nbs) for h in range(kv_heads)]
    qbs = [q_ref[bi * tq:(bi + 1) * tq, :].astype(F32) for bi in range(nbs)]
    sinks = [jnp.concatenate([jnp.full((tq, 1), sink_ref[h * group + g], F32) for g in range(group)], axis=0)
             for h in range(kv_heads)]
    ss, vs = [], []
    for bi, h in items:
        cs = slice(h * hd, (h + 1) * hd)
        trows = slice(bi * tq, (bi + 1) * tq)
        kctx = jnp.concatenate([kc_ref[bi][:, cs], kn_ref[trows, cs], pad], axis=0)
        vs.append(jnp.concatenate([vc_ref[bi][:, cs], vn_ref[trows, cs], pad], axis=0).astype(BF16))
        qg = jnp.concatenate([qbs[bi][:, (h * group + g) * hd:(h * group + g + 1) * hd]
                              for g in range(group)], axis=0)
        ss.append(_bdot_nt(qg, kctx))
    ps = [_softmax_sink(s, mask, sinks[h]) for (bi, h), s in zip(items, ss)]
    for (bi, h), p, v in zip(items, ps, vs):
        o = jnp.dot(p, v, preferred_element_type=F32)
        for g in range(group):
            hh = h * group + g
            o_ref[bi * tq:(bi + 1) * tq, hh * hd:(hh + 1) * hd] = o[g * tq:(g + 1) * tq, :].astype(o_ref.dtype)


def _attn_sample(q, kf, kv, cache_k, cache_v, sinks, o_prev, *, row0, batch, tq, kv_heads, group, hd, nbs):
    win = cache_k.shape[1]
    assert win == WINDOW and batch % nbs == 0 and row0 % (nbs * tq) == 0
    rows = nbs * tq
    rb0 = row0 // rows
    kvw = kv_heads * hd
    return pl.pallas_call(
        functools.partial(_attn_sample_kernel, nbs=nbs, tq=tq, kv_heads=kv_heads, group=group, hd=hd, win=win),
        grid=(batch // nbs,),
        in_specs=[pl.BlockSpec(memory_space=pltpu.SMEM),
                  pl.BlockSpec((rows, q.shape[1]), lambda b: (rb0 + b, 0)),
                  pl.BlockSpec((rows, kvw), lambda b: (rb0 + b, 0)),
                  pl.BlockSpec((rows, kvw), lambda b: (rb0 + b, 1)),
                  pl.BlockSpec((nbs, win, kvw), lambda b: (b, 0, 0)),
                  pl.BlockSpec((nbs, win, kvw), lambda b: (b, 0, 0)),
                  pl.BlockSpec(memory_space=pl.ANY)],
        out_specs=pl.BlockSpec((rows, q.shape[1]), lambda b: (rb0 + b, 0)),
        out_shape=jax.ShapeDtypeStruct(o_prev.shape, o_prev.dtype),
        input_output_aliases={6: 0},
        compiler_params=_cparams(("arbitrary",)),
        name="attn_sample",
    )(sinks, q, kf, kv, cache_k, cache_v, o_prev)


PAST_LEN = 16384
TM_NORM = 256
M_STEPS = 8


TN_WIDE = 1024
TN_RES = 512
TN_FFN = 256
DOWN_KBLOCKS = 2


def _write_rows_kernel(src_ref, dst_ref, o_ref):
    del dst_ref
    o_ref[...] = src_ref[...]


def _merge_rows(dst, src, row0, tm):
    n = src.shape[1]
    rows = src.shape[0] - row0
    assert rows % tm == 0 and row0 % tm == 0
    return pl.pallas_call(
        _write_rows_kernel,
        grid=(rows // tm,),
        in_specs=[pl.BlockSpec((tm, n), lambda i: (row0 // tm + i, 0)),
                  pl.BlockSpec(memory_space=pl.ANY)],
        out_specs=pl.BlockSpec((tm, n), lambda i: (row0 // tm + i, 0)),
        out_shape=jax.ShapeDtypeStruct(dst.shape, dst.dtype),
        input_output_aliases={1: 0},
        compiler_params=_cparams(("arbitrary",)),
        name="merge_rows",
    )(src, dst)


def kernel(x_prompt, x_sample, state_ssm, state_conv, cache_win_k, cache_win_v, norm_mix, norm_ffn,
           a_w_in, a_conv_w, a_log, a_dt_bias, a_o_norm, a_w_out, kv_norm, w_kv, k_norm, b_w_q,
           b_q_norm, b_sinks, b_w_o, ffn_w_gu, ffn_w_down):
    bp, sp, d = x_prompt.shape
    bs, ss, _ = x_sample.shape
    mp, ms = bp * sp, bs * ss
    m = mp + ms
    n_a = a_w_in.shape[0]
    assert n_a == 1 and b_w_q.shape[0] == 1, "layer pattern is one delta layer then one attention layer"
    h_a = a_log.shape[1]
    dk = a_o_norm.shape[1]
    d_qk = h_a * dk
    kv_heads, hd = cache_win_k.shape[2], cache_win_k.shape[3]
    kvw = kv_heads * hd
    h_b = b_sinks.shape[1]
    group = h_b // kv_heads
    rot_dim = hd // 4

    def mm(xb, w, n_out, res=None):
        tn = min(TN_WIDE if res is None else TN_RES, n_out)
        return _mm_stream(xb, w, 0, n_out=n_out, tn=tn, n_m=M_STEPS, res=res)

    def ffn(hin, layer):
        (hn,) = _norm(hin, norm_ffn[layer:layer + 1], TM_NORM)
        act = _ffn_up(hn, ffn_w_gu, layer, n_m=M_STEPS, tn=TN_FFN)
        out = hin
        for kb in range(DOWN_KBLOCKS):
            out = _mm_stream(act, ffn_w_down, layer, n_out=d, tn=min(TN_RES, d), n_m=M_STEPS,
                             kblock=kb, n_kblocks=DOWN_KBLOCKS, res=out)
        return out

    xn, x = _norm_first(x_prompt.reshape(mp, d), x_sample.reshape(ms, d), norm_mix[0], TM_NORM)
    w_in = a_w_in[0]
    proj = mm(xn, a_w_in, 4 * d_qk)
    w_ba = jnp.pad(w_in[:, 4 * d_qk:], ((0, 0), (0, LANES - 2 * h_a))).astype(BF16)
    c_s = math.gcd(ss, DELTA_CHUNK)
    gates = _gates(xn, w_ba, a_log[0], a_dt_bias[0], tm=TM_NORM, n_prompt_rows=mp,
                   c_prompt=DELTA_CHUNK, c_sample=c_s)
    conv_w = a_conv_w[0]
    conv0_p = jnp.zeros((bp, CONV_W - 1, 3 * d_qk), F32)
    o_p, ssm_p, conv_p = _delta_rule(proj, gates[:, :, :mp], conv_w, conv0_p, None, a_o_norm[0],
                                     row0=0, batch=bp, seq=sp, n_heads=h_a, dk=dk,
                                     C=DELTA_CHUNK, nb=1, nc=2, hb=8, out_rows=m)
    o_s, ssm_s, conv_s = _delta_rule(proj, gates[:, :, mp:], conv_w, state_conv[0], state_ssm[0], a_o_norm[0],
                                     row0=mp, batch=bs, seq=ss, n_heads=h_a, dk=dk,
                                     C=c_s, nb=2, nc=1, hb=8, out_rows=m)
    o_a = _merge_rows(o_p, o_s, mp, TM_NORM)
    h1 = mm(o_a, a_w_out, d, res=x)
    h2 = ffn(h1, 0)

    hkv, hq = _norm(h2, jnp.stack([kv_norm, norm_mix[1]]), TM_NORM)
    kv = mm(hkv, w_kv[None], 2 * kvw)
    pos = jnp.concatenate([jnp.tile(jnp.arange(sp, dtype=F32), bp),
                           jnp.tile(PAST_LEN + jnp.arange(ss, dtype=F32), bs)])
    cos_t, sa_t, sb_t = _rope_tables(pos, hd, rot_dim)
    gmat = _group_matrix(kvw, hd)
    kf, kb, vb = pl.pallas_call(
        functools.partial(_kv_post_kernel, hd=hd, half=rot_dim // 2),
        grid=(m // TM_NORM,),
        in_specs=[pl.BlockSpec((TM_NORM, kvw), lambda i: (i, 0)),
                  pl.BlockSpec((TM_NORM, kvw), lambda i: (i, 1)),
                  pl.BlockSpec((1, kvw), lambda i: (0, 0)),
                  pl.BlockSpec((kvw, kvw), lambda i: (0, 0)),
                  *[pl.BlockSpec((TM_NORM, LANES), lambda i: (i, 0)) for _ in range(3)]],
        out_specs=[pl.BlockSpec((TM_NORM, kvw), lambda i: (i, 0)),
                   pl.BlockSpec((kv_heads, TM_NORM, hd), lambda i: (0, i, 0)),
                   pl.BlockSpec((kv_heads, TM_NORM, hd), lambda i: (0, i, 0))],
        out_shape=[jax.ShapeDtypeStruct((m, kvw), F32),
                   jax.ShapeDtypeStruct((kv_heads, m, hd), BF16),
                   jax.ShapeDtypeStruct((kv_heads, m, hd), BF16)],
        compiler_params=_cparams(("arbitrary",)),
        name="kv_post",
    )(kv, kv, jnp.tile(k_norm, kv_heads).reshape(1, kvw), gmat, cos_t, sa_t, sb_t)

    qraw = mm(hq, b_w_q, h_b * hd)
    qw = kvw
    q = pl.pallas_call(
        functools.partial(_q_post_kernel, hd=hd, half=rot_dim // 2, scale=hd ** -0.5),
        grid=(m // TM_NORM, h_b * hd // qw),
        in_specs=[pl.BlockSpec((TM_NORM, qw), lambda i, j: (i, j)),
                  pl.BlockSpec((1, qw), lambda i, j: (0, 0)),
                  pl.BlockSpec((qw, qw), lambda i, j: (0, 0)),
                  *[pl.BlockSpec((TM_NORM, LANES), lambda i, j: (i, 0)) for _ in range(3)]],
        out_specs=pl.BlockSpec((TM_NORM, qw), lambda i, j: (i, j)),
        out_shape=jax.ShapeDtypeStruct((m, h_b * hd), BF16),
        compiler_params=_cparams(("arbitrary", "arbitrary")),
        name="q_post",
    )(qraw, jnp.tile(b_q_norm[0], qw // hd).reshape(1, qw), gmat, cos_t, sa_t, sb_t)
    sinks = b_sinks[0]
    o_b = _attn_prompt(q, kb, vb, sinks, batch=bp, seq=sp, kv_heads=kv_heads, group=group, hd=hd, out_rows=m)
    o_b = _attn_sample(q, kf, kv, cache_win_k.reshape(bs, WINDOW, kvw), cache_win_v.reshape(bs, WINDOW, kvw),
                       sinks, o_b, row0=mp, batch=bs, tq=ss, kv_heads=kv_heads, group=group, hd=hd, nbs=2)
    h3 = mm(o_b, b_w_o, d, res=h2)
    h4 = ffn(h3, 1)

    wb = min(WINDOW, sp)
    kf_p = kf[:mp].reshape(bp, sp, kv_heads, hd)[:, sp - wb:]
    vf_p = kv[:mp, kvw:].reshape(bp, sp, kv_heads, hd)[:, sp - wb:]
    kf_s = kf[mp:].reshape(bs, ss, kv_heads, hd)
    vf_s = kv[mp:, kvw:].reshape(bs, ss, kv_heads, hd)
    wk_s = jnp.concatenate([cache_win_k[:, ss:], kf_s], axis=1)
    wv_s = jnp.concatenate([cache_win_v[:, ss:], vf_s], axis=1)
    return (h4[:mp].reshape(bp, sp, d), h4[mp:].reshape(bs, ss, d),
            ssm_p[None], conv_p[None], kf_p, vf_p, ssm_s[None], conv_s[None], wk_s, wv_s)
```

```python
import functools
import math

import jax
import jax.numpy as jnp
from jax import lax
from jax.experimental import pallas as pl
from jax.experimental.pallas import tpu as pltpu

F32 = jnp.float32
BF16 = jnp.bfloat16

EPS = 1e-6
WINDOW = 128
ROPE_THETA = 500000.0
CONV_W = 4
DELTA_CHUNK = 64
LANES = 128
CONV_PAD = 8
VMEM_LIMIT = 56 * 1024 * 1024


def _cparams(sem):
    return pltpu.CompilerParams(dimension_semantics=sem, vmem_limit_bytes=VMEM_LIMIT)


def _sigmoid(x):
    return 1.0 / (1.0 + jnp.exp(-x))


def _bdot(a, b):
    return jnp.dot(a.astype(BF16), b.astype(BF16), preferred_element_type=F32)


def _bdot_nt(a, b):
    return lax.dot_general(a.astype(BF16), b.astype(BF16), (((1,), (1,)), ((), ())),
                           preferred_element_type=F32)


def _bdot_tn(a, b):
    return lax.dot_general(a.astype(BF16), b.astype(BF16), (((0,), (0,)), ((), ())),
                           preferred_element_type=F32)


def _norm_first_kernel(xp_ref, xs_ref, w_ref, xn_ref, xcat_ref, *, n_prompt_blocks):
    i = pl.program_id(0)

    def emit(x):
        r = lax.rsqrt(jnp.mean(x * x, axis=-1, keepdims=True) + EPS)
        xn_ref[...] = (x * r * w_ref[...]).astype(xn_ref.dtype)
        xcat_ref[...] = x

    @pl.when(i < n_prompt_blocks)
    def _():
        emit(xp_ref[...])

    @pl.when(i >= n_prompt_blocks)
    def _():
        emit(xs_ref[...])


def _norm_first(xp, xs, w, tm):
    mp, d = xp.shape
    ms = xs.shape[0]
    assert mp % tm == 0 and ms == tm
    npb = mp // tm
    m = mp + ms
    return pl.pallas_call(
        functools.partial(_norm_first_kernel, n_prompt_blocks=npb),
        grid=(npb + 1,),
        in_specs=[pl.BlockSpec((tm, d), lambda i: (jnp.minimum(i, npb - 1), 0)),
                  pl.BlockSpec((tm, d), lambda i: (0, 0)),
                  pl.BlockSpec((1, d), lambda i: (0, 0))],
        out_specs=[pl.BlockSpec((tm, d), lambda i: (i, 0)),
                   pl.BlockSpec((tm, d), lambda i: (i, 0))],
        out_shape=[jax.ShapeDtypeStruct((m, d), BF16), jax.ShapeDtypeStruct((m, d), F32)],
        compiler_params=_cparams(("arbitrary",)),
        name="norm_first",
    )(xp, xs, w.reshape(1, d))


def _norm_kernel(x_ref, w_ref, *o_refs):
    x = x_ref[...]
    y = x * lax.rsqrt(jnp.mean(x * x, axis=-1, keepdims=True) + EPS)
    for i, o_ref in enumerate(o_refs):
        o_ref[...] = (y * w_ref[i:i + 1, :]).astype(o_ref.dtype)


def _norm(x, ws, tm):
    m, d = x.shape
    nw = ws.shape[0]
    assert m % tm == 0
    return pl.pallas_call(
        _norm_kernel,
        grid=(m // tm,),
        in_specs=[pl.BlockSpec((tm, d), lambda i: (i, 0)),
                  pl.BlockSpec((nw, d), lambda i: (0, 0))],
        out_specs=[pl.BlockSpec((tm, d), lambda i: (i, 0)) for _ in range(nw)],
        out_shape=[jax.ShapeDtypeStruct((m, d), BF16) for _ in range(nw)],
        compiler_params=_cparams(("arbitrary",)),
        name="norm",
    )(x, ws)


def _first_rows(j, i):
    return jnp.where(j == 0, 0, i)


def _mm_stream_kernel(*refs, has_res, w_t):
    if has_res:
        x_ref, w_ref, res_ref, o_ref, wbf_ref = refs
    else:
        x_ref, w_ref, o_ref, wbf_ref = refs
    j = pl.program_id(0)
    i = pl.program_id(1)
    rows_c = w_ref.shape[0]
    slot = j % 2
    wbf_ref[slot, pl.ds(pl.multiple_of(i * rows_c, rows_c), rows_c), :] = w_ref[...].astype(BF16)

    @pl.when(j > 0)
    def _():
        contract = (((1,), (1 if w_t else 0,)), ((), ()))
        acc = lax.dot_general(x_ref[...], wbf_ref[1 - slot], contract, preferred_element_type=F32)
        if has_res:
            acc = acc + res_ref[...]
        o_ref[...] = acc.astype(o_ref.dtype)


def _mm_stream(x, w, layer, *, n_out, tn, n_m, kblock=0, n_kblocks=1, res=None, out_dtype=F32, w_t=False):
    m = x.shape[0]
    k = x.shape[1] // n_kblocks
    tm, nj = m // n_m, n_out // tn
    rows_c = (tn if w_t else k) // n_m
    assert m % n_m == 0 and tm % 16 == 0 and rows_c * n_m == (tn if w_t else k) and rows_c % 16 == 0
    assert n_out % tn == 0 and w.shape[2 if w_t else 1] == k * n_kblocks == x.shape[1] and k % LANES == 0
    if w_t:
        w_spec = pl.BlockSpec((None, rows_c, k), lambda j, i: (layer, jnp.minimum(j, nj - 1) * n_m + i, kblock))
    else:
        w_spec = pl.BlockSpec((None, rows_c, tn),
                              lambda j, i: (layer, kblock * n_m + i, jnp.minimum(j, nj - 1)))
    in_specs = [pl.BlockSpec((tm, k), lambda j, i: (_first_rows(j, i), kblock)), w_spec]
    args = [x, w]
    out_map = lambda j, i: (_first_rows(j, i), jnp.maximum(j - 1, 0))
    if res is not None:
        in_specs.append(pl.BlockSpec((tm, tn), out_map))
        args.append(res)
    return pl.pallas_call(
        functools.partial(_mm_stream_kernel, has_res=res is not None, w_t=w_t),
        grid=(nj + 1, n_m),
        in_specs=in_specs,
        out_specs=pl.BlockSpec((tm, tn), out_map),
        out_shape=jax.ShapeDtypeStruct((m, n_out), out_dtype),
        scratch_shapes=[pltpu.VMEM((2, tn, k) if w_t else (2, k, tn), BF16)],
        compiler_params=_cparams(("arbitrary", "arbitrary")),
        name="mm_stream",
    )(*args)


FFN_PAIRS = 2


def _ffn_up_kernel(x_ref, *refs, tn):
    w_refs, (o_ref, wbf_ref) = refs[:2 * FFN_PAIRS], refs[2 * FFN_PAIRS:]
    j = pl.program_id(0)
    i = pl.program_id(1)
    rows_c = w_refs[0].shape[0]
    slot = j % 2
    rows = pl.ds(pl.multiple_of(i * rows_c, rows_c), rows_c)
    for c, w_ref in enumerate(w_refs):
        wbf_ref[slot, rows, c * tn:(c + 1) * tn] = w_ref[...].astype(BF16)

    @pl.when(j > 0)
    def _():
        gu = jnp.dot(x_ref[...], wbf_ref[1 - slot], preferred_element_type=F32)
        g = gu[:, :FFN_PAIRS * tn]
        o_ref[...] = (g * _sigmoid(g) * gu[:, FFN_PAIRS * tn:]).astype(o_ref.dtype)


def _ffn_up(x, w_gu, layer, *, n_m, tn):
    m, k = x.shape
    f = w_gu.shape[2] // 2
    tm, rows_c = m // n_m, k // n_m
    assert m % n_m == 0 and tm % 16 == 0 and k % n_m == 0 and rows_c % 16 == 0 and f % tn == 0
    n_pairs = f // tn
    nj = pl.cdiv(n_pairs, FFN_PAIRS)

    def w_map(c):
        g, part = c % FFN_PAIRS, c // FFN_PAIRS

        def index(j, i):
            pair = jnp.minimum(jnp.minimum(j, nj - 1) * FFN_PAIRS + g, n_pairs - 1)
            return layer, i, part * n_pairs + pair
        return index

    out_map = lambda j, i: (_first_rows(j, i), jnp.maximum(j - 1, 0))
    return pl.pallas_call(
        functools.partial(_ffn_up_kernel, tn=tn),
        grid=(nj + 1, n_m),
        in_specs=[pl.BlockSpec((tm, k), lambda j, i: (_first_rows(j, i), 0)),
                  *[pl.BlockSpec((None, rows_c, tn), w_map(c)) for c in range(2 * FFN_PAIRS)]],
        out_specs=pl.BlockSpec((tm, FFN_PAIRS * tn), out_map),
        out_shape=jax.ShapeDtypeStruct((m, f), BF16),
        scratch_shapes=[pltpu.VMEM((2, k, 2 * FFN_PAIRS * tn), BF16)],
        compiler_params=_cparams(("arbitrary", "arbitrary")),
        name="ffn_up",
    )(x, *([w_gu] * (2 * FFN_PAIRS)))


def _gates_kernel(x_ref, w_ref, alog_ref, dtb_ref, o_ref, *, n_heads, n_prompt_blocks, c_prompt, c_sample):
    tm = x_ref.shape[0]
    ba = lax.dot_general(w_ref[...], x_ref[...], (((1,), (1,)), ((), ())), preferred_element_type=F32)
    beta = _sigmoid(ba[:n_heads])
    a = ba[n_heads:] + dtb_ref[...]
    softplus = jnp.maximum(a, 0.0) + jnp.log(1.0 + jnp.exp(-jnp.abs(a)))
    g = -jnp.exp(alog_ref[...]) * softplus
    shift = jnp.where(pl.program_id(0) < n_prompt_blocks,
                      int(math.log2(c_prompt)), int(math.log2(c_sample)))
    jj = lax.broadcasted_iota(jnp.int32, (tm, tm), 0)
    ii = lax.broadcasted_iota(jnp.int32, (tm, tm), 1)
    same = lax.shift_right_logical(jj, shift) == lax.shift_right_logical(ii, shift)
    cum_m = jnp.where(same & (jj <= ii), 1.0, 0.0).astype(F32)
    tot_m = jnp.where(same, 1.0, 0.0).astype(F32)
    gcum = jnp.dot(g, cum_m, preferred_element_type=F32, precision=lax.Precision.HIGHEST)
    glast = jnp.dot(g, tot_m, preferred_element_type=F32, precision=lax.Precision.HIGHEST)
    o_ref[0] = beta
    o_ref[1] = jnp.exp(gcum)
    o_ref[2] = jnp.exp(glast - gcum)
    o_ref[3] = gcum


def _gates(xn, w_ba_t, a_log, dt_bias, *, tm, n_prompt_rows, c_prompt, c_sample):
    m, k = xn.shape
    h = a_log.shape[0]
    assert m % tm == 0 and n_prompt_rows % tm == 0 and tm % c_prompt == 0 and tm % c_sample == 0
    assert w_ba_t.shape == (2 * h, k)
    return pl.pallas_call(
        functools.partial(_gates_kernel, n_heads=h, n_prompt_blocks=n_prompt_rows // tm,
                          c_prompt=c_prompt, c_sample=c_sample),
        grid=(m // tm,),
        in_specs=[pl.BlockSpec((tm, k), lambda i: (i, 0)),
                  pl.BlockSpec((2 * h, k), lambda i: (0, 0)),
                  pl.BlockSpec((h, 1), lambda i: (0, 0)),
                  pl.BlockSpec((h, 1), lambda i: (0, 0))],
        out_specs=pl.BlockSpec((4, h, tm), lambda i: (0, 0, i)),
        out_shape=jax.ShapeDtypeStruct((4, h, m), F32),
        compiler_params=_cparams(("arbitrary",)),
        name="gates",
    )(xn, w_ba_t, a_log.reshape(h, 1), dt_bias.reshape(h, 1))


def _delta_kernel(q_ref, k_ref, v_ref, z_ref, cwq_ref, cwk_ref, cwv_ref, csq_ref, csk_ref, csv_ref,
                  gcol_ref, grow_ref, h0_ref, onw_ref,
                  o_ref, hout_ref, cqo_ref, cko_ref, cvo_ref,
                  h_scr, xq_scr, xk_scr, xv_scr, aq_scr, ak_scr, av_scr,
                  *, C, nb, nc, hb, dk, zero_init):
    t = pl.program_id(2)
    n_t = pl.num_programs(2)
    ttb = nc * C
    hist0 = CONV_PAD - (CONV_W - 1)

    @pl.when(t == 0)
    def _():
        if zero_init:
            h_scr[...] = jnp.zeros(h_scr.shape, F32)
        else:
            h_scr[...] = h0_ref[...]
        for xs, cs in ((xq_scr, csq_ref), (xk_scr, csk_ref), (xv_scr, csv_ref)):
            for bi in range(nb):
                xs[bi, hist0:CONV_PAD, :] = cs[bi]

    for raw_ref, xs, cw_ref, act in ((q_ref, xq_scr, cwq_ref, aq_scr), (k_ref, xk_scr, cwk_ref, ak_scr),
                                     (v_ref, xv_scr, cwv_ref, av_scr)):
        for bi in range(nb):
            xs[bi, CONV_PAD:CONV_PAD + ttb, :] = raw_ref[bi * ttb:(bi + 1) * ttb, :]
            acc = xs[bi, hist0:hist0 + ttb, :] * cw_ref[0:1, :]
            for i in range(1, CONV_W):
                acc = acc + xs[bi, hist0 + i:hist0 + i + ttb, :] * cw_ref[i:i + 1, :]
            act[bi * ttb:(bi + 1) * ttb, :] = acc * _sigmoid(acc)
            xs[bi, hist0:CONV_PAD, :] = xs[bi, CONV_PAD + ttb - (CONV_W - 1):CONV_PAD + ttb, :]

    @pl.when(t == n_t - 1)
    def _():
        for xs, co in ((xq_scr, cqo_ref), (xk_scr, cko_ref), (xv_scr, cvo_ref)):
            for bi in range(nb):
                co[bi] = xs[bi, hist0:CONV_PAD, :]

    ii = lax.broadcasted_iota(jnp.int32, (C, C), 0)
    jj = lax.broadcasted_iota(jnp.int32, (C, C), 1)
    n_levels = int(math.log2(C))
    assert 2 ** n_levels == C

    items = [(bi, c, j) for bi in range(nb) for c in range(nc) for j in range(hb)]

    def prepare(bi, c, j):
        rows = slice(bi * ttb + c * C, bi * ttb + (c + 1) * C)
        cols = slice(j * dk, (j + 1) * dk)
        q = aq_scr[rows, cols]
        k = ak_scr[rows, cols]
        qn = q * (lax.rsqrt(jnp.sum(q * q, axis=-1, keepdims=True) + EPS) * (dk ** -0.5))
        kn = k * lax.rsqrt(jnp.sum(k * k, axis=-1, keepdims=True) + EPS)
        gcb = gcol_ref[0, rows, :]
        beta = gcb[:, j:j + 1]
        eg = gcb[:, hb + j:hb + j + 1]
        ekl = gcb[:, 2 * hb + j:2 * hb + j + 1]
        gc = gcb[:, 3 * hb + j:3 * hb + j + 1]
        gr = grow_ref[0, bi * nc + c, j:j + 1, :]
        dmat = jnp.where(ii >= jj, jnp.exp(gc - gr), 0.0)
        rhs = jnp.concatenate([kn * (beta * eg), av_scr[rows, cols] * beta], axis=1)
        return dict(rows=rows, cols=cols, qn=qn, kn=kn, beta=beta, eg=eg, dmat=dmat, rhs=rhs,
                    qe=qn * eg, kd=kn * ekl)

    st = [prepare(*it) for it in items]
    for e in st:
        e["s"] = _bdot_nt(jnp.concatenate([e["kn"], e["qn"]], axis=0), e["kn"])
    for e in st:
        s = e.pop("s")
        e["qk"] = s[C:] * e["dmat"]
        e["p"] = jnp.where(ii > jj, s[:C] * e["dmat"] * (-e["beta"]), 0.0)
        e["t"] = e["p"]
    for e in st:
        e["p"] = _bdot(e["p"], e["p"])
    for lvl in range(n_levels - 1):
        last = lvl == n_levels - 2
        for e in st:
            e["tp"] = _bdot(e["t"], e["p"])
            if not last:
                e["p2"] = _bdot(e["p"], e["p"])
        for e in st:
            e["t"] = e["t"] + e["p"] + e.pop("tp")
            if not last:
                e["p"] = e.pop("p2")
    for e in st:
        e["wu"] = e["rhs"] + _bdot(e["t"], e["rhs"])
    h_cur = {(bi, j): h_scr[bi, j] for bi in range(nb) for j in range(hb)}
    for c in range(nc):
        sel = [(it, e) for it, e in zip(items, st) if it[1] == c]
        for (bi, _, j), e in sel:
            e["wq"] = _bdot(jnp.concatenate([e["wu"][:, :dk], e["qe"]], axis=0), h_cur[bi, j])
        for _, e in sel:
            e["u"] = e["wu"][:, dk:] - e["wq"][:C]
        for (bi, _, j), e in sel:
            e["o"] = e["wq"][C:] + _bdot(e["qk"], e["u"])
            egl = e["eg"][C - 1:C, :]
            h_cur[bi, j] = h_cur[bi, j] * egl + _bdot_tn(e["kd"], e["u"])
        for _, e in sel:
            o = e["o"]
            on = o * lax.rsqrt(jnp.mean(o * o, axis=-1, keepdims=True) + EPS) * onw_ref[...]
            zz = z_ref[e["rows"], e["cols"]]
            o_ref[e["rows"], e["cols"]] = (on * (zz * _sigmoid(zz))).astype(o_ref.dtype)
    for (bi, j), hv in h_cur.items():
        h_scr[bi, j] = hv

    @pl.when(t == n_t - 1)
    def _():
        hout_ref[...] = h_scr[...]


def _delta_rule(proj, gates, conv_w, conv0, h0, o_norm_w, *, row0, batch, seq, n_heads, dk,
                C, nb, nc, hb, out_rows):
    ttb = nc * C
    tt = nb * ttb
    w = hb * dk
    ng = n_heads // hb
    n_t = seq // ttb
    assert seq % ttb == 0 and batch % nb == 0 and n_heads % hb == 0 and row0 % tt == 0
    zero_init = h0 is None
    d_qk = n_heads * dk
    cb = d_qk // w
    rb0 = row0 // tt

    rows = batch * seq
    g5 = gates.reshape(4, ng, hb, rows)
    gcol = jnp.transpose(g5, (1, 3, 0, 2)).reshape(ng, rows, 4 * hb)
    grow = jnp.transpose(g5[3].reshape(ng, hb, rows // C, C), (0, 2, 1, 3))

    def tok_map(part):
        return lambda b, g, t: (rb0 + b * n_t + t, part * cb + g)

    if zero_init:
        h0_arg = jnp.zeros((nb, hb, dk, dk), F32)
        h0_spec = pl.BlockSpec((nb, hb, dk, dk), lambda b, g, t: (0, 0, 0, 0))
    else:
        h0_arg = h0
        h0_spec = pl.BlockSpec((nb, hb, dk, dk), lambda b, g, t: (b, g, 0, 0))
    cs_spec = [pl.BlockSpec((nb, CONV_W - 1, w), (lambda b, g, t, p=p: (b, 0, p * cb + g))) for p in range(3)]
    cw_spec = [pl.BlockSpec((CONV_W, w), (lambda b, g, t, p=p: (0, p * cb + g))) for p in range(3)]
    kern = functools.partial(_delta_kernel, C=C, nb=nb, nc=nc, hb=hb, dk=dk, zero_init=zero_init)
    o, h_fin, cq, ck, cv = pl.pallas_call(
        kern,
        grid=(batch // nb, ng, n_t),
        in_specs=[pl.BlockSpec((tt, w), tok_map(0)), pl.BlockSpec((tt, w), tok_map(1)),
                  pl.BlockSpec((tt, w), tok_map(2)), pl.BlockSpec((tt, w), tok_map(3)),
                  *cw_spec, *cs_spec,
                  pl.BlockSpec((1, tt, 4 * hb), lambda b, g, t: (g, b * n_t + t, 0)),
                  pl.BlockSpec((1, nb * nc, hb, C), lambda b, g, t: (g, b * n_t + t, 0, 0)),
                  h0_spec,
                  pl.BlockSpec((1, dk), lambda b, g, t: (0, 0))],
        out_specs=[pl.BlockSpec((tt, w), lambda b, g, t: (rb0 + b * n_t + t, g)),
                   pl.BlockSpec((nb, hb, dk, dk), lambda b, g, t: (b, g, 0, 0)),
                   *[pl.BlockSpec((nb, CONV_W - 1, w), lambda b, g, t: (b, 0, g)) for _ in range(3)]],
        out_shape=[jax.ShapeDtypeStruct((out_rows, d_qk), BF16),
                   jax.ShapeDtypeStruct((batch, n_heads, dk, dk), F32),
                   *[jax.ShapeDtypeStruct((batch, CONV_W - 1, d_qk), F32) for _ in range(3)]],
        scratch_shapes=[pltpu.VMEM((nb, hb, dk, dk), F32),
                        *[pltpu.VMEM((nb, CONV_PAD + ttb, w), F32) for _ in range(3)],
                        *[pltpu.VMEM((tt, w), F32) for _ in range(3)]],
        compiler_params=_cparams(("arbitrary", "arbitrary", "arbitrary")),
        name="delta_rule",
    )(proj, proj, proj, proj, conv_w, conv_w, conv_w, conv0, conv0, conv0, gcol, grow, h0_arg,
      o_norm_w.reshape(1, dk))
    return o, h_fin, jnp.concatenate([cq, ck, cv], axis=-1)


def _group_mean_sq(x, gmat_ref, inv_n):
    x2 = x * x
    hi = x2.astype(BF16)
    lo = (x2 - hi.astype(F32)).astype(BF16)
    g = gmat_ref[...]
    return (jnp.dot(hi, g, preferred_element_type=F32) + jnp.dot(lo, g, preferred_element_type=F32)) * inv_n


def _rope(y, cos_ref, sa_ref, sb_ref, half):
    w = y.shape[1]
    reps = w // cos_ref.shape[1]
    tile = lambda r: jnp.concatenate([r[...]] * reps, axis=1)
    return (y * tile(cos_ref) + pltpu.roll(y, w - half, axis=1) * tile(sa_ref)
            + pltpu.roll(y, half, axis=1) * tile(sb_ref))


def _q_post_kernel(x_ref, w_ref, gmat_ref, cos_ref, sa_ref, sb_ref, o_ref, *, hd, half, scale):
    gw = gmat_ref.shape[0]
    for c in range(x_ref.shape[1] // gw):
        cols = slice(c * gw, (c + 1) * gw)
        x = x_ref[:, cols]
        y = x * lax.rsqrt(_group_mean_sq(x, gmat_ref, 1.0 / hd) + EPS) * w_ref[...]
        o_ref[:, cols] = (_rope(y, cos_ref, sa_ref, sb_ref, half) * scale).astype(o_ref.dtype)


def _kv_post_kernel(k_ref, v_ref, w_ref, gmat_ref, cos_ref, sa_ref, sb_ref,
                    kf_ref, kb_ref, vb_ref, kwin_ref, vwin_ref, *, hd, half, n_prompt_blocks, blocks_per_seq):
    i = pl.program_id(0)
    x = k_ref[...]
    y = x * lax.rsqrt(_group_mean_sq(x, gmat_ref, 1.0 / hd) + EPS) * w_ref[...]
    kf = _rope(y, cos_ref, sa_ref, sb_ref, half)
    kf_ref[...] = kf
    v = v_ref[...]
    for hh in range(kb_ref.shape[0]):
        kb_ref[hh] = kf[:, hh * hd:(hh + 1) * hd].astype(BF16)
        vb_ref[hh] = v[:, hh * hd:(hh + 1) * hd].astype(BF16)

    @pl.when((i < n_prompt_blocks) & (i % blocks_per_seq == blocks_per_seq - 1))
    def _():
        wb = kwin_ref.shape[0]
        kwin_ref[...] = kf[kf.shape[0] - wb:, :]
        vwin_ref[...] = v[v.shape[0] - wb:, :]


def _rope_tables(pos, hd, rot_dim):
    half = rot_dim // 2
    inv_freq = jnp.power(ROPE_THETA, -jnp.arange(half, dtype=F32) * 2.0 / rot_dim)
    ang = pos[:, None] * inv_freq[None, :]
    cos, sin = jnp.cos(ang), jnp.sin(ang)
    m = pos.shape[0]
    ones = jnp.ones((m, hd - rot_dim), F32)
    zeros = jnp.zeros((m, hd - rot_dim), F32)
    zh = jnp.zeros((m, half), F32)
    c = jnp.concatenate([cos, cos, ones], axis=1)
    sa = jnp.concatenate([-sin, zh, zeros], axis=1)
    sb = jnp.concatenate([zh, sin, zeros], axis=1)
    reps = LANES // hd
    return tuple(jnp.tile(a, (1, reps)) for a in (c, sa, sb))


def _group_matrix(width, hd):
    r = jnp.arange(width) // hd
    return (r[:, None] == r[None, :]).astype(BF16)


def _softmax_sink(s, mask, sink):
    s = jnp.where(mask, s, -jnp.inf)
    m = jnp.maximum(jnp.max(s, axis=-1, keepdims=True), sink)
    p = jnp.exp(s - m)
    denom = jnp.sum(p, axis=-1, keepdims=True) + jnp.exp(sink - m)
    return (p / denom).astype(BF16)


def _attn_prompt_kernel(sink_ref, q_ref, kp_ref, kc_ref, vp_ref, vc_ref, o_ref, *, kv_heads, group, hd, win):
    nb = pl.program_id(1)
    qi = lax.broadcasted_iota(jnp.int32, (win, 2 * win), 0)
    kj = lax.broadcasted_iota(jnp.int32, (win, 2 * win), 1)
    mask = (kj > qi) & (kj <= qi + win) & ((kj >= win) | (nb > 0))

    def scores(h):
        kk = jnp.concatenate([kp_ref[h], kc_ref[h]], axis=0)
        return [lax.dot_general(q_ref[:, (h * group + g) * hd:(h * group + g + 1) * hd], kk,
                                (((1,), (1,)), ((), ())), preferred_element_type=F32) for g in range(group)]

    def row_max(h, ss):
        out = []
        for g, s in enumerate(ss):
            s = jnp.where(mask, s, -jnp.inf)
            out.append((s, jnp.maximum(jnp.max(s, axis=-1, keepdims=True), sink_ref[h * group + g])))
        return out

    def probs(h, sm):
        out = []
        for g, (s, m) in enumerate(sm):
            p = jnp.exp(s - m)
            denom = jnp.sum(p, axis=-1, keepdims=True) + jnp.exp(sink_ref[h * group + g] - m)
            out.append((p, denom))
        return out

    def weighted_values(h, pd):
        vv = jnp.concatenate([vp_ref[h], vc_ref[h]], axis=0)
        for g, (p, denom) in enumerate(pd):
            hh = h * group + g
            o = jnp.dot((p / denom).astype(BF16), vv, preferred_element_type=F32)
            o_ref[:, hh * hd:(hh + 1) * hd] = o.astype(o_ref.dtype)

    stages = (scores, row_max, probs, weighted_values)
    carry = [None] * len(stages)
    for step in range(kv_heads + len(stages) - 1):
        for k in reversed(range(len(stages))):
            h = step - k
            if 0 <= h < kv_heads:
                carry[k] = stages[k](h) if k == 0 else stages[k](h, carry[k - 1])


def _attn_prompt(q, kb, vb, sinks, *, batch, seq, kv_heads, group, hd, out_rows):
    win = WINDOW
    nblk = seq // win
    qw = kv_heads * group * hd
    prev = lambda b, n: (0, b * nblk + jnp.maximum(n - 1, 0), 0)
    cur = lambda b, n: (0, b * nblk + n, 0)
    kv_spec = lambda index: pl.BlockSpec((kv_heads, win, hd), index)
    return pl.pallas_call(
        functools.partial(_attn_prompt_kernel, kv_heads=kv_heads, group=group, hd=hd, win=win),
        grid=(batch, nblk),
        in_specs=[pl.BlockSpec(memory_space=pltpu.SMEM),
                  pl.BlockSpec((win, qw), lambda b, n: (b * nblk + n, 0)),
                  kv_spec(prev), kv_spec(cur), kv_spec(prev), kv_spec(cur)],
        out_specs=pl.BlockSpec((win, qw), lambda b, n: (b * nblk + n, 0)),
        out_shape=jax.ShapeDtypeStruct((out_rows, qw), BF16),
        compiler_params=_cparams(("arbitrary", "arbitrary")),
        name="attn_prompt",
    )(sinks, q, kb, kb, vb, vb)


def _attn_sample_kernel(sink_ref, q_ref, kn_ref, vn_ref, kc_ref, vc_ref, prev_ref, o_ref,
                        *, nbs, tq, kv_heads, group, hd, win):
    del prev_ref
    nctx = 2 * win
    r = lax.broadcasted_iota(jnp.int32, (group * tq, nctx), 0)
    kj = lax.broadcasted_iota(jnp.int32, (group * tq, nctx), 1)
    tpos = r % tq
    mask = ((kj < win) & (kj > tpos)) | ((kj >= win) & (kj - win <= tpos))
    pad = jnp.zeros((nctx - win - tq, hd), F32)
    items = [(bi, h) for bi in range(nbs) for h in range(kv_heads)]
    qbs = [q_ref[bi * tq:(bi + 1) * tq, :].astype(F32) for bi in range(nbs)]
    sinks = [jnp.concatenate([jnp.full((tq, 1), sink_ref[h * group + g], F32) for g in range(group)], axis=0)
             for h in range(kv_heads)]
    ss, vs = [], []
    for bi, h in items:
        cs = slice(h * hd, (h + 1) * hd)
        trows = slice(bi * tq, (bi + 1) * tq)
        kctx = jnp.concatenate([kc_ref[bi][:, cs], kn_ref[trows, cs], pad], axis=0)
        vs.append(jnp.concatenate([vc_ref[bi][:, cs], vn_ref[trows, cs], pad], axis=0).astype(BF16))
        qg = jnp.concatenate([qbs[bi][:, (h * group + g) * hd:(h * group + g + 1) * hd]
                              for g in range(group)], axis=0)
        ss.append(_bdot_nt(qg, kctx))
    ps = [_softmax_sink(s, mask, sinks[h]) for (bi, h), s in zip(items, ss)]
    for (bi, h), p, v in zip(items, ps, vs):
        o = jnp.dot(p, v, preferred_element_type=F32)
        for g in range(group):
            hh = h * group + g
            o_ref[bi * tq:(bi + 1) * tq, hh * hd:(hh + 1) * hd] = o[g * tq:(g + 1) * tq, :].astype(o_ref.dtype)


def _attn_sample(q, kf, kv, cache_k, cache_v, sinks, o_prev, *, row0, batch, tq, kv_heads, group, hd, nbs):
    win = cache_k.shape[1]
    assert win == WINDOW and batch % nbs == 0 and row0 % (nbs * tq) == 0
    rows = nbs * tq
    rb0 = row0 // rows
    kvw = kv_heads * hd
    return pl.pallas_call(
        functools.partial(_attn_sample_kernel, nbs=nbs, tq=tq, kv_heads=kv_heads, group=group, hd=hd, win=win),
        grid=(batch // nbs,),
        in_specs=[pl.BlockSpec(memory_space=pltpu.SMEM),
                  pl.BlockSpec((rows, q.shape[1]), lambda b: (rb0 + b, 0)),
                  pl.BlockSpec((rows, kvw), lambda b: (rb0 + b, 0)),
                  pl.BlockSpec((rows, kvw), lambda b: (rb0 + b, 1)),
                  pl.BlockSpec((nbs, win, kvw), lambda b: (b, 0, 0)),
                  pl.BlockSpec((nbs, win, kvw), lambda b: (b, 0, 0)),
                  pl.BlockSpec(memory_space=pl.ANY)],
        out_specs=pl.BlockSpec((rows, q.shape[1]), lambda b: (rb0 + b, 0)),
        out_shape=jax.ShapeDtypeStruct(o_prev.shape, o_prev.dtype),
        input_output_aliases={6: 0},
        compiler_params=_cparams(("arbitrary",)),
        name="attn_sample",
    )(sinks, q, kf, kv, cache_k, cache_v, o_prev)


PAST_LEN = 16384
TM_NORM = 256
M_STEPS = 8


Q_POST_COLS = 2048
TN_WIDE = 1024
TN_RES = 512
TN_FFN = 256
DOWN_KBLOCKS = 2


def _write_rows_kernel(src_ref, dst_ref, o_ref):
    del dst_ref
    o_ref[...] = src_ref[...]


def _merge_rows(dst, src, row0, tm):
    n = src.shape[1]
    rows = src.shape[0] - row0
    assert rows % tm == 0 and row0 % tm == 0
    return pl.pallas_call(
        _write_rows_kernel,
        grid=(rows // tm,),
        in_specs=[pl.BlockSpec((tm, n), lambda i: (row0 // tm + i, 0)),
                  pl.BlockSpec(memory_space=pl.ANY)],
        out_specs=pl.BlockSpec((tm, n), lambda i: (row0 // tm + i, 0)),
        out_shape=jax.ShapeDtypeStruct(dst.shape, dst.dtype),
        input_output_aliases={1: 0},
        compiler_params=_cparams(("arbitrary",)),
        name="merge_rows",
    )(src, dst)


def kernel(x_prompt, x_sample, state_ssm, state_conv, cache_win_k, cache_win_v, norm_mix, norm_ffn,
           a_w_in, a_conv_w, a_log, a_dt_bias, a_o_norm, a_w_out, kv_norm, w_kv, k_norm, b_w_q,
           b_q_norm, b_sinks, b_w_o, ffn_w_gu, ffn_w_down):
    bp, sp, d = x_prompt.shape
    bs, ss, _ = x_sample.shape
    mp, ms = bp * sp, bs * ss
    m = mp + ms
    n_a = a_w_in.shape[0]
    assert n_a == 1 and b_w_q.shape[0] == 1, "layer pattern is one delta layer then one attention layer"
    h_a = a_log.shape[1]
    dk = a_o_norm.shape[1]
    d_qk = h_a * dk
    kv_heads, hd = cache_win_k.shape[2], cache_win_k.shape[3]
    kvw = kv_heads * hd
    h_b = b_sinks.shape[1]
    group = h_b // kv_heads
    rot_dim = hd // 4

    def mm(xb, w, n_out, res=None):
        tn = min(TN_WIDE if res is None else TN_RES, n_out)
        return _mm_stream(xb, w, 0, n_out=n_out, tn=tn, n_m=M_STEPS, res=res)

    def ffn(hin, layer):
        (hn,) = _norm(hin, norm_ffn[layer:layer + 1], TM_NORM)
        act = _ffn_up(hn, ffn_w_gu, layer, n_m=M_STEPS, tn=TN_FFN)
        out = hin
        for kb in range(DOWN_KBLOCKS):
            out = _mm_stream(act, ffn_w_down, layer, n_out=d, tn=min(TN_RES, d), n_m=M_STEPS,
                             kblock=kb, n_kblocks=DOWN_KBLOCKS, res=out)
        return out

    xn, x = _norm_first(x_prompt.reshape(mp, d), x_sample.reshape(ms, d), norm_mix[0], TM_NORM)
    w_in_t = jnp.swapaxes(a_w_in, 1, 2)
    proj = _mm_stream(xn, w_in_t, 0, n_out=4 * d_qk, tn=TN_WIDE, n_m=M_STEPS, w_t=True)
    w_ba_t = w_in_t[0, 4 * d_qk:, :].astype(BF16)
    c_s = math.gcd(ss, DELTA_CHUNK)
    gates = _gates(xn, w_ba_t, a_log[0], a_dt_bias[0], tm=TM_NORM, n_prompt_rows=mp,
                   c_prompt=DELTA_CHUNK, c_sample=c_s)
    conv_w = a_conv_w[0]
    conv0_p = jnp.zeros((bp, CONV_W - 1, 3 * d_qk), F32)
    o_p, ssm_p, conv_p = _delta_rule(proj, gates[:, :, :mp], conv_w, conv0_p, None, a_o_norm[0],
                                     row0=0, batch=bp, seq=sp, n_heads=h_a, dk=dk,
                                     C=DELTA_CHUNK, nb=1, nc=2, hb=8, out_rows=m)
    o_s, ssm_s, conv_s = _delta_rule(proj, gates[:, :, mp:], conv_w, state_conv[0], state_ssm[0], a_o_norm[0],
                                     row0=mp, batch=bs, seq=ss, n_heads=h_a, dk=dk,
                                     C=c_s, nb=2, nc=1, hb=8, out_rows=m)
    o_a = _merge_rows(o_p, o_s, mp, TM_NORM)
    h1 = mm(o_a, a_w_out, d, res=x)
    h2 = ffn(h1, 0)

    hkv, hq = _norm(h2, jnp.stack([kv_norm, norm_mix[1]]), TM_NORM)
    kv = mm(hkv, w_kv[None], 2 * kvw)
    pos = jnp.concatenate([jnp.tile(jnp.arange(sp, dtype=F32), bp),
                           jnp.tile(PAST_LEN + jnp.arange(ss, dtype=F32), bs)])
    cos_t, sa_t, sb_t = _rope_tables(pos, hd, rot_dim)
    gmat = _group_matrix(kvw, hd)
    wb = min(WINDOW, sp)
    bps = sp // TM_NORM
    assert sp % TM_NORM == 0 and wb <= TM_NORM
    win_spec = pl.BlockSpec((wb, kvw), lambda i: (jnp.minimum(i // bps, bp - 1), 0))
    kf, kb, vb, kwin_p, vwin_p = pl.pallas_call(
        functools.partial(_kv_post_kernel, hd=hd, half=rot_dim // 2, n_prompt_blocks=mp // TM_NORM,
                          blocks_per_seq=bps),
        grid=(m // TM_NORM,),
        in_specs=[pl.BlockSpec((TM_NORM, kvw), lambda i: (i, 0)),
                  pl.BlockSpec((TM_NORM, kvw), lambda i: (i, 1)),
                  pl.BlockSpec((1, kvw), lambda i: (0, 0)),
                  pl.BlockSpec((kvw, kvw), lambda i: (0, 0)),
                  *[pl.BlockSpec((TM_NORM, LANES), lambda i: (i, 0)) for _ in range(3)]],
        out_specs=[pl.BlockSpec((TM_NORM, kvw), lambda i: (i, 0)),
                   pl.BlockSpec((kv_heads, TM_NORM, hd), lambda i: (0, i, 0)),
                   pl.BlockSpec((kv_heads, TM_NORM, hd), lambda i: (0, i, 0)),
                   win_spec, win_spec],
        out_shape=[jax.ShapeDtypeStruct((m, kvw), F32),
                   jax.ShapeDtypeStruct((kv_heads, m, hd), BF16),
                   jax.ShapeDtypeStruct((kv_heads, m, hd), BF16),
                   jax.ShapeDtypeStruct((bp * wb, kvw), F32),
                   jax.ShapeDtypeStruct((bp * wb, kvw), F32)],
        compiler_params=_cparams(("arbitrary",)),
        name="kv_post",
    )(kv, kv, jnp.tile(k_norm, kv_heads).reshape(1, kvw), gmat, cos_t, sa_t, sb_t)

    qraw = mm(hq, b_w_q, h_b * hd)
    qw = kvw
    qcols = h_b * hd
    qblk = min(Q_POST_COLS, qcols)
    assert qcols % qblk == 0 and qblk % qw == 0
    q = pl.pallas_call(
        functools.partial(_q_post_kernel, hd=hd, half=rot_dim // 2, scale=hd ** -0.5),
        grid=(m // TM_NORM, qcols // qblk),
        in_specs=[pl.BlockSpec((TM_NORM, qblk), lambda i, j: (i, j)),
                  pl.BlockSpec((1, qw), lambda i, j: (0, 0)),
                  pl.BlockSpec((qw, qw), lambda i, j: (0, 0)),
                  *[pl.BlockSpec((TM_NORM, LANES), lambda i, j: (i, 0)) for _ in range(3)]],
        out_specs=pl.BlockSpec((TM_NORM, qblk), lambda i, j: (i, j)),
        out_shape=jax.ShapeDtypeStruct((m, qcols), BF16),
        compiler_params=_cparams(("arbitrary", "arbitrary")),
        name="q_post",
    )(qraw, jnp.tile(b_q_norm[0], qw // hd).reshape(1, qw), gmat, cos_t, sa_t, sb_t)
    sinks = b_sinks[0]
    o_b = _attn_prompt(q, kb, vb, sinks, batch=bp, seq=sp, kv_heads=kv_heads, group=group, hd=hd, out_rows=m)
    o_b = _attn_sample(q, kf, kv, cache_win_k.reshape(bs, WINDOW, kvw), cache_win_v.reshape(bs, WINDOW, kvw),
                       sinks, o_b, row0=mp, batch=bs, tq=ss, kv_heads=kv_heads, group=group, hd=hd, nbs=2)
    h3 = mm(o_b, b_w_o, d, res=h2)
    h4 = ffn(h3, 1)

    kf_p = kwin_p.reshape(bp, wb, kv_heads, hd)
    vf_p = vwin_p.reshape(bp, wb, kv_heads, hd)
    kf_s = kf[mp:].reshape(bs, ss, kv_heads, hd)
    vf_s = kv[mp:, kvw:].reshape(bs, ss, kv_heads, hd)
    wk_s = jnp.concatenate([cache_win_k[:, ss:], kf_s], axis=1)
    wv_s = jnp.concatenate([cache_win_v[:, ss:], vf_s], axis=1)
    return (h4[:mp].reshape(bp, sp, d), h4[mp:].reshape(bs, ss, d),
            ssm_p[None], conv_p[None], kf_p, vf_p, ssm_s[None], conv_s[None], wk_s, wv_s)
```

```python
import functools
import itertools
import math

import jax
import jax.numpy as jnp
from jax import lax
from jax.experimental import pallas as pl
from jax.experimental.pallas import tpu as pltpu

F32 = jnp.float32
BF16 = jnp.bfloat16

EPS = 1e-6
WINDOW = 128
ROPE_THETA = 500000.0
CONV_W = 4
DELTA_CHUNK = 64
LANES = 128
CONV_PAD = 8
GROUP_CHUNKS = 2
VMEM_LIMIT = 56 * 1024 * 1024


def _cparams(sem):
    return pltpu.CompilerParams(dimension_semantics=sem, vmem_limit_bytes=VMEM_LIMIT)


def _sigmoid(x):
    return 1.0 / (1.0 + jnp.exp(-x))


def _bdot(a, b):
    return jnp.dot(a.astype(BF16), b.astype(BF16), preferred_element_type=F32)


def _bdot_nt(a, b):
    return lax.dot_general(a.astype(BF16), b.astype(BF16), (((1,), (1,)), ((), ())),
                           preferred_element_type=F32)


def _bdot_tn(a, b):
    return lax.dot_general(a.astype(BF16), b.astype(BF16), (((0,), (0,)), ((), ())),
                           preferred_element_type=F32)


def _norm_first_kernel(xp_ref, xs_ref, w_ref, xn_ref, xcat_ref, *, n_prompt_blocks):
    i = pl.program_id(0)

    def emit(x):
        r = lax.rsqrt(jnp.mean(x * x, axis=-1, keepdims=True) + EPS)
        xn_ref[...] = (x * r * w_ref[...]).astype(xn_ref.dtype)
        xcat_ref[...] = x

    @pl.when(i < n_prompt_blocks)
    def _():
        emit(xp_ref[...])

    @pl.when(i >= n_prompt_blocks)
    def _():
        emit(xs_ref[...])


def _norm_first(xp, xs, w, tm):
    mp, d = xp.shape
    ms = xs.shape[0]
    assert mp % tm == 0 and ms == tm
    npb = mp // tm
    m = mp + ms
    return pl.pallas_call(
        functools.partial(_norm_first_kernel, n_prompt_blocks=npb),
        grid=(npb + 1,),
        in_specs=[pl.BlockSpec((tm, d), lambda i: (jnp.minimum(i, npb - 1), 0)),
                  pl.BlockSpec((tm, d), lambda i: (0, 0)),
                  pl.BlockSpec((1, d), lambda i: (0, 0))],
        out_specs=[pl.BlockSpec((tm, d), lambda i: (i, 0)),
                   pl.BlockSpec((tm, d), lambda i: (i, 0))],
        out_shape=[jax.ShapeDtypeStruct((m, d), BF16), jax.ShapeDtypeStruct((m, d), F32)],
        compiler_params=_cparams(("arbitrary",)),
        name="norm_first",
    )(xp, xs, w.reshape(1, d))


def _norm_kernel(x_ref, w_ref, *o_refs):
    x = x_ref[...]
    y = x * lax.rsqrt(jnp.mean(x * x, axis=-1, keepdims=True) + EPS)
    for i, o_ref in enumerate(o_refs):
        o_ref[...] = (y * w_ref[i:i + 1, :]).astype(o_ref.dtype)


def _norm(x, ws, tm):
    m, d = x.shape
    nw = ws.shape[0]
    assert m % tm == 0
    return pl.pallas_call(
        _norm_kernel,
        grid=(m // tm,),
        in_specs=[pl.BlockSpec((tm, d), lambda i: (i, 0)),
                  pl.BlockSpec((nw, d), lambda i: (0, 0))],
        out_specs=[pl.BlockSpec((tm, d), lambda i: (i, 0)) for _ in range(nw)],
        out_shape=[jax.ShapeDtypeStruct((m, d), BF16) for _ in range(nw)],
        compiler_params=_cparams(("arbitrary",)),
        name="norm",
    )(x, ws)


def _first_rows(j, i):
    return jnp.where(j == 0, 0, i)


def _mm_stream_kernel(*refs, has_res, w_t):
    if has_res:
        x_ref, w_ref, res_ref, o_ref, wbf_ref = refs
    else:
        x_ref, w_ref, o_ref, wbf_ref = refs
    j = pl.program_id(0)
    i = pl.program_id(1)
    rows_c = w_ref.shape[0]
    slot = j % 2
    wbf_ref[slot, pl.ds(pl.multiple_of(i * rows_c, rows_c), rows_c), :] = w_ref[...].astype(BF16)

    @pl.when(j > 0)
    def _():
        contract = (((1,), (1 if w_t else 0,)), ((), ()))
        acc = lax.dot_general(x_ref[...], wbf_ref[1 - slot], contract, preferred_element_type=F32)
        if has_res:
            acc = acc + res_ref[...]
        o_ref[...] = acc.astype(o_ref.dtype)


def _mm_stream(x, w, layer, *, n_out, tn, n_m, kblock=0, n_kblocks=1, res=None, out_dtype=F32, w_t=False):
    m = x.shape[0]
    k = x.shape[1] // n_kblocks
    tm, nj = m // n_m, n_out // tn
    rows_c = (tn if w_t else k) // n_m
    assert m % n_m == 0 and tm % 16 == 0 and rows_c * n_m == (tn if w_t else k) and rows_c % 16 == 0
    assert n_out % tn == 0 and w.shape[2 if w_t else 1] == k * n_kblocks == x.shape[1] and k % LANES == 0
    if w_t:
        w_spec = pl.BlockSpec((None, rows_c, k), lambda j, i: (layer, jnp.minimum(j, nj - 1) * n_m + i, kblock))
    else:
        w_spec = pl.BlockSpec((None, rows_c, tn),
                              lambda j, i: (layer, kblock * n_m + i, jnp.minimum(j, nj - 1)))
    in_specs = [pl.BlockSpec((tm, k), lambda j, i: (_first_rows(j, i), kblock)), w_spec]
    args = [x, w]
    out_map = lambda j, i: (_first_rows(j, i), jnp.maximum(j - 1, 0))
    if res is not None:
        in_specs.append(pl.BlockSpec((tm, tn), out_map))
        args.append(res)
    return pl.pallas_call(
        functools.partial(_mm_stream_kernel, has_res=res is not None, w_t=w_t),
        grid=(nj + 1, n_m),
        in_specs=in_specs,
        out_specs=pl.BlockSpec((tm, tn), out_map),
        out_shape=jax.ShapeDtypeStruct((m, n_out), out_dtype),
        scratch_shapes=[pltpu.VMEM((2, tn, k) if w_t else (2, k, tn), BF16)],
        compiler_params=_cparams(("arbitrary", "arbitrary")),
        name="mm_stream",
    )(*args)


FFN_PAIRS = 2


def _ffn_up_kernel(x_ref, *refs, tn):
    w_refs, (o_ref, wbf_ref) = refs[:2 * FFN_PAIRS], refs[2 * FFN_PAIRS:]
    j = pl.program_id(0)
    i = pl.program_id(1)
    rows_c = w_refs[0].shape[0]
    slot = j % 2
    rows = pl.ds(pl.multiple_of(i * rows_c, rows_c), rows_c)
    for c, w_ref in enumerate(w_refs):
        wbf_ref[slot, rows, c * tn:(c + 1) * tn] = w_ref[...].astype(BF16)

    @pl.when(j > 0)
    def _():
        gu = jnp.dot(x_ref[...], wbf_ref[1 - slot], preferred_element_type=F32)
        g = gu[:, :FFN_PAIRS * tn]
        o_ref[...] = (g * _sigmoid(g) * gu[:, FFN_PAIRS * tn:]).astype(o_ref.dtype)


def _ffn_up(x, w_gu, layer, *, n_m, tn):
    m, k = x.shape
    f = w_gu.shape[2] // 2
    tm, rows_c = m // n_m, k // n_m
    assert m % n_m == 0 and tm % 16 == 0 and k % n_m == 0 and rows_c % 16 == 0 and f % tn == 0
    n_pairs = f // tn
    nj = pl.cdiv(n_pairs, FFN_PAIRS)

    def w_map(c):
        g, part = c % FFN_PAIRS, c // FFN_PAIRS

        def index(j, i):
            pair = jnp.minimum(jnp.minimum(j, nj - 1) * FFN_PAIRS + g, n_pairs - 1)
            return layer, i, part * n_pairs + pair
        return index

    out_map = lambda j, i: (_first_rows(j, i), jnp.maximum(j - 1, 0))
    return pl.pallas_call(
        functools.partial(_ffn_up_kernel, tn=tn),
        grid=(nj + 1, n_m),
        in_specs=[pl.BlockSpec((tm, k), lambda j, i: (_first_rows(j, i), 0)),
                  *[pl.BlockSpec((None, rows_c, tn), w_map(c)) for c in range(2 * FFN_PAIRS)]],
        out_specs=pl.BlockSpec((tm, FFN_PAIRS * tn), out_map),
        out_shape=jax.ShapeDtypeStruct((m, f), BF16),
        scratch_shapes=[pltpu.VMEM((2, k, 2 * FFN_PAIRS * tn), BF16)],
        compiler_params=_cparams(("arbitrary", "arbitrary")),
        name="ffn_up",
    )(x, *([w_gu] * (2 * FFN_PAIRS)))


def _gates_kernel(x_ref, w_ref, alog_ref, dtb_ref, o_ref, *, n_heads, n_prompt_blocks, c_prompt, c_sample):
    tm = x_ref.shape[0]
    ba = lax.dot_general(w_ref[...], x_ref[...], (((1,), (1,)), ((), ())), preferred_element_type=F32)
    beta = _sigmoid(ba[:n_heads])
    a = ba[n_heads:] + dtb_ref[...]
    softplus = jnp.maximum(a, 0.0) + jnp.log(1.0 + jnp.exp(-jnp.abs(a)))
    g = -jnp.exp(alog_ref[...]) * softplus
    shift = jnp.where(pl.program_id(0) < n_prompt_blocks,
                      int(math.log2(c_prompt)), int(math.log2(c_sample)))
    jj = lax.broadcasted_iota(jnp.int32, (tm, tm), 0)
    ii = lax.broadcasted_iota(jnp.int32, (tm, tm), 1)
    same = lax.shift_right_logical(jj, shift) == lax.shift_right_logical(ii, shift)
    cum_m = jnp.where(same & (jj <= ii), 1.0, 0.0).astype(F32)
    tot_m = jnp.where(same, 1.0, 0.0).astype(F32)
    gcum = jnp.dot(g, cum_m, preferred_element_type=F32, precision=lax.Precision.HIGHEST)
    glast = jnp.dot(g, tot_m, preferred_element_type=F32, precision=lax.Precision.HIGHEST)
    o_ref[0] = beta
    o_ref[1] = jnp.exp(gcum)
    o_ref[2] = jnp.exp(glast - gcum)
    o_ref[3] = gcum


def _gates(xn, w_ba_t, a_log, dt_bias, *, tm, n_prompt_rows, c_prompt, c_sample):
    m, k = xn.shape
    h = a_log.shape[0]
    assert m % tm == 0 and n_prompt_rows % tm == 0 and tm % c_prompt == 0 and tm % c_sample == 0
    assert w_ba_t.shape == (2 * h, k)
    return pl.pallas_call(
        functools.partial(_gates_kernel, n_heads=h, n_prompt_blocks=n_prompt_rows // tm,
                          c_prompt=c_prompt, c_sample=c_sample),
        grid=(m // tm,),
        in_specs=[pl.BlockSpec((tm, k), lambda i: (i, 0)),
                  pl.BlockSpec((2 * h, k), lambda i: (0, 0)),
                  pl.BlockSpec((h, 1), lambda i: (0, 0)),
                  pl.BlockSpec((h, 1), lambda i: (0, 0))],
        out_specs=pl.BlockSpec((4, h, tm), lambda i: (0, 0, i)),
        out_shape=jax.ShapeDtypeStruct((4, h, m), F32),
        compiler_params=_cparams(("arbitrary",)),
        name="gates",
    )(xn, w_ba_t, a_log.reshape(h, 1), dt_bias.reshape(h, 1))


def _delta_kernel(q_ref, k_ref, v_ref, z_ref, cwq_ref, cwk_ref, cwv_ref, csq_ref, csk_ref, csv_ref,
                  gcol_ref, grow_ref, h0_ref, onw_ref,
                  o_ref, hout_ref, cqo_ref, cko_ref, cvo_ref,
                  h_scr, xq_scr, xk_scr, xv_scr, aq_scr, ak_scr, av_scr,
                  *, C, nb, nc, hb, dk, zero_init):
    t = pl.program_id(2)
    n_t = pl.num_programs(2)
    ttb = nc * C
    hist0 = CONV_PAD - (CONV_W - 1)

    @pl.when(t == 0)
    def _():
        if zero_init:
            h_scr[...] = jnp.zeros(h_scr.shape, F32)
        else:
            h_scr[...] = h0_ref[...]
        for xs, cs in ((xq_scr, csq_ref), (xk_scr, csk_ref), (xv_scr, csv_ref)):
            for bi in range(nb):
                xs[bi, hist0:CONV_PAD, :] = cs[bi]

    streams = ((q_ref, xq_scr, cwq_ref, aq_scr), (k_ref, xk_scr, cwk_ref, ak_scr), (v_ref, xv_scr, cwv_ref, av_scr))
    for raw_ref, xs, _, _ in streams:
        for bi in range(nb):
            xs[bi, CONV_PAD:CONV_PAD + ttb, :] = raw_ref[bi * ttb:(bi + 1) * ttb, :]

    def conv_phases(grp):
        r0, n = grp[0] * C, len(grp) * C

        def phase(xs, cw_ref, act):
            def run():
                for bi in range(nb):
                    acc = xs[bi, hist0 + r0:hist0 + r0 + n, :] * cw_ref[0:1, :]
                    for i in range(1, CONV_W):
                        acc = acc + xs[bi, hist0 + r0 + i:hist0 + r0 + i + n, :] * cw_ref[i:i + 1, :]
                    act[bi * ttb + r0:bi * ttb + r0 + n, :] = acc * _sigmoid(acc)
            return run
        return [phase(xs, cw_ref, act) for _, xs, cw_ref, act in streams]

    ii = lax.broadcasted_iota(jnp.int32, (C, C), 0)
    jj = lax.broadcasted_iota(jnp.int32, (C, C), 1)
    n_levels = int(math.log2(C))
    assert 2 ** n_levels == C

    def prepare(bi, c, j):
        rows = slice(bi * ttb + c * C, bi * ttb + (c + 1) * C)
        cols = slice(j * dk, (j + 1) * dk)
        q = aq_scr[rows, cols]
        k = ak_scr[rows, cols]
        qn = q * (lax.rsqrt(jnp.sum(q * q, axis=-1, keepdims=True) + EPS) * (dk ** -0.5))
        kn = k * lax.rsqrt(jnp.sum(k * k, axis=-1, keepdims=True) + EPS)
        gcb = gcol_ref[0, rows, :]
        beta = gcb[:, j:j + 1]
        eg = gcb[:, hb + j:hb + j + 1]
        ekl = gcb[:, 2 * hb + j:2 * hb + j + 1]
        gc = gcb[:, 3 * hb + j:3 * hb + j + 1]
        gr = grow_ref[0, bi * nc + c, j:j + 1, :]
        dmat = jnp.where(ii >= jj, jnp.exp(gc - gr), 0.0)
        rhs = jnp.concatenate([kn * (beta * eg), av_scr[rows, cols] * beta], axis=1)
        return dict(rows=rows, cols=cols, qn=qn, kn=kn, beta=beta, eg=eg, dmat=dmat, rhs=rhs,
                    qe=qn * eg, kd=kn * ekl)

    st = {}

    def solve_phases(group):
        def scores():
            for it in group:
                e = st[it] = prepare(*it)
                e["s"] = _bdot_nt(jnp.concatenate([e["kn"], e["qn"]], axis=0), e["kn"])

        def square():
            for it in group:
                e = st[it]
                s = e.pop("s")
                e["qk"] = s[C:] * e["dmat"]
                e["t"] = jnp.where(ii > jj, s[:C] * e["dmat"] * (-e["beta"]), 0.0)
                e["p"] = _bdot(e["t"], e["t"])

        def level(last):
            def run():
                for it in group:
                    e = st[it]
                    e["tp"] = _bdot(e["t"], e["p"])
                    if not last:
                        e["p2"] = _bdot(e["p"], e["p"])
                for it in group:
                    e = st[it]
                    e["t"] = e["t"] + e["p"] + e.pop("tp")
                    e["p"] = None if last else e.pop("p2")
            return run

        def apply_t():
            for it in group:
                e = st[it]
                e["wu"] = e["rhs"] + _bdot(e["t"], e["rhs"])

        return [scores, square, *[level(lvl == n_levels - 2) for lvl in range(n_levels - 1)], apply_t]

    h_cur = {(bi, j): h_scr[bi, j] for bi in range(nb) for j in range(hb)}

    def chain_phases(c):
        sel = [(bi, c, j) for bi in range(nb) for j in range(hb)]

        def through_state():
            for it in sel:
                e = st[it]
                e["wq"] = _bdot(jnp.concatenate([e["wu"][:, :dk], e["qe"]], axis=0), h_cur[it[0], it[2]])

        def update():
            for it in sel:
                e = st[it]
                e["u"] = e["wu"][:, dk:] - e["wq"][:C]
            for it in sel:
                e = st[it]
                e["o"] = e["wq"][C:] + _bdot(e["qk"], e["u"])
                egl = e["eg"][C - 1:C, :]
                h_cur[it[0], it[2]] = h_cur[it[0], it[2]] * egl + _bdot_tn(e["kd"], e["u"])

        def emit():
            for it in sel:
                e = st.pop(it)
                o = e["o"]
                on = o * lax.rsqrt(jnp.mean(o * o, axis=-1, keepdims=True) + EPS) * onw_ref[...]
                zz = z_ref[e["rows"], e["cols"]]
                o_ref[e["rows"], e["cols"]] = (on * (zz * _sigmoid(zz))).astype(o_ref.dtype)

        return [through_state, update, emit]

    chunk_groups = [list(range(c0, min(c0 + GROUP_CHUNKS, nc))) for c0 in range(0, nc, GROUP_CHUNKS)]
    n_grp = len(chunk_groups)
    convs = [conv_phases(grp) for grp in chunk_groups]
    solves = [solve_phases([(bi, c, j) for bi in range(nb) for c in grp for j in range(hb)])
              for grp in chunk_groups]
    chains = [[phase for c in grp for phase in chain_phases(c)] for grp in chunk_groups]
    for g in range(-2, n_grp):
        tracks = [convs[g + 2] if g + 2 < n_grp else [],
                  solves[g + 1] if 0 <= g + 1 < n_grp else [],
                  chains[g] if g >= 0 else []]
        for phases in itertools.zip_longest(*tracks):
            for phase in phases:
                if phase is not None:
                    phase()
    for (bi, j), hv in h_cur.items():
        h_scr[bi, j] = hv
    for _, xs, _, _ in streams:
        for bi in range(nb):
            xs[bi, hist0:CONV_PAD, :] = xs[bi, CONV_PAD + ttb - (CONV_W - 1):CONV_PAD + ttb, :]

    @pl.when(t == n_t - 1)
    def _():
        for (_, xs, _, _), co in zip(streams, (cqo_ref, cko_ref, cvo_ref)):
            for bi in range(nb):
                co[bi] = xs[bi, hist0:CONV_PAD, :]

    @pl.when(t == n_t - 1)
    def _():
        hout_ref[...] = h_scr[...]


def _delta_rule(proj, gates, conv_w, conv0, h0, o_norm_w, *, row0, batch, seq, n_heads, dk,
                C, nb, nc, hb, out_rows):
    ttb = nc * C
    tt = nb * ttb
    w = hb * dk
    ng = n_heads // hb
    n_t = seq // ttb
    assert seq % ttb == 0 and batch % nb == 0 and n_heads % hb == 0 and row0 % tt == 0
    zero_init = h0 is None
    d_qk = n_heads * dk
    cb = d_qk // w
    rb0 = row0 // tt

    rows = batch * seq
    g5 = gates.reshape(4, ng, hb, rows)
    gcol = jnp.transpose(g5, (1, 3, 0, 2)).reshape(ng, rows, 4 * hb)
    grow = jnp.transpose(g5[3].reshape(ng, hb, rows // C, C), (0, 2, 1, 3))

    def tok_map(part):
        return lambda b, g, t: (rb0 + b * n_t + t, part * cb + g)

    if zero_init:
        h0_arg = jnp.zeros((nb, hb, dk, dk), F32)
        h0_spec = pl.BlockSpec((nb, hb, dk, dk), lambda b, g, t: (0, 0, 0, 0))
    else:
        h0_arg = h0
        h0_spec = pl.BlockSpec((nb, hb, dk, dk), lambda b, g, t: (b, g, 0, 0))
    cs_spec = [pl.BlockSpec((nb, CONV_W - 1, w), (lambda b, g, t, p=p: (b, 0, p * cb + g))) for p in range(3)]
    cw_spec = [pl.BlockSpec((CONV_W, w), (lambda b, g, t, p=p: (0, p * cb + g))) for p in range(3)]
    kern = functools.partial(_delta_kernel, C=C, nb=nb, nc=nc, hb=hb, dk=dk, zero_init=zero_init)
    o, h_fin, cq, ck, cv = pl.pallas_call(
        kern,
        grid=(batch // nb, ng, n_t),
        in_specs=[pl.BlockSpec((tt, w), tok_map(0)), pl.BlockSpec((tt, w), tok_map(1)),
                  pl.BlockSpec((tt, w), tok_map(2)), pl.BlockSpec((tt, w), tok_map(3)),
                  *cw_spec, *cs_spec,
                  pl.BlockSpec((1, tt, 4 * hb), lambda b, g, t: (g, b * n_t + t, 0)),
                  pl.BlockSpec((1, nb * nc, hb, C), lambda b, g, t: (g, b * n_t + t, 0, 0)),
                  h0_spec,
                  pl.BlockSpec((1, dk), lambda b, g, t: (0, 0))],
        out_specs=[pl.BlockSpec((tt, w), lambda b, g, t: (rb0 + b * n_t + t, g)),
                   pl.BlockSpec((nb, hb, dk, dk), lambda b, g, t: (b, g, 0, 0)),
                   *[pl.BlockSpec((nb, CONV_W - 1, w), lambda b, g, t: (b, 0, g)) for _ in range(3)]],
        out_shape=[jax.ShapeDtypeStruct((out_rows, d_qk), BF16),
                   jax.ShapeDtypeStruct((batch, n_heads, dk, dk), F32),
                   *[jax.ShapeDtypeStruct((batch, CONV_W - 1, d_qk), F32) for _ in range(3)]],
        scratch_shapes=[pltpu.VMEM((nb, hb, dk, dk), F32),
                        *[pltpu.VMEM((nb, CONV_PAD + ttb, w), F32) for _ in range(3)],
                        *[pltpu.VMEM((tt, w), F32) for _ in range(3)]],
        compiler_params=_cparams(("arbitrary", "arbitrary", "arbitrary")),
        name="delta_rule",
    )(proj, proj, proj, proj, conv_w, conv_w, conv_w, conv0, conv0, conv0, gcol, grow, h0_arg,
      o_norm_w.reshape(1, dk))
    return o, h_fin, jnp.concatenate([cq, ck, cv], axis=-1)


def _group_mean_sq(x, gmat_ref, inv_n):
    x2 = x * x
    hi = x2.astype(BF16)
    lo = (x2 - hi.astype(F32)).astype(BF16)
    g = gmat_ref[...]
    return (jnp.dot(hi, g, preferred_element_type=F32) + jnp.dot(lo, g, preferred_element_type=F32)) * inv_n


def _rope(y, cos_ref, sa_ref, sb_ref, half):
    w = y.shape[1]
    reps = w // cos_ref.shape[1]
    tile = lambda r: jnp.concatenate([r[...]] * reps, axis=1)
    return (y * tile(cos_ref) + pltpu.roll(y, w - half, axis=1) * tile(sa_ref)
            + pltpu.roll(y, half, axis=1) * tile(sb_ref))


def _q_post_kernel(x_ref, w_ref, gmat_ref, cos_ref, sa_ref, sb_ref, o_ref, *, hd, half, scale):
    gw = gmat_ref.shape[0]
    for c in range(x_ref.shape[1] // gw):
        cols = slice(c * gw, (c + 1) * gw)
        x = x_ref[:, cols]
        y = x * lax.rsqrt(_group_mean_sq(x, gmat_ref, 1.0 / hd) + EPS) * w_ref[...]
        o_ref[:, cols] = (_rope(y, cos_ref, sa_ref, sb_ref, half) * scale).astype(o_ref.dtype)


def _kv_post_kernel(k_ref, v_ref, w_ref, gmat_ref, cos_ref, sa_ref, sb_ref,
                    kf_ref, kb_ref, vbt_ref, kwin_ref, vwin_ref, *, hd, half, n_prompt_blocks, blocks_per_seq):
    i = pl.program_id(0)
    x = k_ref[...]
    y = x * lax.rsqrt(_group_mean_sq(x, gmat_ref, 1.0 / hd) + EPS) * w_ref[...]
    kf = _rope(y, cos_ref, sa_ref, sb_ref, half)
    kf_ref[...] = kf
    v = v_ref[...]
    vt = v.T
    for hh in range(kb_ref.shape[0]):
        kb_ref[hh] = kf[:, hh * hd:(hh + 1) * hd].astype(BF16)
        vbt_ref[hh] = vt[hh * hd:(hh + 1) * hd, :].astype(BF16)

    @pl.when((i < n_prompt_blocks) & (i % blocks_per_seq == blocks_per_seq - 1))
    def _():
        wb = kwin_ref.shape[0]
        kwin_ref[...] = kf[kf.shape[0] - wb:, :]
        vwin_ref[...] = v[v.shape[0] - wb:, :]


def _rope_tables(pos, hd, rot_dim):
    half = rot_dim // 2
    inv_freq = jnp.power(ROPE_THETA, -jnp.arange(half, dtype=F32) * 2.0 / rot_dim)
    ang = pos[:, None] * inv_freq[None, :]
    cos, sin = jnp.cos(ang), jnp.sin(ang)
    m = pos.shape[0]
    ones = jnp.ones((m, hd - rot_dim), F32)
    zeros = jnp.zeros((m, hd - rot_dim), F32)
    zh = jnp.zeros((m, half), F32)
    c = jnp.concatenate([cos, cos, ones], axis=1)
    sa = jnp.concatenate([-sin, zh, zeros], axis=1)
    sb = jnp.concatenate([zh, sin, zeros], axis=1)
    reps = LANES // hd
    return tuple(jnp.tile(a, (1, reps)) for a in (c, sa, sb))


def _group_matrix(width, hd):
    r = jnp.arange(width) // hd
    return (r[:, None] == r[None, :]).astype(BF16)


def _softmax_sink(s, mask, sink):
    s = jnp.where(mask, s, -jnp.inf)
    m = jnp.maximum(jnp.max(s, axis=-1, keepdims=True), sink)
    p = jnp.exp(s - m)
    denom = jnp.sum(p, axis=-1, keepdims=True) + jnp.exp(sink - m)
    return (p / denom).astype(BF16)


def _attn_prompt_kernel(sink_ref, q_ref, kp_ref, kc_ref, vp_ref, vc_ref, o_ref, *, kv_heads, group, hd, win):
    nb = pl.program_id(1)
    kj = lax.broadcasted_iota(jnp.int32, (2 * win, win), 0)
    qi = lax.broadcasted_iota(jnp.int32, (2 * win, win), 1)
    mask = (kj > qi) & (kj <= qi + win) & ((kj >= win) | (nb > 0))

    def scores(h):
        kk = jnp.concatenate([kp_ref[h], kc_ref[h]], axis=0)
        return [lax.dot_general(kk, q_ref[:, (h * group + g) * hd:(h * group + g + 1) * hd],
                                (((1,), (1,)), ((), ())), preferred_element_type=F32) for g in range(group)]

    def row_max(h, ss):
        out = []
        for g, s in enumerate(ss):
            s = jnp.where(mask, s, -jnp.inf)
            out.append((s, jnp.maximum(jnp.max(s, axis=0, keepdims=True), sink_ref[h * group + g])))
        return out

    def probs(h, sm):
        out = []
        for g, (s, m) in enumerate(sm):
            p = jnp.exp(s - m)
            denom = jnp.sum(p, axis=0, keepdims=True) + jnp.exp(sink_ref[h * group + g] - m)
            out.append((p / denom).astype(BF16))
        return out

    def weighted_values(h, ps):
        vt = jnp.concatenate([vp_ref[h], vc_ref[h]], axis=1)
        ot = jnp.concatenate([jnp.dot(vt, p, preferred_element_type=F32) for p in ps], axis=0)
        o_ref[:, h * group * hd:(h + 1) * group * hd] = ot.T.astype(o_ref.dtype)

    stages = (scores, row_max, probs, weighted_values)
    carry = [None] * len(stages)
    for step in range(kv_heads + len(stages) - 1):
        for k in reversed(range(len(stages))):
            h = step - k
            if 0 <= h < kv_heads:
                carry[k] = stages[k](h) if k == 0 else stages[k](h, carry[k - 1])


def _attn_prompt(q, kb, vbt, sinks, *, batch, seq, kv_heads, group, hd, out_rows):
    win = WINDOW
    nblk = seq // win
    qw = kv_heads * group * hd
    k_spec = lambda back: pl.BlockSpec((kv_heads, win, hd),
                                       lambda b, n: (0, b * nblk + jnp.maximum(n - back, 0), 0))
    v_spec = lambda back: pl.BlockSpec((kv_heads, hd, win),
                                       lambda b, n: (0, 0, b * nblk + jnp.maximum(n - back, 0)))
    return pl.pallas_call(
        functools.partial(_attn_prompt_kernel, kv_heads=kv_heads, group=group, hd=hd, win=win),
        grid=(batch, nblk),
        in_specs=[pl.BlockSpec(memory_space=pltpu.SMEM),
                  pl.BlockSpec((win, qw), lambda b, n: (b * nblk + n, 0)),
                  k_spec(1), k_spec(0), v_spec(1), v_spec(0)],
        out_specs=pl.BlockSpec((win, qw), lambda b, n: (b * nblk + n, 0)),
        out_shape=jax.ShapeDtypeStruct((out_rows, qw), BF16),
        compiler_params=_cparams(("arbitrary", "arbitrary")),
        name="attn_prompt",
    )(sinks, q, kb, kb, vbt, vbt)


def _attn_sample_kernel(sink_ref, q_ref, kn_ref, vn_ref, kc_ref, vc_ref, prev_ref, o_ref,
                        *, nbs, tq, kv_heads, group, hd, win):
    del prev_ref
    nctx = 2 * win
    r = lax.broadcasted_iota(jnp.int32, (group * tq, nctx), 0)
    kj = lax.broadcasted_iota(jnp.int32, (group * tq, nctx), 1)
    tpos = r % tq
    mask = ((kj < win) & (kj > tpos)) | ((kj >= win) & (kj - win <= tpos))
    pad = jnp.zeros((nctx - win - tq, hd), F32)
    items = [(bi, h) for bi in range(nbs) for h in range(kv_heads)]
    qbs = [q_ref[bi * tq:(bi + 1) * tq, :].astype(F32) for bi in range(nbs)]
    sinks = [jnp.concatenate([jnp.full((tq, 1), sink_ref[h * group + g], F32) for g in range(group)], axis=0)
             for h in range(kv_heads)]
    ss, vs = [], []
    for bi, h in items:
        cs = slice(h * hd, (h + 1) * hd)
        trows = slice(bi * tq, (bi + 1) * tq)
        kctx = jnp.concatenate([kc_ref[bi][:, cs], kn_ref[trows, cs], pad], axis=0)
        vs.append(jnp.concatenate([vc_ref[bi][:, cs], vn_ref[trows, cs], pad], axis=0).astype(BF16))
        qg = jnp.concatenate([qbs[bi][:, (h * group + g) * hd:(h * group + g + 1) * hd]
                              for g in range(group)], axis=0)
        ss.append(_bdot_nt(qg, kctx))
    ps = [_softmax_sink(s, mask, sinks[h]) for (bi, h), s in zip(items, ss)]
    for (bi, h), p, v in zip(items, ps, vs):
        o = jnp.dot(p, v, preferred_element_type=F32)
        for g in range(group):
            hh = h * group + g
            o_ref[bi * tq:(bi + 1) * tq, hh * hd:(hh + 1) * hd] = o[g * tq:(g + 1) * tq, :].astype(o_ref.dtype)


def _attn_sample(q, kf, kv, cache_k, cache_v, sinks, o_prev, *, row0, batch, tq, kv_heads, group, hd, nbs):
    win = cache_k.shape[1]
    assert win == WINDOW and batch % nbs == 0 and row0 % (nbs * tq) == 0
    rows = nbs * tq
    rb0 = row0 // rows
    kvw = kv_heads * hd
    return pl.pallas_call(
        functools.partial(_attn_sample_kernel, nbs=nbs, tq=tq, kv_heads=kv_heads, group=group, hd=hd, win=win),
        grid=(batch // nbs,),
        in_specs=[pl.BlockSpec(memory_space=pltpu.SMEM),
                  pl.BlockSpec((rows, q.shape[1]), lambda b: (rb0 + b, 0)),
                  pl.BlockSpec((rows, kvw), lambda b: (rb0 + b, 0)),
                  pl.BlockSpec((rows, kvw), lambda b: (rb0 + b, 1)),
                  pl.BlockSpec((nbs, win, kvw), lambda b: (b, 0, 0)),
                  pl.BlockSpec((nbs, win, kvw), lambda b: (b, 0, 0)),
                  pl.BlockSpec(memory_space=pl.ANY)],
        out_specs=pl.BlockSpec((rows, q.shape[1]), lambda b: (rb0 + b, 0)),
        out_shape=jax.ShapeDtypeStruct(o_prev.shape, o_prev.dtype),
        input_output_aliases={6: 0},
        compiler_params=_cparams(("arbitrary",)),
        name="attn_sample",
    )(sinks, q, kf, kv, cache_k, cache_v, o_prev)


PAST_LEN = 16384
TM_NORM = 256
M_STEPS = 8


Q_POST_COLS = 2048
TN_WIDE = 1024
TN_RES = 512
TN_FFN = 256
DOWN_KBLOCKS = 2


def _write_rows_kernel(src_ref, dst_ref, o_ref):
    del dst_ref
    o_ref[...] = src_ref[...]


def _merge_rows(dst, src, row0, tm):
    n = src.shape[1]
    rows = src.shape[0] - row0
    assert rows % tm == 0 and row0 % tm == 0
    return pl.pallas_call(
        _write_rows_kernel,
        grid=(rows // tm,),
        in_specs=[pl.BlockSpec((tm, n), lambda i: (row0 // tm + i, 0)),
                  pl.BlockSpec(memory_space=pl.ANY)],
        out_specs=pl.BlockSpec((tm, n), lambda i: (row0 // tm + i, 0)),
        out_shape=jax.ShapeDtypeStruct(dst.shape, dst.dtype),
        input_output_aliases={1: 0},
        compiler_params=_cparams(("arbitrary",)),
        name="merge_rows",
    )(src, dst)


def kernel(x_prompt, x_sample, state_ssm, state_conv, cache_win_k, cache_win_v, norm_mix, norm_ffn,
           a_w_in, a_conv_w, a_log, a_dt_bias, a_o_norm, a_w_out, kv_norm, w_kv, k_norm, b_w_q,
           b_q_norm, b_sinks, b_w_o, ffn_w_gu, ffn_w_down):
    bp, sp, d = x_prompt.shape
    bs, ss, _ = x_sample.shape
    mp, ms = bp * sp, bs * ss
    m = mp + ms
    n_a = a_w_in.shape[0]
    assert n_a == 1 and b_w_q.shape[0] == 1, "layer pattern is one delta layer then one attention layer"
    h_a = a_log.shape[1]
    dk = a_o_norm.shape[1]
    d_qk = h_a * dk
    kv_heads, hd = cache_win_k.shape[2], cache_win_k.shape[3]
    kvw = kv_heads * hd
    h_b = b_sinks.shape[1]
    group = h_b // kv_heads
    rot_dim = hd // 4

    def mm(xb, w, n_out, res=None):
        tn = min(TN_WIDE if res is None else TN_RES, n_out)
        return _mm_stream(xb, w, 0, n_out=n_out, tn=tn, n_m=M_STEPS, res=res)

    def ffn(hin, layer):
        (hn,) = _norm(hin, norm_ffn[layer:layer + 1], TM_NORM)
        act = _ffn_up(hn, ffn_w_gu, layer, n_m=M_STEPS, tn=TN_FFN)
        out = hin
        for kb in range(DOWN_KBLOCKS):
            out = _mm_stream(act, ffn_w_down, layer, n_out=d, tn=min(TN_RES, d), n_m=M_STEPS,
                             kblock=kb, n_kblocks=DOWN_KBLOCKS, res=out)
        return out

    xn, x = _norm_first(x_prompt.reshape(mp, d), x_sample.reshape(ms, d), norm_mix[0], TM_NORM)
    w_in_t = jnp.swapaxes(a_w_in, 1, 2)
    proj = _mm_stream(xn, w_in_t, 0, n_out=4 * d_qk, tn=TN_WIDE, n_m=M_STEPS, w_t=True)
    w_ba_t = w_in_t[0, 4 * d_qk:, :].astype(BF16)
    c_s = math.gcd(ss, DELTA_CHUNK)
    gates = _gates(xn, w_ba_t, a_log[0], a_dt_bias[0], tm=TM_NORM, n_prompt_rows=mp,
                   c_prompt=DELTA_CHUNK, c_sample=c_s)
    conv_w = a_conv_w[0]
    conv0_p = jnp.zeros((bp, CONV_W - 1, 3 * d_qk), F32)
    o_p, ssm_p, conv_p = _delta_rule(proj, gates[:, :, :mp], conv_w, conv0_p, None, a_o_norm[0],
                                     row0=0, batch=bp, seq=sp, n_heads=h_a, dk=dk,
                                     C=DELTA_CHUNK, nb=1, nc=4, hb=8, out_rows=m)
    o_s, ssm_s, conv_s = _delta_rule(proj, gates[:, :, mp:], conv_w, state_conv[0], state_ssm[0], a_o_norm[0],
                                     row0=mp, batch=bs, seq=ss, n_heads=h_a, dk=dk,
                                     C=c_s, nb=2, nc=1, hb=8, out_rows=m)
    o_a = _merge_rows(o_p, o_s, mp, TM_NORM)
    h1 = mm(o_a, a_w_out, d, res=x)
    h2 = ffn(h1, 0)

    hkv, hq = _norm(h2, jnp.stack([kv_norm, norm_mix[1]]), TM_NORM)
    kv = mm(hkv, w_kv[None], 2 * kvw)
    pos = jnp.concatenate([jnp.tile(jnp.arange(sp, dtype=F32), bp),
                           jnp.tile(PAST_LEN + jnp.arange(ss, dtype=F32), bs)])
    cos_t, sa_t, sb_t = _rope_tables(pos, hd, rot_dim)
    gmat = _group_matrix(kvw, hd)
    wb = min(WINDOW, sp)
    bps = sp // TM_NORM
    assert sp % TM_NORM == 0 and wb <= TM_NORM
    win_spec = pl.BlockSpec((wb, kvw), lambda i: (jnp.minimum(i // bps, bp - 1), 0))
    kf, kb, vbt, kwin_p, vwin_p = pl.pallas_call(
        functools.partial(_kv_post_kernel, hd=hd, half=rot_dim // 2, n_prompt_blocks=mp // TM_NORM,
                          blocks_per_seq=bps),
        grid=(m // TM_NORM,),
        in_specs=[pl.BlockSpec((TM_NORM, kvw), lambda i: (i, 0)),
                  pl.BlockSpec((TM_NORM, kvw), lambda i: (i, 1)),
                  pl.BlockSpec((1, kvw), lambda i: (0, 0)),
                  pl.BlockSpec((kvw, kvw), lambda i: (0, 0)),
                  *[pl.BlockSpec((TM_NORM, LANES), lambda i: (i, 0)) for _ in range(3)]],
        out_specs=[pl.BlockSpec((TM_NORM, kvw), lambda i: (i, 0)),
                   pl.BlockSpec((kv_heads, TM_NORM, hd), lambda i: (0, i, 0)),
                   pl.BlockSpec((kv_heads, hd, TM_NORM), lambda i: (0, 0, i)),
                   win_spec, win_spec],
        out_shape=[jax.ShapeDtypeStruct((m, kvw), F32),
                   jax.ShapeDtypeStruct((kv_heads, m, hd), BF16),
                   jax.ShapeDtypeStruct((kv_heads, hd, m), BF16),
                   jax.ShapeDtypeStruct((bp * wb, kvw), F32),
                   jax.ShapeDtypeStruct((bp * wb, kvw), F32)],
        compiler_params=_cparams(("arbitrary",)),
        name="kv_post",
    )(kv, kv, jnp.tile(k_norm, kv_heads).reshape(1, kvw), gmat, cos_t, sa_t, sb_t)

    qraw = mm(hq, b_w_q, h_b * hd)
    qw = kvw
    qcols = h_b * hd
    qblk = min(Q_POST_COLS, qcols)
    assert qcols % qblk == 0 and qblk % qw == 0
    q = pl.pallas_call(
        functools.partial(_q_post_kernel, hd=hd, half=rot_dim // 2, scale=hd ** -0.5),
        grid=(m // TM_NORM, qcols // qblk),
        in_specs=[pl.BlockSpec((TM_NORM, qblk), lambda i, j: (i, j)),
                  pl.BlockSpec((1, qw), lambda i, j: (0, 0)),
                  pl.BlockSpec((qw, qw), lambda i, j: (0, 0)),
                  *[pl.BlockSpec((TM_NORM, LANES), lambda i, j: (i, 0)) for _ in range(3)]],
        out_specs=pl.BlockSpec((TM_NORM, qblk), lambda i, j: (i, j)),
        out_shape=jax.ShapeDtypeStruct((m, qcols), BF16),
        compiler_params=_cparams(("arbitrary", "arbitrary")),
        name="q_post",
    )(qraw, jnp.tile(b_q_norm[0], qw // hd).reshape(1, qw), gmat, cos_t, sa_t, sb_t)
    sinks = b_sinks[0]
    o_b = _attn_prompt(q, kb, vbt, sinks, batch=bp, seq=sp, kv_heads=kv_heads, group=group, hd=hd, out_rows=m)
    o_b = _attn_sample(q, kf, kv, cache_win_k.reshape(bs, WINDOW, kvw), cache_win_v.reshape(bs, WINDOW, kvw),
                       sinks, o_b, row0=mp, batch=bs, tq=ss, kv_heads=kv_heads, group=group, hd=hd, nbs=2)
    h3 = mm(o_b, b_w_o, d, res=h2)
    h4 = ffn(h3, 1)

    kf_p = kwin_p.reshape(bp, wb, kv_heads, hd)
    vf_p = vwin_p.reshape(bp, wb, kv_heads, hd)
    kf_s = kf[mp:].reshape(bs, ss, kv_heads, hd)
    vf_s = kv[mp:, kvw:].reshape(bs, ss, kv_heads, hd)
    wk_s = jnp.concatenate([cache_win_k[:, ss:], kf_s], axis=1)
    wv_s = jnp.concatenate([cache_win_v[:, ss:], vf_s], axis=1)
    return (h4[:mp].reshape(bp, sp, d), h4[mp:].reshape(bs, ss, d),
            ssm_p[None], conv_p[None], kf_p, vf_p, ssm_s[None], conv_s[None], wk_s, wv_s)
```

```python
import functools
import itertools
import math

import jax
import jax.numpy as jnp
from jax import lax
from jax.experimental import pallas as pl
from jax.experimental.pallas import tpu as pltpu

F32 = jnp.float32
BF16 = jnp.bfloat16

EPS = 1e-6
WINDOW = 128
ROPE_THETA = 500000.0
CONV_W = 4
DELTA_CHUNK = 64
LANES = 128
CONV_PAD = 8
GROUP_CHUNKS = 2
VMEM_LIMIT = 56 * 1024 * 1024


def _cparams(sem):
    return pltpu.CompilerParams(dimension_semantics=sem, vmem_limit_bytes=VMEM_LIMIT)


def _sigmoid(x):
    return 1.0 / (1.0 + jnp.exp(-x))


def _bdot(a, b):
    return jnp.dot(a.astype(BF16), b.astype(BF16), preferred_element_type=F32)


def _bdot_nt(a, b):
    return lax.dot_general(a.astype(BF16), b.astype(BF16), (((1,), (1,)), ((), ())),
                           preferred_element_type=F32)


def _bdot_tn(a, b):
    return lax.dot_general(a.astype(BF16), b.astype(BF16), (((0,), (0,)), ((), ())),
                           preferred_element_type=F32)


def _norm_inputs_kernel(xp_ref, xs_ref, w_ref, *refs, n_prompt_blocks, has_y):
    i = pl.program_id(0)

    def emit(x):
        if has_y:
            y_ref, xn_ref, h_ref = refs
            x = x + y_ref[...]
            h_ref[...] = x
        else:
            (xn_ref,) = refs
        r = lax.rsqrt(jnp.mean(x * x, axis=-1, keepdims=True) + EPS)
        xn_ref[...] = (x * r * w_ref[...]).astype(xn_ref.dtype)

    @pl.when(i < n_prompt_blocks)
    def _():
        emit(xp_ref[...])

    @pl.when(i >= n_prompt_blocks)
    def _():
        emit(xs_ref[...])


def _norm_inputs(xp, xs, w, tm, y=None):
    mp, d = xp.shape
    ms = xs.shape[0]
    assert mp % tm == 0 and ms == tm
    npb = mp // tm
    m = mp + ms
    row = pl.BlockSpec((tm, d), lambda i: (i, 0))
    return pl.pallas_call(
        functools.partial(_norm_inputs_kernel, n_prompt_blocks=npb, has_y=y is not None),
        grid=(npb + 1,),
        in_specs=[pl.BlockSpec((tm, d), lambda i: (jnp.minimum(i, npb - 1), 0)),
                  pl.BlockSpec((tm, d), lambda i: (0, 0)),
                  pl.BlockSpec((1, d), lambda i: (0, 0))] + ([row] if y is not None else []),
        out_specs=[row] + ([row] if y is not None else []),
        out_shape=[jax.ShapeDtypeStruct((m, d), BF16)] + ([jax.ShapeDtypeStruct((m, d), F32)] if y is not None else []),
        compiler_params=_cparams(("arbitrary",)),
        name="norm_inputs",
    )(xp, xs, w.reshape(1, d), *([y] if y is not None else []))


def _norm_kernel(x_ref, w_ref, *o_refs):
    x = x_ref[...]
    y = x * lax.rsqrt(jnp.mean(x * x, axis=-1, keepdims=True) + EPS)
    for i, o_ref in enumerate(o_refs):
        o_ref[...] = (y * w_ref[i:i + 1, :]).astype(o_ref.dtype)


def _norm(x, ws, tm):
    m, d = x.shape
    nw = ws.shape[0]
    assert m % tm == 0
    return pl.pallas_call(
        _norm_kernel,
        grid=(m // tm,),
        in_specs=[pl.BlockSpec((tm, d), lambda i: (i, 0)),
                  pl.BlockSpec((nw, d), lambda i: (0, 0))],
        out_specs=[pl.BlockSpec((tm, d), lambda i: (i, 0)) for _ in range(nw)],
        out_shape=[jax.ShapeDtypeStruct((m, d), BF16) for _ in range(nw)],
        compiler_params=_cparams(("arbitrary",)),
        name="norm",
    )(x, ws)


def _first_rows(j, i):
    return jnp.where(j == 0, 0, i)


def _mm_stream_kernel(*refs, has_res, w_t, n_m, tail):
    x_ref, w_ref, *mid, wbf_ref = refs
    res_ref = mid[0] if has_res else None
    out_refs = mid[1:] if has_res else mid
    j = pl.program_id(0)
    i = pl.program_id(1)
    rows_c = w_ref.shape[0]
    slot = j % 2
    ic = jnp.minimum(i, n_m - 1)
    wbf_ref[slot, pl.ds(pl.multiple_of(ic * rows_c, rows_c), rows_c), :] = w_ref[...].astype(BF16)

    def product(rows):
        contract = (((1,), (1 if w_t else 0,)), ((), ()))
        acc = lax.dot_general(x_ref[:rows, :], wbf_ref[1 - slot], contract, preferred_element_type=F32)
        if has_res:
            acc = acc + res_ref[:rows, :]
        return acc

    @pl.when((j > 0) & (i < n_m))
    def _():
        out_refs[0][...] = product(x_ref.shape[0]).astype(out_refs[0].dtype)

    if tail:
        @pl.when((j > 0) & (i == n_m))
        def _():
            out_refs[1][...] = product(tail).astype(out_refs[1].dtype)


def _mm_stream(x, w, layer, *, n_out, tn, n_m, kblock=0, n_kblocks=1, res=None, out_dtype=F32, w_t=False,
               tail=0):
    m = x.shape[0]
    m_main = m - tail
    k = x.shape[1] // n_kblocks
    tm, nj = m_main // n_m, n_out // tn
    rows_c = (tn if w_t else k) // n_m
    assert m_main % n_m == 0 and tm % 16 == 0 and rows_c * n_m == (tn if w_t else k) and rows_c % 16 == 0
    assert n_out % tn == 0 and w.shape[2 if w_t else 1] == k * n_kblocks == x.shape[1] and k % LANES == 0
    assert 0 <= tail <= tm and tail % 8 == 0
    chunk = lambda i: jnp.minimum(i, n_m - 1)
    if w_t:
        w_spec = pl.BlockSpec((None, rows_c, k),
                              lambda j, i: (layer, jnp.minimum(j, nj - 1) * n_m + chunk(i), kblock))
    else:
        w_spec = pl.BlockSpec((None, rows_c, tn),
                              lambda j, i: (layer, kblock * n_m + chunk(i), jnp.minimum(j, nj - 1)))
    in_specs = [pl.BlockSpec((tm, k), lambda j, i: (_first_rows(j, i), kblock)), w_spec]
    args = [x, w]
    col = lambda j: jnp.maximum(j - 1, 0)
    if res is not None:
        in_specs.append(pl.BlockSpec((tm, tn), lambda j, i: (_first_rows(j, i), col(j))))
        args.append(res)
    out_specs = [pl.BlockSpec((tm, tn), lambda j, i: (jnp.minimum(_first_rows(j, i), n_m - 1), col(j)))]
    out_shape = [jax.ShapeDtypeStruct((m_main, n_out), out_dtype)]
    if tail:
        out_specs.append(pl.BlockSpec((tail, tn), lambda j, i: (0, col(j))))
        out_shape.append(jax.ShapeDtypeStruct((tail, n_out), out_dtype))
    outs = pl.pallas_call(
        functools.partial(_mm_stream_kernel, has_res=res is not None, w_t=w_t, n_m=n_m, tail=tail),
        grid=(nj + 1, n_m + (1 if tail else 0)),
        in_specs=in_specs,
        out_specs=out_specs,
        out_shape=out_shape,
        scratch_shapes=[pltpu.VMEM((2, tn, k) if w_t else (2, k, tn), BF16)],
        compiler_params=_cparams(("arbitrary", "arbitrary")),
        name="mm_stream",
    )(*args)
    return tuple(outs) if tail else outs[0]


FFN_PAIRS = 2


def _ffn_up_kernel(x_ref, *refs, tn):
    w_refs, (o_ref, wbf_ref) = refs[:2 * FFN_PAIRS], refs[2 * FFN_PAIRS:]
    j = pl.program_id(0)
    i = pl.program_id(1)
    rows_c = w_refs[0].shape[0]
    slot = j % 2
    rows = pl.ds(pl.multiple_of(i * rows_c, rows_c), rows_c)
    for c, w_ref in enumerate(w_refs):
        wbf_ref[slot, rows, c * tn:(c + 1) * tn] = w_ref[...].astype(BF16)

    @pl.when(j > 0)
    def _():
        gu = jnp.dot(x_ref[...], wbf_ref[1 - slot], preferred_element_type=F32)
        g = gu[:, :FFN_PAIRS * tn]
        o_ref[...] = (g * _sigmoid(g) * gu[:, FFN_PAIRS * tn:]).astype(o_ref.dtype)


def _ffn_up(x, w_gu, layer, *, n_m, tn):
    m, k = x.shape
    f = w_gu.shape[2] // 2
    tm, rows_c = m // n_m, k // n_m
    assert m % n_m == 0 and tm % 16 == 0 and k % n_m == 0 and rows_c % 16 == 0 and f % tn == 0
    n_pairs = f // tn
    nj = pl.cdiv(n_pairs, FFN_PAIRS)

    def w_map(c):
        g, part = c % FFN_PAIRS, c // FFN_PAIRS

        def index(j, i):
            pair = jnp.minimum(jnp.minimum(j, nj - 1) * FFN_PAIRS + g, n_pairs - 1)
            return layer, i, part * n_pairs + pair
        return index

    out_map = lambda j, i: (_first_rows(j, i), jnp.maximum(j - 1, 0))
    return pl.pallas_call(
        functools.partial(_ffn_up_kernel, tn=tn),
        grid=(nj + 1, n_m),
        in_specs=[pl.BlockSpec((tm, k), lambda j, i: (_first_rows(j, i), 0)),
                  *[pl.BlockSpec((None, rows_c, tn), w_map(c)) for c in range(2 * FFN_PAIRS)]],
        out_specs=pl.BlockSpec((tm, FFN_PAIRS * tn), out_map),
        out_shape=jax.ShapeDtypeStruct((m, f), BF16),
        scratch_shapes=[pltpu.VMEM((2, k, 2 * FFN_PAIRS * tn), BF16)],
        compiler_params=_cparams(("arbitrary", "arbitrary")),
        name="ffn_up",
    )(x, *([w_gu] * (2 * FFN_PAIRS)))


def _gates_kernel(x_ref, w_ref, alog_ref, dtb_ref, o_ref, *, n_heads, n_prompt_blocks, c_prompt, c_sample):
    tm = x_ref.shape[0]
    ba = lax.dot_general(w_ref[...], x_ref[...], (((1,), (1,)), ((), ())), preferred_element_type=F32)
    beta = _sigmoid(ba[:n_heads])
    a = ba[n_heads:] + dtb_ref[...]
    softplus = jnp.maximum(a, 0.0) + jnp.log(1.0 + jnp.exp(-jnp.abs(a)))
    g = -jnp.exp(alog_ref[...]) * softplus
    shift = jnp.where(pl.program_id(0) < n_prompt_blocks,
                      int(math.log2(c_prompt)), int(math.log2(c_sample)))
    jj = lax.broadcasted_iota(jnp.int32, (tm, tm), 0)
    ii = lax.broadcasted_iota(jnp.int32, (tm, tm), 1)
    same = lax.shift_right_logical(jj, shift) == lax.shift_right_logical(ii, shift)
    cum_m = jnp.where(same & (jj <= ii), 1.0, 0.0).astype(F32)
    tot_m = jnp.where(same, 1.0, 0.0).astype(F32)
    gcum = jnp.dot(g, cum_m, preferred_element_type=F32, precision=lax.Precision.HIGHEST)
    glast = jnp.dot(g, tot_m, preferred_element_type=F32, precision=lax.Precision.HIGHEST)
    o_ref[0] = beta
    o_ref[1] = jnp.exp(gcum)
    o_ref[2] = jnp.exp(glast - gcum)
    o_ref[3] = gcum


def _gates(xn, w_ba_t, a_log, dt_bias, *, tm, n_prompt_rows, c_prompt, c_sample):
    m, k = xn.shape
    h = a_log.shape[0]
    assert m % tm == 0 and n_prompt_rows % tm == 0 and tm % c_prompt == 0 and tm % c_sample == 0
    assert w_ba_t.shape == (2 * h, k)
    return pl.pallas_call(
        functools.partial(_gates_kernel, n_heads=h, n_prompt_blocks=n_prompt_rows // tm,
                          c_prompt=c_prompt, c_sample=c_sample),
        grid=(m // tm,),
        in_specs=[pl.BlockSpec((tm, k), lambda i: (i, 0)),
                  pl.BlockSpec((2 * h, k), lambda i: (0, 0)),
                  pl.BlockSpec((h, 1), lambda i: (0, 0)),
                  pl.BlockSpec((h, 1), lambda i: (0, 0))],
        out_specs=pl.BlockSpec((4, h, tm), lambda i: (0, 0, i)),
        out_shape=jax.ShapeDtypeStruct((4, h, m), F32),
        compiler_params=_cparams(("arbitrary",)),
        name="gates",
    )(xn, w_ba_t, a_log.reshape(h, 1), dt_bias.reshape(h, 1))


def _delta_kernel(q_ref, k_ref, v_ref, z_ref, cwq_ref, cwk_ref, cwv_ref, csq_ref, csk_ref, csv_ref,
                  gcol_ref, grow_ref, h0_ref, onw_ref,
                  o_ref, hout_ref, cqo_ref, cko_ref, cvo_ref,
                  h_scr, xq_scr, xk_scr, xv_scr, aq_scr, ak_scr, av_scr,
                  *, C, nb, nc, hb, dk, zero_init):
    t = pl.program_id(2)
    n_t = pl.num_programs(2)
    ttb = nc * C
    hist0 = CONV_PAD - (CONV_W - 1)

    @pl.when(t == 0)
    def _():
        if zero_init:
            h_scr[...] = jnp.zeros(h_scr.shape, F32)
        else:
            h_scr[...] = h0_ref[...]
        for xs, cs in ((xq_scr, csq_ref), (xk_scr, csk_ref), (xv_scr, csv_ref)):
            for bi in range(nb):
                xs[bi, hist0:CONV_PAD, :] = cs[bi]

    streams = ((q_ref, xq_scr, cwq_ref, aq_scr), (k_ref, xk_scr, cwk_ref, ak_scr), (v_ref, xv_scr, cwv_ref, av_scr))
    for raw_ref, xs, _, _ in streams:
        for bi in range(nb):
            xs[bi, CONV_PAD:CONV_PAD + ttb, :] = raw_ref[bi * ttb:(bi + 1) * ttb, :]

    def conv_phases(grp):
        r0, n = grp[0] * C, len(grp) * C

        def phase(xs, cw_ref, act):
            def run():
                for bi in range(nb):
                    acc = xs[bi, hist0 + r0:hist0 + r0 + n, :] * cw_ref[0:1, :]
                    for i in range(1, CONV_W):
                        acc = acc + xs[bi, hist0 + r0 + i:hist0 + r0 + i + n, :] * cw_ref[i:i + 1, :]
                    act[bi * ttb + r0:bi * ttb + r0 + n, :] = acc * _sigmoid(acc)
            return run
        return [phase(xs, cw_ref, act) for _, xs, cw_ref, act in streams]

    ii = lax.broadcasted_iota(jnp.int32, (C, C), 0)
    jj = lax.broadcasted_iota(jnp.int32, (C, C), 1)
    n_levels = int(math.log2(C))
    assert 2 ** n_levels == C

    def prepare(bi, c, j):
        rows = slice(bi * ttb + c * C, bi * ttb + (c + 1) * C)
        cols = slice(j * dk, (j + 1) * dk)
        q = aq_scr[rows, cols]
        k = ak_scr[rows, cols]
        qn = q * (lax.rsqrt(jnp.sum(q * q, axis=-1, keepdims=True) + EPS) * (dk ** -0.5))
        kn = k * lax.rsqrt(jnp.sum(k * k, axis=-1, keepdims=True) + EPS)
        gcb = gcol_ref[0, rows, :]
        beta = gcb[:, j:j + 1]
        eg = gcb[:, hb + j:hb + j + 1]
        ekl = gcb[:, 2 * hb + j:2 * hb + j + 1]
        gc = gcb[:, 3 * hb + j:3 * hb + j + 1]
        gr = grow_ref[0, bi * nc + c, j:j + 1, :]
        dmat = jnp.where(ii >= jj, jnp.exp(gc - gr), 0.0)
        rhs = jnp.concatenate([kn * (beta * eg), av_scr[rows, cols] * beta], axis=1)
        return dict(rows=rows, cols=cols, qn=qn, kn=kn, beta=beta, eg=eg, dmat=dmat, rhs=rhs,
                    qe=qn * eg, kd=kn * ekl)

    st = {}

    def solve_phases(group):
        def scores():
            for it in group:
                e = st[it] = prepare(*it)
                e["s"] = _bdot_nt(jnp.concatenate([e["kn"], e["qn"]], axis=0), e["kn"])

        def square():
            for it in group:
                e = st[it]
                s = e.pop("s")
                e["qk"] = s[C:] * e["dmat"]
                e["t"] = jnp.where(ii > jj, s[:C] * e["dmat"] * (-e["beta"]), 0.0)
                e["p"] = _bdot(e["t"], e["t"])

        def level(last):
            def run():
                for it in group:
                    e = st[it]
                    e["tp"] = _bdot(e["t"], e["p"])
                    if not last:
                        e["p2"] = _bdot(e["p"], e["p"])
                for it in group:
                    e = st[it]
                    e["t"] = e["t"] + e["p"] + e.pop("tp")
                    e["p"] = None if last else e.pop("p2")
            return run

        def apply_t():
            for it in group:
                e = st[it]
                e["wu"] = e["rhs"] + _bdot(e["t"], e["rhs"])

        return [scores, square, *[level(lvl == n_levels - 2) for lvl in range(n_levels - 1)], apply_t]

    h_cur = {(bi, j): h_scr[bi, j] for bi in range(nb) for j in range(hb)}

    def chain_phases(c):
        sel = [(bi, c, j) for bi in range(nb) for j in range(hb)]

        def through_state():
            for it in sel:
                e = st[it]
                e["wq"] = _bdot(jnp.concatenate([e["wu"][:, :dk], e["qe"]], axis=0), h_cur[it[0], it[2]])

        def update():
            for it in sel:
                e = st[it]
                e["u"] = e["wu"][:, dk:] - e["wq"][:C]
            for it in sel:
                e = st[it]
                e["o"] = e["wq"][C:] + _bdot(e["qk"], e["u"])
                egl = e["eg"][C - 1:C, :]
                h_cur[it[0], it[2]] = h_cur[it[0], it[2]] * egl + _bdot_tn(e["kd"], e["u"])

        def emit():
            for it in sel:
                e = st.pop(it)
                o = e["o"]
                on = o * lax.rsqrt(jnp.mean(o * o, axis=-1, keepdims=True) + EPS) * onw_ref[...]
                zz = z_ref[e["rows"], e["cols"]]
                o_ref[e["rows"], e["cols"]] = (on * (zz * _sigmoid(zz))).astype(o_ref.dtype)

        return [through_state, update, emit]

    chunk_groups = [list(range(c0, min(c0 + GROUP_CHUNKS, nc))) for c0 in range(0, nc, GROUP_CHUNKS)]
    n_grp = len(chunk_groups)
    convs = [conv_phases(grp) for grp in chunk_groups]
    solves = [solve_phases([(bi, c, j) for bi in range(nb) for c in grp for j in range(hb)])
              for grp in chunk_groups]
    chains = [[phase for c in grp for phase in chain_phases(c)] for grp in chunk_groups]
    for g in range(-2, n_grp):
        tracks = [convs[g + 2] if g + 2 < n_grp else [],
                  solves[g + 1] if 0 <= g + 1 < n_grp else [],
                  chains[g] if g >= 0 else []]
        for phases in itertools.zip_longest(*tracks):
            for phase in phases:
                if phase is not None:
                    phase()
    for (bi, j), hv in h_cur.items():
        h_scr[bi, j] = hv
    for _, xs, _, _ in streams:
        for bi in range(nb):
            xs[bi, hist0:CONV_PAD, :] = xs[bi, CONV_PAD + ttb - (CONV_W - 1):CONV_PAD + ttb, :]

    @pl.when(t == n_t - 1)
    def _():
        for (_, xs, _, _), co in zip(streams, (cqo_ref, cko_ref, cvo_ref)):
            for bi in range(nb):
                co[bi] = xs[bi, hist0:CONV_PAD, :]

    @pl.when(t == n_t - 1)
    def _():
        hout_ref[...] = h_scr[...]


def _delta_rule(proj, gates, conv_w, conv0, h0, o_norm_w, *, row0, batch, seq, n_heads, dk,
                C, nb, nc, hb, out_rows):
    ttb = nc * C
    tt = nb * ttb
    w = hb * dk
    ng = n_heads // hb
    n_t = seq // ttb
    assert seq % ttb == 0 and batch % nb == 0 and n_heads % hb == 0 and row0 % tt == 0
    zero_init = h0 is None
    d_qk = n_heads * dk
    cb = d_qk // w
    rb0 = row0 // tt

    rows = batch * seq
    g5 = gates.reshape(4, ng, hb, rows)
    gcol = jnp.transpose(g5, (1, 3, 0, 2)).reshape(ng, rows, 4 * hb)
    grow = jnp.transpose(g5[3].reshape(ng, hb, rows // C, C), (0, 2, 1, 3))

    def tok_map(part):
        return lambda b, g, t: (rb0 + b * n_t + t, part * cb + g)

    if zero_init:
        h0_arg = jnp.zeros((nb, hb, dk, dk), F32)
        h0_spec = pl.BlockSpec((nb, hb, dk, dk), lambda b, g, t: (0, 0, 0, 0))
    else:
        h0_arg = h0
        h0_spec = pl.BlockSpec((nb, hb, dk, dk), lambda b, g, t: (b, g, 0, 0))
    cs_spec = [pl.BlockSpec((nb, CONV_W - 1, w), (lambda b, g, t, p=p: (b, 0, p * cb + g))) for p in range(3)]
    cw_spec = [pl.BlockSpec((CONV_W, w), (lambda b, g, t, p=p: (0, p * cb + g))) for p in range(3)]
    kern = functools.partial(_delta_kernel, C=C, nb=nb, nc=nc, hb=hb, dk=dk, zero_init=zero_init)
    o, h_fin, cq, ck, cv = pl.pallas_call(
        kern,
        grid=(batch // nb, ng, n_t),
        in_specs=[pl.BlockSpec((tt, w), tok_map(0)), pl.BlockSpec((tt, w), tok_map(1)),
                  pl.BlockSpec((tt, w), tok_map(2)), pl.BlockSpec((tt, w), tok_map(3)),
                  *cw_spec, *cs_spec,
                  pl.BlockSpec((1, tt, 4 * hb), lambda b, g, t: (g, b * n_t + t, 0)),
                  pl.BlockSpec((1, nb * nc, hb, C), lambda b, g, t: (g, b * n_t + t, 0, 0)),
                  h0_spec,
                  pl.BlockSpec((1, dk), lambda b, g, t: (0, 0))],
        out_specs=[pl.BlockSpec((tt, w), lambda b, g, t: (rb0 + b * n_t + t, g)),
                   pl.BlockSpec((nb, hb, dk, dk), lambda b, g, t: (b, g, 0, 0)),
                   *[pl.BlockSpec((nb, CONV_W - 1, w), lambda b, g, t: (b, 0, g)) for _ in range(3)]],
        out_shape=[jax.ShapeDtypeStruct((out_rows, d_qk), BF16),
                   jax.ShapeDtypeStruct((batch, n_heads, dk, dk), F32),
                   *[jax.ShapeDtypeStruct((batch, CONV_W - 1, d_qk), F32) for _ in range(3)]],
        scratch_shapes=[pltpu.VMEM((nb, hb, dk, dk), F32),
                        *[pltpu.VMEM((nb, CONV_PAD + ttb, w), F32) for _ in range(3)],
                        *[pltpu.VMEM((tt, w), F32) for _ in range(3)]],
        compiler_params=_cparams(("arbitrary", "arbitrary", "arbitrary")),
        name="delta_rule",
    )(proj, proj, proj, proj, conv_w, conv_w, conv_w, conv0, conv0, conv0, gcol, grow, h0_arg,
      o_norm_w.reshape(1, dk))
    return o, h_fin, jnp.concatenate([cq, ck, cv], axis=-1)


def _group_mean_sq(x, gmat_ref, inv_n):
    x2 = x * x
    hi = x2.astype(BF16)
    lo = (x2 - hi.astype(F32)).astype(BF16)
    g = gmat_ref[...]
    return (jnp.dot(hi, g, preferred_element_type=F32) + jnp.dot(lo, g, preferred_element_type=F32)) * inv_n


def _rope(y, cos_ref, sa_ref, sb_ref, half):
    w = y.shape[1]
    reps = w // cos_ref.shape[1]
    tile = lambda r: jnp.concatenate([r[...]] * reps, axis=1)
    return (y * tile(cos_ref) + pltpu.roll(y, w - half, axis=1) * tile(sa_ref)
            + pltpu.roll(y, half, axis=1) * tile(sb_ref))


def _q_post_kernel(x_ref, w_ref, gmat_ref, cos_ref, sa_ref, sb_ref, o_ref, *, hd, half, scale):
    gw = gmat_ref.shape[0]
    for c in range(x_ref.shape[1] // gw):
        cols = slice(c * gw, (c + 1) * gw)
        x = x_ref[:, cols]
        y = x * lax.rsqrt(_group_mean_sq(x, gmat_ref, 1.0 / hd) + EPS) * w_ref[...]
        o_ref[:, cols] = (_rope(y, cos_ref, sa_ref, sb_ref, half) * scale).astype(o_ref.dtype)


def _kv_post_kernel(k_ref, v_ref, w_ref, gmat_ref, cos_ref, sa_ref, sb_ref,
                    kf_ref, kb_ref, vbt_ref, kwin_ref, vwin_ref, *, hd, half, n_prompt_blocks, blocks_per_seq):
    i = pl.program_id(0)
    x = k_ref[...]
    y = x * lax.rsqrt(_group_mean_sq(x, gmat_ref, 1.0 / hd) + EPS) * w_ref[...]
    kf = _rope(y, cos_ref, sa_ref, sb_ref, half)
    kf_ref[...] = kf
    v = v_ref[...]
    vt = v.T
    for hh in range(kb_ref.shape[0]):
        kb_ref[hh] = kf[:, hh * hd:(hh + 1) * hd].astype(BF16)
        vbt_ref[hh] = vt[hh * hd:(hh + 1) * hd, :].astype(BF16)

    @pl.when((i < n_prompt_blocks) & (i % blocks_per_seq == blocks_per_seq - 1))
    def _():
        wb = kwin_ref.shape[0]
        kwin_ref[...] = kf[kf.shape[0] - wb:, :]
        vwin_ref[...] = v[v.shape[0] - wb:, :]


def _rope_tables(pos, hd, rot_dim):
    half = rot_dim // 2
    inv_freq = jnp.power(ROPE_THETA, -jnp.arange(half, dtype=F32) * 2.0 / rot_dim)
    ang = pos[:, None] * inv_freq[None, :]
    cos, sin = jnp.cos(ang), jnp.sin(ang)
    m = pos.shape[0]
    ones = jnp.ones((m, hd - rot_dim), F32)
    zeros = jnp.zeros((m, hd - rot_dim), F32)
    zh = jnp.zeros((m, half), F32)
    c = jnp.concatenate([cos, cos, ones], axis=1)
    sa = jnp.concatenate([-sin, zh, zeros], axis=1)
    sb = jnp.concatenate([zh, sin, zeros], axis=1)
    reps = LANES // hd
    return tuple(jnp.tile(a, (1, reps)) for a in (c, sa, sb))


def _group_matrix(width, hd):
    r = jnp.arange(width) // hd
    return (r[:, None] == r[None, :]).astype(BF16)


def _softmax_sink(s, mask, sink):
    s = jnp.where(mask, s, -jnp.inf)
    m = jnp.maximum(jnp.max(s, axis=-1, keepdims=True), sink)
    p = jnp.exp(s - m)
    denom = jnp.sum(p, axis=-1, keepdims=True) + jnp.exp(sink - m)
    return (p / denom).astype(BF16)


def _attn_prompt_kernel(sink_ref, q_ref, kp_ref, kc_ref, vp_ref, vc_ref, o_ref, *, kv_heads, group, hd, win):
    nb = pl.program_id(1)
    kj = lax.broadcasted_iota(jnp.int32, (2 * win, win), 0)
    qi = lax.broadcasted_iota(jnp.int32, (2 * win, win), 1)
    mask = (kj > qi) & (kj <= qi + win) & ((kj >= win) | (nb > 0))

    def scores(h):
        kk = jnp.concatenate([kp_ref[h], kc_ref[h]], axis=0)
        return [lax.dot_general(kk, q_ref[:, (h * group + g) * hd:(h * group + g + 1) * hd],
                                (((1,), (1,)), ((), ())), preferred_element_type=F32) for g in range(group)]

    def row_max(h, ss):
        out = []
        for g, s in enumerate(ss):
            s = jnp.where(mask, s, -jnp.inf)
            out.append((s, jnp.maximum(jnp.max(s, axis=0, keepdims=True), sink_ref[h * group + g])))
        return out

    def probs(h, sm):
        out = []
        for g, (s, m) in enumerate(sm):
            p = jnp.exp(s - m)
            denom = jnp.sum(p, axis=0, keepdims=True) + jnp.exp(sink_ref[h * group + g] - m)
            out.append((p / denom).astype(BF16))
        return out

    def weighted_values(h, ps):
        vt = jnp.concatenate([vp_ref[h], vc_ref[h]], axis=1)
        ot = jnp.concatenate([jnp.dot(vt, p, preferred_element_type=F32) for p in ps], axis=0)
        o_ref[:, h * group * hd:(h + 1) * group * hd] = ot.T.astype(o_ref.dtype)

    stages = (scores, row_max, probs, weighted_values)
    carry = [None] * len(stages)
    for step in range(kv_heads + len(stages) - 1):
        for k in reversed(range(len(stages))):
            h = step - k
            if 0 <= h < kv_heads:
                carry[k] = stages[k](h) if k == 0 else stages[k](h, carry[k - 1])


def _attn_prompt(q, kb, vbt, sinks, *, batch, seq, kv_heads, group, hd, out_rows):
    win = WINDOW
    nblk = seq // win
    qw = kv_heads * group * hd
    k_spec = lambda back: pl.BlockSpec((kv_heads, win, hd),
                                       lambda b, n: (0, b * nblk + jnp.maximum(n - back, 0), 0))
    v_spec = lambda back: pl.BlockSpec((kv_heads, hd, win),
                                       lambda b, n: (0, 0, b * nblk + jnp.maximum(n - back, 0)))
    return pl.pallas_call(
        functools.partial(_attn_prompt_kernel, kv_heads=kv_heads, group=group, hd=hd, win=win),
        grid=(batch, nblk),
        in_specs=[pl.BlockSpec(memory_space=pltpu.SMEM),
                  pl.BlockSpec((win, qw), lambda b, n: (b * nblk + n, 0)),
                  k_spec(1), k_spec(0), v_spec(1), v_spec(0)],
        out_specs=pl.BlockSpec((win, qw), lambda b, n: (b * nblk + n, 0)),
        out_shape=jax.ShapeDtypeStruct((out_rows, qw), BF16),
        compiler_params=_cparams(("arbitrary", "arbitrary")),
        name="attn_prompt",
    )(sinks, q, kb, kb, vbt, vbt)


def _attn_sample_kernel(sink_ref, q_ref, kn_ref, vn_ref, kc_ref, vc_ref, prev_ref, o_ref,
                        *, nbs, tq, kv_heads, group, hd, win):
    del prev_ref
    nctx = 2 * win
    r = lax.broadcasted_iota(jnp.int32, (group * tq, nctx), 0)
    kj = lax.broadcasted_iota(jnp.int32, (group * tq, nctx), 1)
    tpos = r % tq
    mask = ((kj < win) & (kj > tpos)) | ((kj >= win) & (kj - win <= tpos))
    pad = jnp.zeros((nctx - win - tq, hd), F32)
    items = [(bi, h) for bi in range(nbs) for h in range(kv_heads)]
    qbs = [q_ref[bi * tq:(bi + 1) * tq, :].astype(F32) for bi in range(nbs)]
    sinks = [jnp.concatenate([jnp.full((tq, 1), sink_ref[h * group + g], F32) for g in range(group)], axis=0)
             for h in range(kv_heads)]
    ss, vs = [], []
    for bi, h in items:
        cs = slice(h * hd, (h + 1) * hd)
        trows = slice(bi * tq, (bi + 1) * tq)
        kctx = jnp.concatenate([kc_ref[bi][:, cs], kn_ref[trows, cs], pad], axis=0)
        vs.append(jnp.concatenate([vc_ref[bi][:, cs], vn_ref[trows, cs], pad], axis=0).astype(BF16))
        qg = jnp.concatenate([qbs[bi][:, (h * group + g) * hd:(h * group + g + 1) * hd]
                              for g in range(group)], axis=0)
        ss.append(_bdot_nt(qg, kctx))
    ps = [_softmax_sink(s, mask, sinks[h]) for (bi, h), s in zip(items, ss)]
    for (bi, h), p, v in zip(items, ps, vs):
        o = jnp.dot(p, v, preferred_element_type=F32)
        for g in range(group):
            hh = h * group + g
            o_ref[bi * tq:(bi + 1) * tq, hh * hd:(hh + 1) * hd] = o[g * tq:(g + 1) * tq, :].astype(o_ref.dtype)


def _attn_sample(q, kf, kv, cache_k, cache_v, sinks, o_prev, *, row0, batch, tq, kv_heads, group, hd, nbs):
    win = cache_k.shape[1]
    assert win == WINDOW and batch % nbs == 0 and row0 % (nbs * tq) == 0
    rows = nbs * tq
    rb0 = row0 // rows
    kvw = kv_heads * hd
    return pl.pallas_call(
        functools.partial(_attn_sample_kernel, nbs=nbs, tq=tq, kv_heads=kv_heads, group=group, hd=hd, win=win),
        grid=(batch // nbs,),
        in_specs=[pl.BlockSpec(memory_space=pltpu.SMEM),
                  pl.BlockSpec((rows, q.shape[1]), lambda b: (rb0 + b, 0)),
                  pl.BlockSpec((rows, kvw), lambda b: (rb0 + b, 0)),
                  pl.BlockSpec((rows, kvw), lambda b: (rb0 + b, 1)),
                  pl.BlockSpec((nbs, win, kvw), lambda b: (b, 0, 0)),
                  pl.BlockSpec((nbs, win, kvw), lambda b: (b, 0, 0)),
                  pl.BlockSpec(memory_space=pl.ANY)],
        out_specs=pl.BlockSpec((rows, q.shape[1]), lambda b: (rb0 + b, 0)),
        out_shape=jax.ShapeDtypeStruct(o_prev.shape, o_prev.dtype),
        input_output_aliases={6: 0},
        compiler_params=_cparams(("arbitrary",)),
        name="attn_sample",
    )(sinks, q, kf, kv, cache_k, cache_v, o_prev)


PAST_LEN = 16384
TM_NORM = 256
M_STEPS = 8


Q_POST_COLS = 2048
TN_WIDE = 1024
TN_RES = 512
TN_FFN = 256
DOWN_KBLOCKS = 2


def _write_rows_kernel(src_ref, dst_ref, o_ref):
    del dst_ref
    o_ref[...] = src_ref[...]


def _merge_rows(dst, src, row0, tm):
    n = src.shape[1]
    rows = src.shape[0] - row0
    assert rows % tm == 0 and row0 % tm == 0
    return pl.pallas_call(
        _write_rows_kernel,
        grid=(rows // tm,),
        in_specs=[pl.BlockSpec((tm, n), lambda i: (row0 // tm + i, 0)),
                  pl.BlockSpec(memory_space=pl.ANY)],
        out_specs=pl.BlockSpec((tm, n), lambda i: (row0 // tm + i, 0)),
        out_shape=jax.ShapeDtypeStruct(dst.shape, dst.dtype),
        input_output_aliases={1: 0},
        compiler_params=_cparams(("arbitrary",)),
        name="merge_rows",
    )(src, dst)


def kernel(x_prompt, x_sample, state_ssm, state_conv, cache_win_k, cache_win_v, norm_mix, norm_ffn,
           a_w_in, a_conv_w, a_log, a_dt_bias, a_o_norm, a_w_out, kv_norm, w_kv, k_norm, b_w_q,
           b_q_norm, b_sinks, b_w_o, ffn_w_gu, ffn_w_down):
    bp, sp, d = x_prompt.shape
    bs, ss, _ = x_sample.shape
    mp, ms = bp * sp, bs * ss
    m = mp + ms
    n_a = a_w_in.shape[0]
    assert n_a == 1 and b_w_q.shape[0] == 1, "layer pattern is one delta layer then one attention layer"
    h_a = a_log.shape[1]
    dk = a_o_norm.shape[1]
    d_qk = h_a * dk
    kv_heads, hd = cache_win_k.shape[2], cache_win_k.shape[3]
    kvw = kv_heads * hd
    h_b = b_sinks.shape[1]
    group = h_b // kv_heads
    rot_dim = hd // 4

    def mm(xb, w, n_out, res=None):
        tn = min(TN_WIDE if res is None else TN_RES, n_out)
        return _mm_stream(xb, w, 0, n_out=n_out, tn=tn, n_m=M_STEPS, res=res)

    def ffn(hin, hn, layer, split_streams=False):
        act = _ffn_up(hn, ffn_w_gu, layer, n_m=M_STEPS, tn=TN_FFN)
        out = hin
        for kb in range(DOWN_KBLOCKS):
            tail = ms if split_streams and kb == DOWN_KBLOCKS - 1 else 0
            out = _mm_stream(act, ffn_w_down, layer, n_out=d, tn=min(TN_RES, d), n_m=M_STEPS,
                             kblock=kb, n_kblocks=DOWN_KBLOCKS, res=out, tail=tail)
        return out

    xp2, xs2 = x_prompt.reshape(mp, d), x_sample.reshape(ms, d)
    (xn,) = _norm_inputs(xp2, xs2, norm_mix[0], TM_NORM)
    w_in_t = jnp.swapaxes(a_w_in, 1, 2)
    proj = _mm_stream(xn, w_in_t, 0, n_out=4 * d_qk, tn=TN_WIDE, n_m=M_STEPS, w_t=True)
    w_ba_t = w_in_t[0, 4 * d_qk:, :].astype(BF16)
    c_s = math.gcd(ss, DELTA_CHUNK)
    gates = _gates(xn, w_ba_t, a_log[0], a_dt_bias[0], tm=TM_NORM, n_prompt_rows=mp,
                   c_prompt=DELTA_CHUNK, c_sample=c_s)
    conv_w = a_conv_w[0]
    conv0_p = jnp.zeros((bp, CONV_W - 1, 3 * d_qk), F32)
    o_p, ssm_p, conv_p = _delta_rule(proj, gates[:, :, :mp], conv_w, conv0_p, None, a_o_norm[0],
                                     row0=0, batch=bp, seq=sp, n_heads=h_a, dk=dk,
                                     C=DELTA_CHUNK, nb=1, nc=4, hb=8, out_rows=m)
    o_s, ssm_s, conv_s = _delta_rule(proj, gates[:, :, mp:], conv_w, state_conv[0], state_ssm[0], a_o_norm[0],
                                     row0=mp, batch=bs, seq=ss, n_heads=h_a, dk=dk,
                                     C=c_s, nb=4, nc=1, hb=8, out_rows=m)
    o_a = _merge_rows(o_p, o_s, mp, TM_NORM)
    hn1, h1 = _norm_inputs(xp2, xs2, norm_ffn[0], TM_NORM, y=mm(o_a, a_w_out, d))
    h2 = ffn(h1, hn1, 0)

    hkv, hq = _norm(h2, jnp.stack([kv_norm, norm_mix[1]]), TM_NORM)
    kv = mm(hkv, w_kv[None], 2 * kvw)
    pos = jnp.concatenate([jnp.tile(jnp.arange(sp, dtype=F32), bp),
                           jnp.tile(PAST_LEN + jnp.arange(ss, dtype=F32), bs)])
    cos_t, sa_t, sb_t = _rope_tables(pos, hd, rot_dim)
    gmat = _group_matrix(kvw, hd)
    wb = min(WINDOW, sp)
    bps = sp // TM_NORM
    assert sp % TM_NORM == 0 and wb <= TM_NORM
    win_spec = pl.BlockSpec((wb, kvw), lambda i: (jnp.minimum(i // bps, bp - 1), 0))
    kf, kb, vbt, kwin_p, vwin_p = pl.pallas_call(
        functools.partial(_kv_post_kernel, hd=hd, half=rot_dim // 2, n_prompt_blocks=mp // TM_NORM,
                          blocks_per_seq=bps),
        grid=(m // TM_NORM,),
        in_specs=[pl.BlockSpec((TM_NORM, kvw), lambda i: (i, 0)),
                  pl.BlockSpec((TM_NORM, kvw), lambda i: (i, 1)),
                  pl.BlockSpec((1, kvw), lambda i: (0, 0)),
                  pl.BlockSpec((kvw, kvw), lambda i: (0, 0)),
                  *[pl.BlockSpec((TM_NORM, LANES), lambda i: (i, 0)) for _ in range(3)]],
        out_specs=[pl.BlockSpec((TM_NORM, kvw), lambda i: (i, 0)),
                   pl.BlockSpec((kv_heads, TM_NORM, hd), lambda i: (0, i, 0)),
                   pl.BlockSpec((kv_heads, hd, TM_NORM), lambda i: (0, 0, i)),
                   win_spec, win_spec],
        out_shape=[jax.ShapeDtypeStruct((m, kvw), F32),
                   jax.ShapeDtypeStruct((kv_heads, m, hd), BF16),
                   jax.ShapeDtypeStruct((kv_heads, hd, m), BF16),
                   jax.ShapeDtypeStruct((bp * wb, kvw), F32),
                   jax.ShapeDtypeStruct((bp * wb, kvw), F32)],
        compiler_params=_cparams(("arbitrary",)),
        name="kv_post",
    )(kv, kv, jnp.tile(k_norm, kv_heads).reshape(1, kvw), gmat, cos_t, sa_t, sb_t)

    qraw = mm(hq, b_w_q, h_b * hd)
    qw = kvw
    qcols = h_b * hd
    qblk = min(Q_POST_COLS, qcols)
    assert qcols % qblk == 0 and qblk % qw == 0
    q = pl.pallas_call(
        functools.partial(_q_post_kernel, hd=hd, half=rot_dim // 2, scale=hd ** -0.5),
        grid=(m // TM_NORM, qcols // qblk),
        in_specs=[pl.BlockSpec((TM_NORM, qblk), lambda i, j: (i, j)),
                  pl.BlockSpec((1, qw), lambda i, j: (0, 0)),
                  pl.BlockSpec((qw, qw), lambda i, j: (0, 0)),
                  *[pl.BlockSpec((TM_NORM, LANES), lambda i, j: (i, 0)) for _ in range(3)]],
        out_specs=pl.BlockSpec((TM_NORM, qblk), lambda i, j: (i, j)),
        out_shape=jax.ShapeDtypeStruct((m, qcols), BF16),
        compiler_params=_cparams(("arbitrary", "arbitrary")),
        name="q_post",
    )(qraw, jnp.tile(b_q_norm[0], qw // hd).reshape(1, qw), gmat, cos_t, sa_t, sb_t)
    sinks = b_sinks[0]
    o_b = _attn_prompt(q, kb, vbt, sinks, batch=bp, seq=sp, kv_heads=kv_heads, group=group, hd=hd, out_rows=m)
    o_b = _attn_sample(q, kf, kv, cache_win_k.reshape(bs, WINDOW, kvw), cache_win_v.reshape(bs, WINDOW, kvw),
                       sinks, o_b, row0=mp, batch=bs, tq=ss, kv_heads=kv_heads, group=group, hd=hd, nbs=2)
    h3 = mm(o_b, b_w_o, d, res=h2)
    (hn3,) = _norm(h3, norm_ffn[1:2], TM_NORM)
    y_p, y_s = ffn(h3, hn3, 1, split_streams=True)

    kf_p = kwin_p.reshape(bp, wb, kv_heads, hd)
    vf_p = vwin_p.reshape(bp, wb, kv_heads, hd)
    kf_s = kf[mp:].reshape(bs, ss, kv_heads, hd)
    vf_s = kv[mp:, kvw:].reshape(bs, ss, kv_heads, hd)
    wk_s = jnp.concatenate([cache_win_k[:, ss:], kf_s], axis=1)
    wv_s = jnp.concatenate([cache_win_v[:, ss:], vf_s], axis=1)
    return (y_p.reshape(bp, sp, d), y_s.reshape(bs, ss, d),
            ssm_p[None], conv_p[None], kf_p, vf_p, ssm_s[None], conv_s[None], wk_s, wv_s)
```

```python
import functools
import itertools
import math

import jax
import jax.numpy as jnp
from jax import lax
from jax.experimental import pallas as pl
from jax.experimental.pallas import tpu as pltpu

F32 = jnp.float32
BF16 = jnp.bfloat16

EPS = 1e-6
LOG2E = math.log2(math.e)
WINDOW = 128
ROPE_THETA = 500000.0
CONV_W = 4
DELTA_CHUNK = 64
LANES = 128
CONV_PAD = 8
GROUP_CHUNKS = 2
VMEM_LIMIT = 56 * 1024 * 1024


def _cparams(sem):
    return pltpu.CompilerParams(dimension_semantics=sem, vmem_limit_bytes=VMEM_LIMIT)


def _sigmoid(x):
    return 1.0 / (1.0 + jnp.exp(-x))


def _bdot(a, b):
    return jnp.dot(a.astype(BF16), b.astype(BF16), preferred_element_type=F32)


def _bdot_nt(a, b):
    return lax.dot_general(a.astype(BF16), b.astype(BF16), (((1,), (1,)), ((), ())),
                           preferred_element_type=F32)


def _bdot_tn(a, b):
    return lax.dot_general(a.astype(BF16), b.astype(BF16), (((0,), (0,)), ((), ())),
                           preferred_element_type=F32)


def _norm_inputs_kernel(xp_ref, xs_ref, w_ref, *refs, n_prompt_blocks, has_y):
    i = pl.program_id(0)

    def emit(x):
        if has_y:
            y_ref, xn_ref, h_ref = refs
            x = x + y_ref[...]
            h_ref[...] = x
        else:
            (xn_ref,) = refs
        r = lax.rsqrt(jnp.mean(x * x, axis=-1, keepdims=True) + EPS)
        xn_ref[...] = (x * r * w_ref[...]).astype(xn_ref.dtype)

    @pl.when(i < n_prompt_blocks)
    def _():
        emit(xp_ref[...])

    @pl.when(i >= n_prompt_blocks)
    def _():
        emit(xs_ref[...])


def _norm_inputs(xp, xs, w, tm, y=None):
    mp, d = xp.shape
    ms = xs.shape[0]
    assert mp % tm == 0 and ms == tm
    npb = mp // tm
    m = mp + ms
    row = pl.BlockSpec((tm, d), lambda i: (i, 0))
    return pl.pallas_call(
        functools.partial(_norm_inputs_kernel, n_prompt_blocks=npb, has_y=y is not None),
        grid=(npb + 1,),
        in_specs=[pl.BlockSpec((tm, d), lambda i: (jnp.minimum(i, npb - 1), 0)),
                  pl.BlockSpec((tm, d), lambda i: (0, 0)),
                  pl.BlockSpec((1, d), lambda i: (0, 0))] + ([row] if y is not None else []),
        out_specs=[row] + ([row] if y is not None else []),
        out_shape=[jax.ShapeDtypeStruct((m, d), BF16)] + ([jax.ShapeDtypeStruct((m, d), F32)] if y is not None else []),
        compiler_params=_cparams(("arbitrary",)),
        name="norm_inputs",
    )(xp, xs, w.reshape(1, d), *([y] if y is not None else []))


def _norm_kernel(x_ref, w_ref, *o_refs):
    x = x_ref[...]
    y = x * lax.rsqrt(jnp.mean(x * x, axis=-1, keepdims=True) + EPS)
    for i, o_ref in enumerate(o_refs):
        o_ref[...] = (y * w_ref[i:i + 1, :]).astype(o_ref.dtype)


def _norm(x, ws, tm):
    m, d = x.shape
    nw = ws.shape[0]
    assert m % tm == 0
    return pl.pallas_call(
        _norm_kernel,
        grid=(m // tm,),
        in_specs=[pl.BlockSpec((tm, d), lambda i: (i, 0)),
                  pl.BlockSpec((nw, d), lambda i: (0, 0))],
        out_specs=[pl.BlockSpec((tm, d), lambda i: (i, 0)) for _ in range(nw)],
        out_shape=[jax.ShapeDtypeStruct((m, d), BF16) for _ in range(nw)],
        compiler_params=_cparams(("arbitrary",)),
        name="norm",
    )(x, ws)


def _first_rows(j, i):
    return jnp.where(j == 0, 0, i)


def _mm_stream_kernel(*refs, has_res, w_t, n_m, tail):
    x_ref, w_ref, *mid, wbf_ref = refs
    res_ref = mid[0] if has_res else None
    out_refs = mid[1:] if has_res else mid
    j = pl.program_id(0)
    i = pl.program_id(1)
    rows_c = w_ref.shape[0]
    slot = j % 2
    ic = jnp.minimum(i, n_m - 1)
    wbf_ref[slot, pl.ds(pl.multiple_of(ic * rows_c, rows_c), rows_c), :] = w_ref[...].astype(BF16)

    def product(rows):
        contract = (((1,), (1 if w_t else 0,)), ((), ()))
        acc = lax.dot_general(x_ref[:rows, :], wbf_ref[1 - slot], contract, preferred_element_type=F32)
        if has_res:
            acc = acc + res_ref[:rows, :]
        return acc

    @pl.when((j > 0) & (i < n_m))
    def _():
        out_refs[0][...] = product(x_ref.shape[0]).astype(out_refs[0].dtype)

    if tail:
        @pl.when((j > 0) & (i == n_m))
        def _():
            out_refs[1][...] = product(tail).astype(out_refs[1].dtype)


def _mm_stream(x, w, layer, *, n_out, tn, n_m, kblock=0, n_kblocks=1, res=None, out_dtype=F32, w_t=False,
               tail=0):
    m = x.shape[0]
    m_main = m - tail
    k = x.shape[1] // n_kblocks
    tm, nj = m_main // n_m, n_out // tn
    rows_c = (tn if w_t else k) // n_m
    assert m_main % n_m == 0 and tm % 16 == 0 and rows_c * n_m == (tn if w_t else k) and rows_c % 16 == 0
    assert n_out % tn == 0 and w.shape[2 if w_t else 1] == k * n_kblocks == x.shape[1] and k % LANES == 0
    assert 0 <= tail <= tm and tail % 8 == 0
    chunk = lambda i: jnp.minimum(i, n_m - 1)
    if w_t:
        w_spec = pl.BlockSpec((None, rows_c, k),
                              lambda j, i: (layer, jnp.minimum(j, nj - 1) * n_m + chunk(i), kblock))
    else:
        w_spec = pl.BlockSpec((None, rows_c, tn),
                              lambda j, i: (layer, kblock * n_m + chunk(i), jnp.minimum(j, nj - 1)))
    in_specs = [pl.BlockSpec((tm, k), lambda j, i: (_first_rows(j, i), kblock)), w_spec]
    args = [x, w]
    col = lambda j: jnp.maximum(j - 1, 0)
    if res is not None:
        in_specs.append(pl.BlockSpec((tm, tn), lambda j, i: (_first_rows(j, i), col(j))))
        args.append(res)
    out_specs = [pl.BlockSpec((tm, tn), lambda j, i: (jnp.minimum(_first_rows(j, i), n_m - 1), col(j)))]
    out_shape = [jax.ShapeDtypeStruct((m_main, n_out), out_dtype)]
    if tail:
        out_specs.append(pl.BlockSpec((tail, tn), lambda j, i: (0, col(j))))
        out_shape.append(jax.ShapeDtypeStruct((tail, n_out), out_dtype))
    outs = pl.pallas_call(
        functools.partial(_mm_stream_kernel, has_res=res is not None, w_t=w_t, n_m=n_m, tail=tail),
        grid=(nj + 1, n_m + (1 if tail else 0)),
        in_specs=in_specs,
        out_specs=out_specs,
        out_shape=out_shape,
        scratch_shapes=[pltpu.VMEM((2, tn, k) if w_t else (2, k, tn), BF16)],
        compiler_params=_cparams(("arbitrary", "arbitrary")),
        name="mm_stream",
    )(*args)
    return tuple(outs) if tail else outs[0]


FFN_PAIRS = 2


def _ffn_up_kernel(x_ref, *refs, tn):
    w_refs, (o_ref, wbf_ref) = refs[:2 * FFN_PAIRS], refs[2 * FFN_PAIRS:]
    j = pl.program_id(0)
    i = pl.program_id(1)
    rows_c = w_refs[0].shape[0]
    slot = j % 2
    rows = pl.ds(pl.multiple_of(i * rows_c, rows_c), rows_c)
    for c, w_ref in enumerate(w_refs):
        wbf_ref[slot, rows, c * tn:(c + 1) * tn] = w_ref[...].astype(BF16)

    @pl.when(j > 0)
    def _():
        gu = jnp.dot(x_ref[...], wbf_ref[1 - slot], preferred_element_type=F32)
        g = gu[:, :FFN_PAIRS * tn]
        o_ref[...] = (g * _sigmoid(g) * gu[:, FFN_PAIRS * tn:]).astype(o_ref.dtype)


def _ffn_up(x, w_gu, layer, *, n_m, tn):
    m, k = x.shape
    f = w_gu.shape[2] // 2
    tm, rows_c = m // n_m, k // n_m
    assert m % n_m == 0 and tm % 16 == 0 and k % n_m == 0 and rows_c % 16 == 0 and f % tn == 0
    n_pairs = f // tn
    nj = pl.cdiv(n_pairs, FFN_PAIRS)

    def w_map(c):
        g, part = c % FFN_PAIRS, c // FFN_PAIRS

        def index(j, i):
            pair = jnp.minimum(jnp.minimum(j, nj - 1) * FFN_PAIRS + g, n_pairs - 1)
            return layer, i, part * n_pairs + pair
        return index

    out_map = lambda j, i: (_first_rows(j, i), jnp.maximum(j - 1, 0))
    return pl.pallas_call(
        functools.partial(_ffn_up_kernel, tn=tn),
        grid=(nj + 1, n_m),
        in_specs=[pl.BlockSpec((tm, k), lambda j, i: (_first_rows(j, i), 0)),
                  *[pl.BlockSpec((None, rows_c, tn), w_map(c)) for c in range(2 * FFN_PAIRS)]],
        out_specs=pl.BlockSpec((tm, FFN_PAIRS * tn), out_map),
        out_shape=jax.ShapeDtypeStruct((m, f), BF16),
        scratch_shapes=[pltpu.VMEM((2, k, 2 * FFN_PAIRS * tn), BF16)],
        compiler_params=_cparams(("arbitrary", "arbitrary")),
        name="ffn_up",
    )(x, *([w_gu] * (2 * FFN_PAIRS)))


def _gates_kernel(x_ref, w_ref, alog_ref, dtb_ref, o_ref, *, n_heads, n_prompt_blocks, c_prompt, c_sample):
    tm = x_ref.shape[0]
    ba = lax.dot_general(w_ref[...], x_ref[...], (((1,), (1,)), ((), ())), preferred_element_type=F32)
    beta = _sigmoid(ba[:n_heads])
    a = ba[n_heads:] + dtb_ref[...]
    softplus = jnp.maximum(a, 0.0) + jnp.log(1.0 + jnp.exp(-jnp.abs(a)))
    g = -jnp.exp(alog_ref[...]) * softplus
    shift = jnp.where(pl.program_id(0) < n_prompt_blocks,
                      int(math.log2(c_prompt)), int(math.log2(c_sample)))
    jj = lax.broadcasted_iota(jnp.int32, (tm, tm), 0)
    ii = lax.broadcasted_iota(jnp.int32, (tm, tm), 1)
    same = lax.shift_right_logical(jj, shift) == lax.shift_right_logical(ii, shift)
    cum_m = jnp.where(same & (jj <= ii), 1.0, 0.0).astype(F32)
    tot_m = jnp.where(same, 1.0, 0.0).astype(F32)
    gcum = jnp.dot(g, cum_m, preferred_element_type=F32, precision=lax.Precision.HIGHEST)
    glast = jnp.dot(g, tot_m, preferred_element_type=F32, precision=lax.Precision.HIGHEST)
    o_ref[0] = beta
    o_ref[1] = jnp.exp(gcum)
    o_ref[2] = jnp.exp(glast - gcum)
    o_ref[3] = gcum


def _gates(xn, w_ba_t, a_log, dt_bias, *, tm, n_prompt_rows, c_prompt, c_sample):
    m, k = xn.shape
    h = a_log.shape[0]
    assert m % tm == 0 and n_prompt_rows % tm == 0 and tm % c_prompt == 0 and tm % c_sample == 0
    assert w_ba_t.shape == (2 * h, k)
    return pl.pallas_call(
        functools.partial(_gates_kernel, n_heads=h, n_prompt_blocks=n_prompt_rows // tm,
                          c_prompt=c_prompt, c_sample=c_sample),
        grid=(m // tm,),
        in_specs=[pl.BlockSpec((tm, k), lambda i: (i, 0)),
                  pl.BlockSpec((2 * h, k), lambda i: (0, 0)),
                  pl.BlockSpec((h, 1), lambda i: (0, 0)),
                  pl.BlockSpec((h, 1), lambda i: (0, 0))],
        out_specs=pl.BlockSpec((4, h, tm), lambda i: (0, 0, i)),
        out_shape=jax.ShapeDtypeStruct((4, h, m), F32),
        compiler_params=_cparams(("arbitrary",)),
        name="gates",
    )(xn, w_ba_t, a_log.reshape(h, 1), dt_bias.reshape(h, 1))


def _delta_kernel(q_ref, k_ref, v_ref, z_ref, cwq_ref, cwk_ref, cwv_ref, csq_ref, csk_ref, csv_ref,
                  gcol_ref, grow_ref, h0_ref, onw_ref,
                  o_ref, hout_ref, cqo_ref, cko_ref, cvo_ref,
                  h_scr, xq_scr, xk_scr, xv_scr, aq_scr, ak_scr, av_scr,
                  *, C, nb, nc, hb, dk, zero_init):
    t = pl.program_id(2)
    n_t = pl.num_programs(2)
    ttb = nc * C
    hist0 = CONV_PAD - (CONV_W - 1)

    @pl.when(t == 0)
    def _():
        if zero_init:
            h_scr[...] = jnp.zeros(h_scr.shape, F32)
        else:
            h_scr[...] = h0_ref[...]
        for xs, cs in ((xq_scr, csq_ref), (xk_scr, csk_ref), (xv_scr, csv_ref)):
            for bi in range(nb):
                xs[bi, hist0:CONV_PAD, :] = cs[bi]

    streams = ((q_ref, xq_scr, cwq_ref, aq_scr), (k_ref, xk_scr, cwk_ref, ak_scr), (v_ref, xv_scr, cwv_ref, av_scr))
    for raw_ref, xs, _, _ in streams:
        for bi in range(nb):
            xs[bi, CONV_PAD:CONV_PAD + ttb, :] = raw_ref[bi * ttb:(bi + 1) * ttb, :]

    def conv_phases(grp):
        r0, n = grp[0] * C, len(grp) * C

        def phase(xs, cw_ref, act):
            def run():
                for bi in range(nb):
                    win = xs[bi, r0:r0 + CONV_PAD + n, :]
                    acc = win[CONV_PAD:] * cw_ref[CONV_W - 1:CONV_W, :]
                    for s in range(1, CONV_W):
                        acc = acc + pltpu.roll(win, s, axis=0)[CONV_PAD:] * cw_ref[CONV_W - 1 - s:CONV_W - s, :]
                    act[bi * ttb + r0:bi * ttb + r0 + n, :] = acc * _sigmoid(acc)
            return run
        return [phase(xs, cw_ref, act) for _, xs, cw_ref, act in streams]

    ii = lax.broadcasted_iota(jnp.int32, (C, C), 0)
    jj = lax.broadcasted_iota(jnp.int32, (C, C), 1)
    n_levels = int(math.log2(C))
    assert 2 ** n_levels == C

    def prepare(bi, c, j):
        rows = slice(bi * ttb + c * C, bi * ttb + (c + 1) * C)
        cols = slice(j * dk, (j + 1) * dk)
        q = aq_scr[rows, cols]
        k = ak_scr[rows, cols]
        qn = q * (lax.rsqrt(jnp.sum(q * q, axis=-1, keepdims=True) + EPS) * (dk ** -0.5))
        kn = k * lax.rsqrt(jnp.sum(k * k, axis=-1, keepdims=True) + EPS)
        gcb = gcol_ref[0, rows, :]
        beta = gcb[:, j:j + 1]
        eg = gcb[:, hb + j:hb + j + 1]
        ekl = gcb[:, 2 * hb + j:2 * hb + j + 1]
        gc = gcb[:, 3 * hb + j:3 * hb + j + 1]
        gr = grow_ref[0, bi * nc + c, j:j + 1, :]
        dmat = jnp.where(ii >= jj, jnp.exp(gc - gr), 0.0)
        rhs = jnp.concatenate([kn * (beta * eg), av_scr[rows, cols] * beta], axis=1)
        return dict(rows=rows, cols=cols, qn=qn, kn=kn, beta=beta, eg=eg, dmat=dmat, rhs=rhs,
                    qe=qn * eg, kd=kn * ekl)

    st = {}

    def solve_phases(group):
        def scores():
            for it in group:
                e = st[it] = prepare(*it)
                e["s"] = _bdot_nt(jnp.concatenate([e["kn"], e["qn"]], axis=0), e["kn"])

        def square():
            for it in group:
                e = st[it]
                s = e.pop("s")
                e["qk"] = s[C:] * e["dmat"]
                e["t"] = jnp.where(ii > jj, s[:C] * e["dmat"] * (-e["beta"]), 0.0)
                e["p"] = _bdot(e["t"], e["t"])

        def level(last):
            def run():
                for it in group:
                    e = st[it]
                    e["tp"] = _bdot(e["t"], e["p"])
                    if not last:
                        e["p2"] = _bdot(e["p"], e["p"])
                for it in group:
                    e = st[it]
                    e["t"] = e["t"] + e["p"] + e.pop("tp")
                    e["p"] = None if last else e.pop("p2")
            return run

        def apply_t():
            for it in group:
                e = st[it]
                e["wu"] = e["rhs"] + _bdot(e["t"], e["rhs"])

        return [scores, square, *[level(lvl == n_levels - 2) for lvl in range(n_levels - 1)], apply_t]

    h_cur = {(bi, j): h_scr[bi, j] for bi in range(nb) for j in range(hb)}

    def chain_phases(c):
        sel = [(bi, c, j) for bi in range(nb) for j in range(hb)]

        def through_state():
            for it in sel:
                e = st[it]
                e["wq"] = _bdot(jnp.concatenate([e["wu"][:, :dk], e["qe"]], axis=0), h_cur[it[0], it[2]])

        def update():
            for it in sel:
                e = st[it]
                e["u"] = e["wu"][:, dk:] - e["wq"][:C]
            for it in sel:
                e = st[it]
                e["o"] = e["wq"][C:] + _bdot(e["qk"], e["u"])
                egl = e["eg"][C - 1:C, :]
                h_cur[it[0], it[2]] = h_cur[it[0], it[2]] * egl + _bdot_tn(e["kd"], e["u"])

        def emit():
            for it in sel:
                e = st.pop(it)
                o = e["o"]
                on = o * lax.rsqrt(jnp.mean(o * o, axis=-1, keepdims=True) + EPS) * onw_ref[...]
                zz = z_ref[e["rows"], e["cols"]]
                o_ref[e["rows"], e["cols"]] = (on * (zz * _sigmoid(zz))).astype(o_ref.dtype)

        return [through_state, update, emit]

    chunk_groups = [list(range(c0, min(c0 + GROUP_CHUNKS, nc))) for c0 in range(0, nc, GROUP_CHUNKS)]
    n_grp = len(chunk_groups)
    convs = [conv_phases(grp) for grp in chunk_groups]
    solves = [solve_phases([(bi, c, j) for bi in range(nb) for c in grp for j in range(hb)])
              for grp in chunk_groups]
    chains = [[phase for c in grp for phase in chain_phases(c)] for grp in chunk_groups]
    for g in range(-2, n_grp):
        tracks = [convs[g + 2] if g + 2 < n_grp else [],
                  solves[g + 1] if 0 <= g + 1 < n_grp else [],
                  chains[g] if g >= 0 else []]
        for phases in itertools.zip_longest(*tracks):
            for phase in phases:
                if phase is not None:
                    phase()
    for (bi, j), hv in h_cur.items():
        h_scr[bi, j] = hv
    for _, xs, _, _ in streams:
        for bi in range(nb):
            xs[bi, hist0:CONV_PAD, :] = xs[bi, CONV_PAD + ttb - (CONV_W - 1):CONV_PAD + ttb, :]

    @pl.when(t == n_t - 1)
    def _():
        for (_, xs, _, _), co in zip(streams, (cqo_ref, cko_ref, cvo_ref)):
            for bi in range(nb):
                co[bi] = xs[bi, hist0:CONV_PAD, :]

    @pl.when(t == n_t - 1)
    def _():
        hout_ref[...] = h_scr[...]


def _delta_rule(proj, gates, conv_w, conv0, h0, o_norm_w, *, row0, batch, seq, n_heads, dk,
                C, nb, nc, hb, out_rows):
    ttb = nc * C
    tt = nb * ttb
    w = hb * dk
    ng = n_heads // hb
    n_t = seq // ttb
    assert seq % ttb == 0 and batch % nb == 0 and n_heads % hb == 0 and row0 % tt == 0
    zero_init = h0 is None
    d_qk = n_heads * dk
    cb = d_qk // w
    rb0 = row0 // tt

    rows = batch * seq
    g5 = gates.reshape(4, ng, hb, rows)
    gcol = jnp.transpose(g5, (1, 3, 0, 2)).reshape(ng, rows, 4 * hb)
    grow = jnp.transpose(g5[3].reshape(ng, hb, rows // C, C), (0, 2, 1, 3))

    def tok_map(part):
        return lambda b, g, t: (rb0 + b * n_t + t, part * cb + g)

    if zero_init:
        h0_arg = jnp.zeros((nb, hb, dk, dk), F32)
        h0_spec = pl.BlockSpec((nb, hb, dk, dk), lambda b, g, t: (0, 0, 0, 0))
    else:
        h0_arg = h0
        h0_spec = pl.BlockSpec((nb, hb, dk, dk), lambda b, g, t: (b, g, 0, 0))
    cs_spec = [pl.BlockSpec((nb, CONV_W - 1, w), (lambda b, g, t, p=p: (b, 0, p * cb + g))) for p in range(3)]
    cw_spec = [pl.BlockSpec((CONV_W, w), (lambda b, g, t, p=p: (0, p * cb + g))) for p in range(3)]
    kern = functools.partial(_delta_kernel, C=C, nb=nb, nc=nc, hb=hb, dk=dk, zero_init=zero_init)
    o, h_fin, cq, ck, cv = pl.pallas_call(
        kern,
        grid=(batch // nb, ng, n_t),
        in_specs=[pl.BlockSpec((tt, w), tok_map(0)), pl.BlockSpec((tt, w), tok_map(1)),
                  pl.BlockSpec((tt, w), tok_map(2)), pl.BlockSpec((tt, w), tok_map(3)),
                  *cw_spec, *cs_spec,
                  pl.BlockSpec((1, tt, 4 * hb), lambda b, g, t: (g, b * n_t + t, 0)),
                  pl.BlockSpec((1, nb * nc, hb, C), lambda b, g, t: (g, b * n_t + t, 0, 0)),
                  h0_spec,
                  pl.BlockSpec((1, dk), lambda b, g, t: (0, 0))],
        out_specs=[pl.BlockSpec((tt, w), lambda b, g, t: (rb0 + b * n_t + t, g)),
                   pl.BlockSpec((nb, hb, dk, dk), lambda b, g, t: (b, g, 0, 0)),
                   *[pl.BlockSpec((nb, CONV_W - 1, w), lambda b, g, t: (b, 0, g)) for _ in range(3)]],
        out_shape=[jax.ShapeDtypeStruct((out_rows, d_qk), BF16),
                   jax.ShapeDtypeStruct((batch, n_heads, dk, dk), F32),
                   *[jax.ShapeDtypeStruct((batch, CONV_W - 1, d_qk), F32) for _ in range(3)]],
        scratch_shapes=[pltpu.VMEM((nb, hb, dk, dk), F32),
                        *[pltpu.VMEM((nb, CONV_PAD + ttb, w), F32) for _ in range(3)],
                        *[pltpu.VMEM((tt, w), F32) for _ in range(3)]],
        compiler_params=_cparams(("arbitrary", "arbitrary", "arbitrary")),
        name="delta_rule",
    )(proj, proj, proj, proj, conv_w, conv_w, conv_w, conv0, conv0, conv0, gcol, grow, h0_arg,
      o_norm_w.reshape(1, dk))
    return o, h_fin, jnp.concatenate([cq, ck, cv], axis=-1)


def _group_mean_sq(x, gmat_ref, inv_n):
    x2 = x * x
    hi = x2.astype(BF16)
    lo = (x2 - hi.astype(F32)).astype(BF16)
    g = gmat_ref[...]
    return (jnp.dot(hi, g, preferred_element_type=F32) + jnp.dot(lo, g, preferred_element_type=F32)) * inv_n


def _rope(y, cos_ref, sa_ref, sb_ref, half):
    w = y.shape[1]
    reps = w // cos_ref.shape[1]
    tile = lambda r: jnp.concatenate([r[...]] * reps, axis=1)
    return (y * tile(cos_ref) + pltpu.roll(y, w - half, axis=1) * tile(sa_ref)
            + pltpu.roll(y, half, axis=1) * tile(sb_ref))


def _q_post_kernel(x_ref, w_ref, gmat_ref, cos_ref, sa_ref, sb_ref, o_ref, *, hd, half, scale):
    gw = gmat_ref.shape[0]
    for c in range(x_ref.shape[1] // gw):
        cols = slice(c * gw, (c + 1) * gw)
        x = x_ref[:, cols]
        y = x * lax.rsqrt(_group_mean_sq(x, gmat_ref, 1.0 / hd) + EPS) * w_ref[...]
        o_ref[:, cols] = (_rope(y, cos_ref, sa_ref, sb_ref, half) * scale).astype(o_ref.dtype)


def _kv_post_kernel(k_ref, v_ref, w_ref, gmat_ref, cos_ref, sa_ref, sb_ref,
                    kf_ref, kb_ref, vbt_ref, kwin_ref, vwin_ref, *, hd, half, n_prompt_blocks, blocks_per_seq):
    i = pl.program_id(0)
    x = k_ref[...]
    y = x * lax.rsqrt(_group_mean_sq(x, gmat_ref, 1.0 / hd) + EPS) * w_ref[...]
    kf = _rope(y, cos_ref, sa_ref, sb_ref, half)
    kf_ref[...] = kf
    v = v_ref[...]
    vt = v.T
    for hh in range(kb_ref.shape[0]):
        kb_ref[hh] = kf[:, hh * hd:(hh + 1) * hd].astype(BF16)
        vbt_ref[hh] = vt[hh * hd:(hh + 1) * hd, :].astype(BF16)

    @pl.when((i < n_prompt_blocks) & (i % blocks_per_seq == blocks_per_seq - 1))
    def _():
        wb = kwin_ref.shape[0]
        kwin_ref[...] = kf[kf.shape[0] - wb:, :]
        vwin_ref[...] = v[v.shape[0] - wb:, :]


def _rope_tables(pos, hd, rot_dim):
    half = rot_dim // 2
    inv_freq = jnp.power(ROPE_THETA, -jnp.arange(half, dtype=F32) * 2.0 / rot_dim)
    ang = pos[:, None] * inv_freq[None, :]
    cos, sin = jnp.cos(ang), jnp.sin(ang)
    m = pos.shape[0]
    ones = jnp.ones((m, hd - rot_dim), F32)
    zeros = jnp.zeros((m, hd - rot_dim), F32)
    zh = jnp.zeros((m, half), F32)
    c = jnp.concatenate([cos, cos, ones], axis=1)
    sa = jnp.concatenate([-sin, zh, zeros], axis=1)
    sb = jnp.concatenate([zh, sin, zeros], axis=1)
    reps = LANES // hd
    return tuple(jnp.tile(a, (1, reps)) for a in (c, sa, sb))


def _group_matrix(width, hd):
    r = jnp.arange(width) // hd
    return (r[:, None] == r[None, :]).astype(BF16)


def _softmax_sink(s, mask, sink):
    s = jnp.where(mask, s, -jnp.inf)
    m = jnp.maximum(jnp.max(s, axis=-1, keepdims=True), sink)
    p = jnp.exp2(s - m)
    denom = jnp.sum(p, axis=-1, keepdims=True) + jnp.exp2(sink - m)
    return (p / denom).astype(BF16)


def _attn_prompt_kernel(sink_ref, q_ref, kp_ref, kc_ref, vp_ref, vc_ref, o_ref, *, kv_heads, group, hd, win):
    nb = pl.program_id(1)
    kj = lax.broadcasted_iota(jnp.int32, (2 * win, win), 0)
    qi = lax.broadcasted_iota(jnp.int32, (2 * win, win), 1)
    mask = (kj > qi) & (kj <= qi + win) & ((kj >= win) | (nb > 0))

    def scores(h):
        kk = jnp.concatenate([kp_ref[h], kc_ref[h]], axis=0)
        return [lax.dot_general(kk, q_ref[:, (h * group + g) * hd:(h * group + g + 1) * hd],
                                (((1,), (1,)), ((), ())), preferred_element_type=F32) for g in range(group)]

    def row_max(h, ss):
        out = []
        for g, s in enumerate(ss):
            s = jnp.where(mask, s, -jnp.inf)
            out.append((s, jnp.maximum(jnp.max(s, axis=0, keepdims=True), sink_ref[h * group + g] * LOG2E)))
        return out

    def probs(h, sm):
        out = []
        for g, (s, m) in enumerate(sm):
            p = jnp.exp2(s - m)
            denom = jnp.sum(p, axis=0, keepdims=True) + jnp.exp2(sink_ref[h * group + g] * LOG2E - m)
            out.append((p.astype(BF16), denom))
        return out

    def weighted_values(h, pd):
        vt = jnp.concatenate([vp_ref[h], vc_ref[h]], axis=1)
        ot = jnp.concatenate([jnp.dot(vt, p, preferred_element_type=F32) / denom for p, denom in pd], axis=0)
        o_ref[:, h * group * hd:(h + 1) * group * hd] = ot.T.astype(o_ref.dtype)

    stages = (scores, row_max, probs, weighted_values)
    carry = [None] * len(stages)
    for step in range(kv_heads + len(stages) - 1):
        for k in reversed(range(len(stages))):
            h = step - k
            if 0 <= h < kv_heads:
                carry[k] = stages[k](h) if k == 0 else stages[k](h, carry[k - 1])


def _attn_prompt(q, kb, vbt, sinks, *, batch, seq, kv_heads, group, hd, out_rows):
    win = WINDOW
    nblk = seq // win
    qw = kv_heads * group * hd
    k_spec = lambda back: pl.BlockSpec((kv_heads, win, hd),
                                       lambda b, n: (0, b * nblk + jnp.maximum(n - back, 0), 0))
    v_spec = lambda back: pl.BlockSpec((kv_heads, hd, win),
                                       lambda b, n: (0, 0, b * nblk + jnp.maximum(n - back, 0)))
    return pl.pallas_call(
        functools.partial(_attn_prompt_kernel, kv_heads=kv_heads, group=group, hd=hd, win=win),
        grid=(batch, nblk),
        in_specs=[pl.BlockSpec(memory_space=pltpu.SMEM),
                  pl.BlockSpec((win, qw), lambda b, n: (b * nblk + n, 0)),
                  k_spec(1), k_spec(0), v_spec(1), v_spec(0)],
        out_specs=pl.BlockSpec((win, qw), lambda b, n: (b * nblk + n, 0)),
        out_shape=jax.ShapeDtypeStruct((out_rows, qw), BF16),
        compiler_params=_cparams(("arbitrary", "arbitrary")),
        name="attn_prompt",
    )(sinks, q, kb, kb, vbt, vbt)


def _attn_sample_kernel(sink_ref, q_ref, kn_ref, vn_ref, kc_ref, vc_ref, prev_ref, o_ref,
                        *, nbs, tq, kv_heads, group, hd, win):
    del prev_ref
    nctx = 2 * win
    r = lax.broadcasted_iota(jnp.int32, (group * tq, nctx), 0)
    kj = lax.broadcasted_iota(jnp.int32, (group * tq, nctx), 1)
    tpos = r % tq
    mask = ((kj < win) & (kj > tpos)) | ((kj >= win) & (kj - win <= tpos))
    pad = jnp.zeros((nctx - win - tq, hd), F32)
    items = [(bi, h) for bi in range(nbs) for h in range(kv_heads)]
    qbs = [q_ref[bi * tq:(bi + 1) * tq, :].astype(F32) for bi in range(nbs)]
    sinks = [jnp.concatenate([jnp.full((tq, 1), sink_ref[h * group + g] * LOG2E, F32) for g in range(group)],
                             axis=0) for h in range(kv_heads)]
    ss, vs = [], []
    for bi, h in items:
        cs = slice(h * hd, (h + 1) * hd)
        trows = slice(bi * tq, (bi + 1) * tq)
        kctx = jnp.concatenate([kc_ref[bi][:, cs], kn_ref[trows, cs], pad], axis=0)
        vs.append(jnp.concatenate([vc_ref[bi][:, cs], vn_ref[trows, cs], pad], axis=0).astype(BF16))
        qg = jnp.concatenate([qbs[bi][:, (h * group + g) * hd:(h * group + g + 1) * hd]
                              for g in range(group)], axis=0)
        ss.append(_bdot_nt(qg, kctx))
    ps = [_softmax_sink(s, mask, sinks[h]) for (bi, h), s in zip(items, ss)]
    for (bi, h), p, v in zip(items, ps, vs):
        o = jnp.dot(p, v, preferred_element_type=F32)
        for g in range(group):
            hh = h * group + g
            o_ref[bi * tq:(bi + 1) * tq, hh * hd:(hh + 1) * hd] = o[g * tq:(g + 1) * tq, :].astype(o_ref.dtype)


def _attn_sample(q, kf, kv, cache_k, cache_v, sinks, o_prev, *, row0, batch, tq, kv_heads, group, hd, nbs):
    win = cache_k.shape[1]
    assert win == WINDOW and batch % nbs == 0 and row0 % (nbs * tq) == 0
    rows = nbs * tq
    rb0 = row0 // rows
    kvw = kv_heads * hd
    return pl.pallas_call(
        functools.partial(_attn_sample_kernel, nbs=nbs, tq=tq, kv_heads=kv_heads, group=group, hd=hd, win=win),
        grid=(batch // nbs,),
        in_specs=[pl.BlockSpec(memory_space=pltpu.SMEM),
                  pl.BlockSpec((rows, q.shape[1]), lambda b: (rb0 + b, 0)),
                  pl.BlockSpec((rows, kvw), lambda b: (rb0 + b, 0)),
                  pl.BlockSpec((rows, kvw), lambda b: (rb0 + b, 1)),
                  pl.BlockSpec((nbs, win, kvw), lambda b: (b, 0, 0)),
                  pl.BlockSpec((nbs, win, kvw), lambda b: (b, 0, 0)),
                  pl.BlockSpec(memory_space=pl.ANY)],
        out_specs=pl.BlockSpec((rows, q.shape[1]), lambda b: (rb0 + b, 0)),
        out_shape=jax.ShapeDtypeStruct(o_prev.shape, o_prev.dtype),
        input_output_aliases={6: 0},
        compiler_params=_cparams(("arbitrary",)),
        name="attn_sample",
    )(sinks, q, kf, kv, cache_k, cache_v, o_prev)


PAST_LEN = 16384
TM_NORM = 256
M_STEPS = 8


Q_POST_COLS = 2048
TN_WIDE = 1024
TN_RES = 512
TN_FFN = 256
DOWN_KBLOCKS = 2


def _write_rows_kernel(src_ref, dst_ref, o_ref):
    del dst_ref
    o_ref[...] = src_ref[...]


def _merge_rows(dst, src, row0, tm):
    n = src.shape[1]
    rows = src.shape[0] - row0
    assert rows % tm == 0 and row0 % tm == 0
    return pl.pallas_call(
        _write_rows_kernel,
        grid=(rows // tm,),
        in_specs=[pl.BlockSpec((tm, n), lambda i: (row0 // tm + i, 0)),
                  pl.BlockSpec(memory_space=pl.ANY)],
        out_specs=pl.BlockSpec((tm, n), lambda i: (row0 // tm + i, 0)),
        out_shape=jax.ShapeDtypeStruct(dst.shape, dst.dtype),
        input_output_aliases={1: 0},
        compiler_params=_cparams(("arbitrary",)),
        name="merge_rows",
    )(src, dst)


def kernel(x_prompt, x_sample, state_ssm, state_conv, cache_win_k, cache_win_v, norm_mix, norm_ffn,
           a_w_in, a_conv_w, a_log, a_dt_bias, a_o_norm, a_w_out, kv_norm, w_kv, k_norm, b_w_q,
           b_q_norm, b_sinks, b_w_o, ffn_w_gu, ffn_w_down):
    bp, sp, d = x_prompt.shape
    bs, ss, _ = x_sample.shape
    mp, ms = bp * sp, bs * ss
    m = mp + ms
    n_a = a_w_in.shape[0]
    assert n_a == 1 and b_w_q.shape[0] == 1, "layer pattern is one delta layer then one attention layer"
    h_a = a_log.shape[1]
    dk = a_o_norm.shape[1]
    d_qk = h_a * dk
    kv_heads, hd = cache_win_k.shape[2], cache_win_k.shape[3]
    kvw = kv_heads * hd
    h_b = b_sinks.shape[1]
    group = h_b // kv_heads
    rot_dim = hd // 4

    def mm(xb, w, n_out, res=None):
        tn = min(TN_WIDE if res is None else TN_RES, n_out)
        return _mm_stream(xb, w, 0, n_out=n_out, tn=tn, n_m=M_STEPS, res=res)

    def ffn(hin, hn, layer, split_streams=False):
        act = _ffn_up(hn, ffn_w_gu, layer, n_m=M_STEPS, tn=TN_FFN)
        out = hin
        for kb in range(DOWN_KBLOCKS):
            tail = ms if split_streams and kb == DOWN_KBLOCKS - 1 else 0
            out = _mm_stream(act, ffn_w_down, layer, n_out=d, tn=min(TN_RES, d), n_m=M_STEPS,
                             kblock=kb, n_kblocks=DOWN_KBLOCKS, res=out, tail=tail)
        return out

    xp2, xs2 = x_prompt.reshape(mp, d), x_sample.reshape(ms, d)
    (xn,) = _norm_inputs(xp2, xs2, norm_mix[0], TM_NORM)
    w_in_t = jnp.swapaxes(a_w_in, 1, 2)
    proj = _mm_stream(xn, w_in_t, 0, n_out=4 * d_qk, tn=TN_WIDE, n_m=M_STEPS, w_t=True)
    w_ba_t = w_in_t[0, 4 * d_qk:, :].astype(BF16)
    c_s = math.gcd(ss, DELTA_CHUNK)
    gates = _gates(xn, w_ba_t, a_log[0], a_dt_bias[0], tm=TM_NORM, n_prompt_rows=mp,
                   c_prompt=DELTA_CHUNK, c_sample=c_s)
    conv_w = a_conv_w[0]
    conv0_p = jnp.zeros((bp, CONV_W - 1, 3 * d_qk), F32)
    o_p, ssm_p, conv_p = _delta_rule(proj, gates[:, :, :mp], conv_w, conv0_p, None, a_o_norm[0],
                                     row0=0, batch=bp, seq=sp, n_heads=h_a, dk=dk,
                                     C=DELTA_CHUNK, nb=1, nc=4, hb=8, out_rows=m)
    o_s, ssm_s, conv_s = _delta_rule(proj, gates[:, :, mp:], conv_w, state_conv[0], state_ssm[0], a_o_norm[0],
                                     row0=mp, batch=bs, seq=ss, n_heads=h_a, dk=dk,
                                     C=c_s, nb=4, nc=1, hb=8, out_rows=m)
    o_a = _merge_rows(o_p, o_s, mp, TM_NORM)
    hn1, h1 = _norm_inputs(xp2, xs2, norm_ffn[0], TM_NORM, y=mm(o_a, a_w_out, d))
    h2 = ffn(h1, hn1, 0)

    hkv, hq = _norm(h2, jnp.stack([kv_norm, norm_mix[1]]), TM_NORM)
    kv = mm(hkv, w_kv[None], 2 * kvw)
    pos = jnp.concatenate([jnp.tile(jnp.arange(sp, dtype=F32), bp),
                           jnp.tile(PAST_LEN + jnp.arange(ss, dtype=F32), bs)])
    cos_t, sa_t, sb_t = _rope_tables(pos, hd, rot_dim)
    gmat = _group_matrix(kvw, hd)
    wb = min(WINDOW, sp)
    bps = sp // TM_NORM
    assert sp % TM_NORM == 0 and wb <= TM_NORM
    win_spec = pl.BlockSpec((wb, kvw), lambda i: (jnp.minimum(i // bps, bp - 1), 0))
    kf, kb, vbt, kwin_p, vwin_p = pl.pallas_call(
        functools.partial(_kv_post_kernel, hd=hd, half=rot_dim // 2, n_prompt_blocks=mp // TM_NORM,
                          blocks_per_seq=bps),
        grid=(m // TM_NORM,),
        in_specs=[pl.BlockSpec((TM_NORM, kvw), lambda i: (i, 0)),
                  pl.BlockSpec((TM_NORM, kvw), lambda i: (i, 1)),
                  pl.BlockSpec((1, kvw), lambda i: (0, 0)),
                  pl.BlockSpec((kvw, kvw), lambda i: (0, 0)),
                  *[pl.BlockSpec((TM_NORM, LANES), lambda i: (i, 0)) for _ in range(3)]],
        out_specs=[pl.BlockSpec((TM_NORM, kvw), lambda i: (i, 0)),
                   pl.BlockSpec((kv_heads, TM_NORM, hd), lambda i: (0, i, 0)),
                   pl.BlockSpec((kv_heads, hd, TM_NORM), lambda i: (0, 0, i)),
                   win_spec, win_spec],
        out_shape=[jax.ShapeDtypeStruct((m, kvw), F32),
                   jax.ShapeDtypeStruct((kv_heads, m, hd), BF16),
                   jax.ShapeDtypeStruct((kv_heads, hd, m), BF16),
                   jax.ShapeDtypeStruct((bp * wb, kvw), F32),
                   jax.ShapeDtypeStruct((bp * wb, kvw), F32)],
        compiler_params=_cparams(("arbitrary",)),
        name="kv_post",
    )(kv, kv, jnp.tile(k_norm, kv_heads).reshape(1, kvw), gmat, cos_t, sa_t, sb_t)

    qraw = mm(hq, b_w_q, h_b * hd)
    qw = kvw
    qcols = h_b * hd
    qblk = min(Q_POST_COLS, qcols)
    assert qcols % qblk == 0 and qblk % qw == 0
    q = pl.pallas_call(
        functools.partial(_q_post_kernel, hd=hd, half=rot_dim // 2, scale=hd ** -0.5 * LOG2E),
        grid=(m // TM_NORM, qcols // qblk),
        in_specs=[pl.BlockSpec((TM_NORM, qblk), lambda i, j: (i, j)),
                  pl.BlockSpec((1, qw), lambda i, j: (0, 0)),
                  pl.BlockSpec((qw, qw), lambda i, j: (0, 0)),
                  *[pl.BlockSpec((TM_NORM, LANES), lambda i, j: (i, 0)) for _ in range(3)]],
        out_specs=pl.BlockSpec((TM_NORM, qblk), lambda i, j: (i, j)),
        out_shape=jax.ShapeDtypeStruct((m, qcols), BF16),
        compiler_params=_cparams(("arbitrary", "arbitrary")),
        name="q_post",
    )(qraw, jnp.tile(b_q_norm[0], qw // hd).reshape(1, qw), gmat, cos_t, sa_t, sb_t)
    sinks = b_sinks[0]
    o_b = _attn_prompt(q, kb, vbt, sinks, batch=bp, seq=sp, kv_heads=kv_heads, group=group, hd=hd, out_rows=m)
    o_b = _attn_sample(q, kf, kv, cache_win_k.reshape(bs, WINDOW, kvw), cache_win_v.reshape(bs, WINDOW, kvw),
                       sinks, o_b, row0=mp, batch=bs, tq=ss, kv_heads=kv_heads, group=group, hd=hd, nbs=2)
    h3 = mm(o_b, b_w_o, d, res=h2)
    (hn3,) = _norm(h3, norm_ffn[1:2], TM_NORM)
    y_p, y_s = ffn(h3, hn3, 1, split_streams=True)

    kf_p = kwin_p.reshape(bp, wb, kv_heads, hd)
    vf_p = vwin_p.reshape(bp, wb, kv_heads, hd)
    kf_s = kf[mp:].reshape(bs, ss, kv_heads, hd)
    vf_s = kv[mp:, kvw:].reshape(bs, ss, kv_heads, hd)
    wk_s = jnp.concatenate([cache_win_k[:, ss:], kf_s], axis=1)
    wv_s = jnp.concatenate([cache_win_v[:, ss:], vf_s], axis=1)
    return (y_p.reshape(bp, sp, d), y_s.reshape(bs, ss, d),
            ssm_p[None], conv_p[None], kf_p, vf_p, ssm_s[None], conv_s[None], wk_s, wv_s)
```

```python
import functools
import itertools
import math

import jax
import jax.numpy as jnp
from jax import lax
from jax.experimental import pallas as pl
from jax.experimental.pallas import tpu as pltpu

F32 = jnp.float32
BF16 = jnp.bfloat16

EPS = 1e-6
LOG2E = math.log2(math.e)
WINDOW = 128
ROPE_THETA = 500000.0
CONV_W = 4
DELTA_CHUNK = 64
LANES = 128
CONV_PAD = 8
GROUP_CHUNKS = 2
VMEM_LIMIT = 56 * 1024 * 1024


def _cparams(sem):
    return pltpu.CompilerParams(dimension_semantics=sem, vmem_limit_bytes=VMEM_LIMIT)


def _sigmoid(x):
    return 1.0 / (1.0 + jnp.exp(-x))


def _bdot(a, b):
    return jnp.dot(a.astype(BF16), b.astype(BF16), preferred_element_type=F32)


def _bdot_nt(a, b):
    return lax.dot_general(a.astype(BF16), b.astype(BF16), (((1,), (1,)), ((), ())),
                           preferred_element_type=F32)


def _bdot_tn(a, b):
    return lax.dot_general(a.astype(BF16), b.astype(BF16), (((0,), (0,)), ((), ())),
                           preferred_element_type=F32)


def _norm_inputs_kernel(xp_ref, xs_ref, w_ref, *refs, n_prompt_blocks, has_y):
    i = pl.program_id(0)

    def emit(x):
        if has_y:
            y_ref, xn_ref, h_ref = refs
            x = x + y_ref[...]
            h_ref[...] = x
        else:
            (xn_ref,) = refs
        r = lax.rsqrt(jnp.mean(x * x, axis=-1, keepdims=True) + EPS)
        xn_ref[...] = (x * r * w_ref[...]).astype(xn_ref.dtype)

    @pl.when(i < n_prompt_blocks)
    def _():
        emit(xp_ref[...])

    @pl.when(i >= n_prompt_blocks)
    def _():
        emit(xs_ref[...])


def _norm_inputs(xp, xs, w, tm, y=None):
    mp, d = xp.shape
    ms = xs.shape[0]
    assert mp % tm == 0 and ms == tm
    npb = mp // tm
    m = mp + ms
    row = pl.BlockSpec((tm, d), lambda i: (i, 0))
    return pl.pallas_call(
        functools.partial(_norm_inputs_kernel, n_prompt_blocks=npb, has_y=y is not None),
        grid=(npb + 1,),
        in_specs=[pl.BlockSpec((tm, d), lambda i: (jnp.minimum(i, npb - 1), 0)),
                  pl.BlockSpec((tm, d), lambda i: (0, 0)),
                  pl.BlockSpec((1, d), lambda i: (0, 0))] + ([row] if y is not None else []),
        out_specs=[row] + ([row] if y is not None else []),
        out_shape=[jax.ShapeDtypeStruct((m, d), BF16)] + ([jax.ShapeDtypeStruct((m, d), F32)] if y is not None else []),
        compiler_params=_cparams(("arbitrary",)),
        name="norm_inputs",
    )(xp, xs, w.reshape(1, d), *([y] if y is not None else []))


def _norm_kernel(x_ref, w_ref, *o_refs):
    x = x_ref[...]
    y = x * lax.rsqrt(jnp.mean(x * x, axis=-1, keepdims=True) + EPS)
    for i, o_ref in enumerate(o_refs):
        o_ref[...] = (y * w_ref[i:i + 1, :]).astype(o_ref.dtype)


def _norm(x, ws, tm):
    m, d = x.shape
    nw = ws.shape[0]
    assert m % tm == 0
    return pl.pallas_call(
        _norm_kernel,
        grid=(m // tm,),
        in_specs=[pl.BlockSpec((tm, d), lambda i: (i, 0)),
                  pl.BlockSpec((nw, d), lambda i: (0, 0))],
        out_specs=[pl.BlockSpec((tm, d), lambda i: (i, 0)) for _ in range(nw)],
        out_shape=[jax.ShapeDtypeStruct((m, d), BF16) for _ in range(nw)],
        compiler_params=_cparams(("arbitrary",)),
        name="norm",
    )(x, ws)


def _first_rows(j, i):
    return jnp.where(j == 0, 0, i)


def _mm_stream_kernel(*refs, has_res, w_t, n_m, tail):
    x_ref, w_ref, *mid, wbf0_ref, wbf1_ref = refs
    res_ref = mid[0] if has_res else None
    out_refs = mid[1:] if has_res else mid
    j = pl.program_id(0)
    i = pl.program_id(1)
    rows_c = w_ref.shape[0]
    ic = jnp.minimum(i, n_m - 1)
    chunk_rows = pl.ds(pl.multiple_of(ic * rows_c, rows_c), rows_c)

    def step(cast_ref, use_ref, rows, o_ref):
        if o_ref is not None:
            contract = (((1,), (1 if w_t else 0,)), ((), ()))
            acc = lax.dot_general(x_ref[:rows, :], use_ref[...], contract, preferred_element_type=F32)
            if has_res:
                acc = acc + res_ref[:rows, :]
            o_ref[...] = acc.astype(o_ref.dtype)
        cast_ref[chunk_rows, :] = w_ref[...].astype(BF16)

    @pl.when(j == 0)
    def _():
        step(wbf0_ref, None, None, None)

    for parity, (cast_ref, use_ref) in enumerate(((wbf0_ref, wbf1_ref), (wbf1_ref, wbf0_ref))):
        @pl.when((j > 0) & (j % 2 == parity) & (i < n_m))
        def _(cast_ref=cast_ref, use_ref=use_ref):
            step(cast_ref, use_ref, x_ref.shape[0], out_refs[0])

        if tail:
            @pl.when((j > 0) & (j % 2 == parity) & (i == n_m))
            def _(cast_ref=cast_ref, use_ref=use_ref):
                step(cast_ref, use_ref, tail, out_refs[1])


def _mm_stream(x, w, layer, *, n_out, tn, n_m, kblock=0, n_kblocks=1, res=None, out_dtype=F32, w_t=False,
               tail=0):
    m = x.shape[0]
    m_main = m - tail
    k = x.shape[1] // n_kblocks
    tm, nj = m_main // n_m, n_out // tn
    rows_c = (tn if w_t else k) // n_m
    assert m_main % n_m == 0 and tm % 16 == 0 and rows_c * n_m == (tn if w_t else k) and rows_c % 16 == 0
    assert n_out % tn == 0 and w.shape[2 if w_t else 1] == k * n_kblocks == x.shape[1] and k % LANES == 0
    assert 0 <= tail <= tm and tail % 8 == 0
    chunk = lambda i: jnp.minimum(i, n_m - 1)
    if w_t:
        w_spec = pl.BlockSpec((None, rows_c, k),
                              lambda j, i: (layer, jnp.minimum(j, nj - 1) * n_m + chunk(i), kblock))
    else:
        w_spec = pl.BlockSpec((None, rows_c, tn),
                              lambda j, i: (layer, kblock * n_m + chunk(i), jnp.minimum(j, nj - 1)))
    in_specs = [pl.BlockSpec((tm, k), lambda j, i: (_first_rows(j, i), kblock)), w_spec]
    args = [x, w]
    col = lambda j: jnp.maximum(j - 1, 0)
    if res is not None:
        in_specs.append(pl.BlockSpec((tm, tn), lambda j, i: (_first_rows(j, i), col(j))))
        args.append(res)
    out_specs = [pl.BlockSpec((tm, tn), lambda j, i: (jnp.minimum(_first_rows(j, i), n_m - 1), col(j)))]
    out_shape = [jax.ShapeDtypeStruct((m_main, n_out), out_dtype)]
    if tail:
        out_specs.append(pl.BlockSpec((tail, tn), lambda j, i: (0, col(j))))
        out_shape.append(jax.ShapeDtypeStruct((tail, n_out), out_dtype))
    outs = pl.pallas_call(
        functools.partial(_mm_stream_kernel, has_res=res is not None, w_t=w_t, n_m=n_m, tail=tail),
        grid=(nj + 1, n_m + (1 if tail else 0)),
        in_specs=in_specs,
        out_specs=out_specs,
        out_shape=out_shape,
        scratch_shapes=[pltpu.VMEM((tn, k) if w_t else (k, tn), BF16) for _ in range(2)],
        compiler_params=_cparams(("arbitrary", "arbitrary")),
        name="mm_stream",
    )(*args)
    return tuple(outs) if tail else outs[0]


FFN_PAIRS = 2


def _ffn_up_kernel(x_ref, *refs, tn):
    w_refs, (o_ref, wbf0_ref, wbf1_ref) = refs[:2 * FFN_PAIRS], refs[2 * FFN_PAIRS:]
    j = pl.program_id(0)
    i = pl.program_id(1)
    rows_c = w_refs[0].shape[0]
    rows = pl.ds(pl.multiple_of(i * rows_c, rows_c), rows_c)

    def step(cast_ref, use_ref):
        if use_ref is not None:
            gu = jnp.dot(x_ref[...], use_ref[...], preferred_element_type=F32)
            g = gu[:, :FFN_PAIRS * tn]
            o_ref[...] = (g * _sigmoid(g) * gu[:, FFN_PAIRS * tn:]).astype(o_ref.dtype)
        for c, w_ref in enumerate(w_refs):
            cast_ref[rows, c * tn:(c + 1) * tn] = w_ref[...].astype(BF16)

    @pl.when(j == 0)
    def _():
        step(wbf0_ref, None)

    for parity, (cast_ref, use_ref) in enumerate(((wbf0_ref, wbf1_ref), (wbf1_ref, wbf0_ref))):
        @pl.when((j > 0) & (j % 2 == parity))
        def _(cast_ref=cast_ref, use_ref=use_ref):
            step(cast_ref, use_ref)


def _ffn_up(x, w_gu, layer, *, n_m, tn):
    m, k = x.shape
    f = w_gu.shape[2] // 2
    tm, rows_c = m // n_m, k // n_m
    assert m % n_m == 0 and tm % 16 == 0 and k % n_m == 0 and rows_c % 16 == 0 and f % tn == 0
    n_pairs = f // tn
    nj = pl.cdiv(n_pairs, FFN_PAIRS)

    def w_map(c):
        g, part = c % FFN_PAIRS, c // FFN_PAIRS

        def index(j, i):
            pair = jnp.minimum(jnp.minimum(j, nj - 1) * FFN_PAIRS + g, n_pairs - 1)
            return layer, i, part * n_pairs + pair
        return index

    out_map = lambda j, i: (_first_rows(j, i), jnp.maximum(j - 1, 0))
    return pl.pallas_call(
        functools.partial(_ffn_up_kernel, tn=tn),
        grid=(nj + 1, n_m),
        in_specs=[pl.BlockSpec((tm, k), lambda j, i: (_first_rows(j, i), 0)),
                  *[pl.BlockSpec((None, rows_c, tn), w_map(c)) for c in range(2 * FFN_PAIRS)]],
        out_specs=pl.BlockSpec((tm, FFN_PAIRS * tn), out_map),
        out_shape=jax.ShapeDtypeStruct((m, f), BF16),
        scratch_shapes=[pltpu.VMEM((k, 2 * FFN_PAIRS * tn), BF16) for _ in range(2)],
        compiler_params=_cparams(("arbitrary", "arbitrary")),
        name="ffn_up",
    )(x, *([w_gu] * (2 * FFN_PAIRS)))


def _gates_kernel(x_ref, w_ref, alog_ref, dtb_ref, o_ref, *, n_heads, n_prompt_blocks, c_prompt, c_sample):
    tm = x_ref.shape[0]
    ba = lax.dot_general(w_ref[...], x_ref[...], (((1,), (1,)), ((), ())), preferred_element_type=F32)
    beta = _sigmoid(ba[:n_heads])
    a = ba[n_heads:] + dtb_ref[...]
    softplus = jnp.maximum(a, 0.0) + jnp.log(1.0 + jnp.exp(-jnp.abs(a)))
    g = -jnp.exp(alog_ref[...]) * softplus
    shift = jnp.where(pl.program_id(0) < n_prompt_blocks,
                      int(math.log2(c_prompt)), int(math.log2(c_sample)))
    jj = lax.broadcasted_iota(jnp.int32, (tm, tm), 0)
    ii = lax.broadcasted_iota(jnp.int32, (tm, tm), 1)
    same = lax.shift_right_logical(jj, shift) == lax.shift_right_logical(ii, shift)
    cum_m = jnp.where(same & (jj <= ii), 1.0, 0.0).astype(F32)
    tot_m = jnp.where(same, 1.0, 0.0).astype(F32)
    gcum = jnp.dot(g, cum_m, preferred_element_type=F32, precision=lax.Precision.HIGHEST)
    glast = jnp.dot(g, tot_m, preferred_element_type=F32, precision=lax.Precision.HIGHEST)
    o_ref[0] = beta
    o_ref[1] = jnp.exp(gcum)
    o_ref[2] = jnp.exp(glast - gcum)
    o_ref[3] = gcum


def _gates(xn, w_ba_t, a_log, dt_bias, *, tm, n_prompt_rows, c_prompt, c_sample):
    m, k = xn.shape
    h = a_log.shape[0]
    assert m % tm == 0 and n_prompt_rows % tm == 0 and tm % c_prompt == 0 and tm % c_sample == 0
    assert w_ba_t.shape == (2 * h, k)
    return pl.pallas_call(
        functools.partial(_gates_kernel, n_heads=h, n_prompt_blocks=n_prompt_rows // tm,
                          c_prompt=c_prompt, c_sample=c_sample),
        grid=(m // tm,),
        in_specs=[pl.BlockSpec((tm, k), lambda i: (i, 0)),
                  pl.BlockSpec((2 * h, k), lambda i: (0, 0)),
                  pl.BlockSpec((h, 1), lambda i: (0, 0)),
                  pl.BlockSpec((h, 1), lambda i: (0, 0))],
        out_specs=pl.BlockSpec((4, h, tm), lambda i: (0, 0, i)),
        out_shape=jax.ShapeDtypeStruct((4, h, m), F32),
        compiler_params=_cparams(("arbitrary",)),
        name="gates",
    )(xn, w_ba_t, a_log.reshape(h, 1), dt_bias.reshape(h, 1))


def _delta_kernel(q_ref, k_ref, v_ref, z_ref, cwq_ref, cwk_ref, cwv_ref, csq_ref, csk_ref, csv_ref,
                  gcol_ref, grow_ref, h0_ref, onw_ref,
                  o_ref, hout_ref, cqo_ref, cko_ref, cvo_ref,
                  h_scr, xq_scr, xk_scr, xv_scr, aq_scr, ak_scr, av_scr,
                  *, C, nb, nc, hb, dk, zero_init):
    t = pl.program_id(2)
    n_t = pl.num_programs(2)
    ttb = nc * C
    hist0 = CONV_PAD - (CONV_W - 1)

    @pl.when(t == 0)
    def _():
        if zero_init:
            h_scr[...] = jnp.zeros(h_scr.shape, F32)
        else:
            h_scr[...] = h0_ref[...]
        for xs, cs in ((xq_scr, csq_ref), (xk_scr, csk_ref), (xv_scr, csv_ref)):
            for bi in range(nb):
                xs[bi, hist0:CONV_PAD, :] = cs[bi]

    streams = ((q_ref, xq_scr, cwq_ref, aq_scr), (k_ref, xk_scr, cwk_ref, ak_scr), (v_ref, xv_scr, cwv_ref, av_scr))
    for raw_ref, xs, _, _ in streams:
        for bi in range(nb):
            xs[bi, CONV_PAD:CONV_PAD + ttb, :] = raw_ref[bi * ttb:(bi + 1) * ttb, :]

    def conv_phases(grp):
        r0, n = grp[0] * C, len(grp) * C

        def phase(xs, cw_ref, act):
            def run():
                for bi in range(nb):
                    win = xs[bi, r0:r0 + CONV_PAD + n, :]
                    acc = win[CONV_PAD:] * cw_ref[CONV_W - 1:CONV_W, :]
                    for s in range(1, CONV_W):
                        acc = acc + pltpu.roll(win, s, axis=0)[CONV_PAD:] * cw_ref[CONV_W - 1 - s:CONV_W - s, :]
                    act[bi * ttb + r0:bi * ttb + r0 + n, :] = acc * _sigmoid(acc)
            return run
        return [phase(xs, cw_ref, act) for _, xs, cw_ref, act in streams]

    ii = lax.broadcasted_iota(jnp.int32, (C, C), 0)
    jj = lax.broadcasted_iota(jnp.int32, (C, C), 1)
    n_levels = int(math.log2(C))
    assert 2 ** n_levels == C

    def prepare(bi, c, j):
        rows = slice(bi * ttb + c * C, bi * ttb + (c + 1) * C)
        cols = slice(j * dk, (j + 1) * dk)
        q = aq_scr[rows, cols]
        k = ak_scr[rows, cols]
        qn = q * (lax.rsqrt(jnp.sum(q * q, axis=-1, keepdims=True) + EPS) * (dk ** -0.5))
        kn = k * lax.rsqrt(jnp.sum(k * k, axis=-1, keepdims=True) + EPS)
        gcb = gcol_ref[0, rows, :]
        beta = gcb[:, j:j + 1]
        eg = gcb[:, hb + j:hb + j + 1]
        ekl = gcb[:, 2 * hb + j:2 * hb + j + 1]
        gc = gcb[:, 3 * hb + j:3 * hb + j + 1]
        gr = grow_ref[0, bi * nc + c, j:j + 1, :]
        dmat = jnp.where(ii >= jj, jnp.exp(gc - gr), 0.0)
        rhs = jnp.concatenate([kn * (beta * eg), av_scr[rows, cols] * beta], axis=1)
        return dict(rows=rows, cols=cols, qn=qn, kn=kn, beta=beta, eg=eg, dmat=dmat, rhs=rhs,
                    qe=qn * eg, kd=kn * ekl)

    st = {}

    def solve_phases(group):
        def scores():
            for it in group:
                e = st[it] = prepare(*it)
                e["s"] = _bdot_nt(jnp.concatenate([e["kn"], e["qn"]], axis=0), e["kn"])

        def square():
            for it in group:
                e = st[it]
                s = e.pop("s")
                e["qk"] = s[C:] * e["dmat"]
                e["t"] = jnp.where(ii > jj, s[:C] * e["dmat"] * (-e["beta"]), 0.0)
                e["p"] = _bdot(e["t"], e["t"])

        def level(last):
            def run():
                for it in group:
                    e = st[it]
                    e["tp"] = _bdot(e["t"], e["p"])
                    if not last:
                        e["p2"] = _bdot(e["p"], e["p"])
                for it in group:
                    e = st[it]
                    e["t"] = e["t"] + e["p"] + e.pop("tp")
                    e["p"] = None if last else e.pop("p2")
            return run

        def apply_t():
            for it in group:
                e = st[it]
                e["wu"] = e["rhs"] + _bdot(e["t"], e["rhs"])

        return [scores, square, *[level(lvl == n_levels - 2) for lvl in range(n_levels - 1)], apply_t]

    h_cur = {(bi, j): h_scr[bi, j] for bi in range(nb) for j in range(hb)}

    def chain_phases(c):
        sel = [(bi, c, j) for bi in range(nb) for j in range(hb)]

        def through_state():
            for it in sel:
                e = st[it]
                e["wq"] = _bdot(jnp.concatenate([e["wu"][:, :dk], e["qe"]], axis=0), h_cur[it[0], it[2]])

        def update():
            for it in sel:
                e = st[it]
                e["u"] = e["wu"][:, dk:] - e["wq"][:C]
            for it in sel:
                e = st[it]
                e["o"] = e["wq"][C:] + _bdot(e["qk"], e["u"])
                egl = e["eg"][C - 1:C, :]
                h_cur[it[0], it[2]] = h_cur[it[0], it[2]] * egl + _bdot_tn(e["kd"], e["u"])

        def emit():
            for it in sel:
                e = st.pop(it)
                o = e["o"]
                on = o * lax.rsqrt(jnp.mean(o * o, axis=-1, keepdims=True) + EPS) * onw_ref[...]
                zz = z_ref[e["rows"], e["cols"]]
                o_ref[e["rows"], e["cols"]] = (on * (zz * _sigmoid(zz))).astype(o_ref.dtype)

        return [through_state, update, emit]

    chunk_groups = [list(range(c0, min(c0 + GROUP_CHUNKS, nc))) for c0 in range(0, nc, GROUP_CHUNKS)]
    n_grp = len(chunk_groups)
    convs = [conv_phases(grp) for grp in chunk_groups]
    solves = [solve_phases([(bi, c, j) for bi in range(nb) for c in grp for j in range(hb)])
              for grp in chunk_groups]
    chains = [[phase for c in grp for phase in chain_phases(c)] for grp in chunk_groups]
    for g in range(-2, n_grp):
        tracks = [convs[g + 2] if g + 2 < n_grp else [],
                  solves[g + 1] if 0 <= g + 1 < n_grp else [],
                  chains[g] if g >= 0 else []]
        for phases in itertools.zip_longest(*tracks):
            for phase in phases:
                if phase is not None:
                    phase()
    for (bi, j), hv in h_cur.items():
        h_scr[bi, j] = hv
    for _, xs, _, _ in streams:
        for bi in range(nb):
            xs[bi, hist0:CONV_PAD, :] = xs[bi, CONV_PAD + ttb - (CONV_W - 1):CONV_PAD + ttb, :]

    @pl.when(t == n_t - 1)
    def _():
        for (_, xs, _, _), co in zip(streams, (cqo_ref, cko_ref, cvo_ref)):
            for bi in range(nb):
                co[bi] = xs[bi, hist0:CONV_PAD, :]

    @pl.when(t == n_t - 1)
    def _():
        hout_ref[...] = h_scr[...]


def _delta_rule(proj, gates, conv_w, conv0, h0, o_norm_w, *, row0, batch, seq, n_heads, dk,
                C, nb, nc, hb, out_rows):
    ttb = nc * C
    tt = nb * ttb
    w = hb * dk
    ng = n_heads // hb
    n_t = seq // ttb
    assert seq % ttb == 0 and batch % nb == 0 and n_heads % hb == 0 and row0 % tt == 0
    zero_init = h0 is None
    d_qk = n_heads * dk
    cb = d_qk // w
    rb0 = row0 // tt

    rows = batch * seq
    g5 = gates.reshape(4, ng, hb, rows)
    gcol = jnp.transpose(g5, (1, 3, 0, 2)).reshape(ng, rows, 4 * hb)
    grow = jnp.transpose(g5[3].reshape(ng, hb, rows // C, C), (0, 2, 1, 3))

    def tok_map(part):
        return lambda b, g, t: (rb0 + b * n_t + t, part * cb + g)

    if zero_init:
        h0_arg = jnp.zeros((nb, hb, dk, dk), F32)
        h0_spec = pl.BlockSpec((nb, hb, dk, dk), lambda b, g, t: (0, 0, 0, 0))
    else:
        h0_arg = h0
        h0_spec = pl.BlockSpec((nb, hb, dk, dk), lambda b, g, t: (b, g, 0, 0))
    cs_spec = [pl.BlockSpec((nb, CONV_W - 1, w), (lambda b, g, t, p=p: (b, 0, p * cb + g))) for p in range(3)]
    cw_spec = [pl.BlockSpec((CONV_W, w), (lambda b, g, t, p=p: (0, p * cb + g))) for p in range(3)]
    kern = functools.partial(_delta_kernel, C=C, nb=nb, nc=nc, hb=hb, dk=dk, zero_init=zero_init)
    o, h_fin, cq, ck, cv = pl.pallas_call(
        kern,
        grid=(batch // nb, ng, n_t),
        in_specs=[pl.BlockSpec((tt, w), tok_map(0)), pl.BlockSpec((tt, w), tok_map(1)),
                  pl.BlockSpec((tt, w), tok_map(2)), pl.BlockSpec((tt, w), tok_map(3)),
                  *cw_spec, *cs_spec,
                  pl.BlockSpec((1, tt, 4 * hb), lambda b, g, t: (g, b * n_t + t, 0)),
                  pl.BlockSpec((1, nb * nc, hb, C), lambda b, g, t: (g, b * n_t + t, 0, 0)),
                  h0_spec,
                  pl.BlockSpec((1, dk), lambda b, g, t: (0, 0))],
        out_specs=[pl.BlockSpec((tt, w), lambda b, g, t: (rb0 + b * n_t + t, g)),
                   pl.BlockSpec((nb, hb, dk, dk), lambda b, g, t: (b, g, 0, 0)),
                   *[pl.BlockSpec((nb, CONV_W - 1, w), lambda b, g, t: (b, 0, g)) for _ in range(3)]],
        out_shape=[jax.ShapeDtypeStruct((out_rows, d_qk), BF16),
                   jax.ShapeDtypeStruct((batch, n_heads, dk, dk), F32),
                   *[jax.ShapeDtypeStruct((batch, CONV_W - 1, d_qk), F32) for _ in range(3)]],
        scratch_shapes=[pltpu.VMEM((nb, hb, dk, dk), F32),
                        *[pltpu.VMEM((nb, CONV_PAD + ttb, w), F32) for _ in range(3)],
                        *[pltpu.VMEM((tt, w), F32) for _ in range(3)]],
        compiler_params=_cparams(("arbitrary", "arbitrary", "arbitrary")),
        name="delta_rule",
    )(proj, proj, proj, proj, conv_w, conv_w, conv_w, conv0, conv0, conv0, gcol, grow, h0_arg,
      o_norm_w.reshape(1, dk))
    return o, h_fin, jnp.concatenate([cq, ck, cv], axis=-1)


def _group_mean_sq(x, gmat_ref, inv_n):
    x2 = x * x
    hi = x2.astype(BF16)
    lo = (x2 - hi.astype(F32)).astype(BF16)
    g = gmat_ref[...]
    return (jnp.dot(hi, g, preferred_element_type=F32) + jnp.dot(lo, g, preferred_element_type=F32)) * inv_n


def _rope(y, cos_ref, sa_ref, sb_ref, half):
    w = y.shape[1]
    reps = w // cos_ref.shape[1]
    tile = lambda r: jnp.concatenate([r[...]] * reps, axis=1)
    return (y * tile(cos_ref) + pltpu.roll(y, w - half, axis=1) * tile(sa_ref)
            + pltpu.roll(y, half, axis=1) * tile(sb_ref))


def _q_post_kernel(x_ref, w_ref, gmat_ref, cos_ref, sa_ref, sb_ref, o_ref, *, hd, half, scale):
    gw = gmat_ref.shape[0]
    for c in range(x_ref.shape[1] // gw):
        cols = slice(c * gw, (c + 1) * gw)
        x = x_ref[:, cols]
        y = x * lax.rsqrt(_group_mean_sq(x, gmat_ref, 1.0 / hd) + EPS) * w_ref[...]
        o_ref[:, cols] = (_rope(y, cos_ref, sa_ref, sb_ref, half) * scale).astype(o_ref.dtype)


def _kv_post_kernel(k_ref, v_ref, w_ref, gmat_ref, cos_ref, sa_ref, sb_ref,
                    kf_ref, kb_ref, vbt_ref, kwin_ref, vwin_ref, *, hd, half, n_prompt_blocks, blocks_per_seq):
    i = pl.program_id(0)
    x = k_ref[...]
    y = x * lax.rsqrt(_group_mean_sq(x, gmat_ref, 1.0 / hd) + EPS) * w_ref[...]
    kf = _rope(y, cos_ref, sa_ref, sb_ref, half)
    kf_ref[...] = kf
    v = v_ref[...]
    vt = v.T
    for hh in range(kb_ref.shape[0]):
        kb_ref[hh] = kf[:, hh * hd:(hh + 1) * hd].astype(BF16)
        vbt_ref[hh] = vt[hh * hd:(hh + 1) * hd, :].astype(BF16)

    @pl.when((i < n_prompt_blocks) & (i % blocks_per_seq == blocks_per_seq - 1))
    def _():
        wb = kwin_ref.shape[0]
        kwin_ref[...] = kf[kf.shape[0] - wb:, :]
        vwin_ref[...] = v[v.shape[0] - wb:, :]


def _rope_tables(pos, hd, rot_dim):
    half = rot_dim // 2
    inv_freq = jnp.power(ROPE_THETA, -jnp.arange(half, dtype=F32) * 2.0 / rot_dim)
    ang = pos[:, None] * inv_freq[None, :]
    cos, sin = jnp.cos(ang), jnp.sin(ang)
    m = pos.shape[0]
    ones = jnp.ones((m, hd - rot_dim), F32)
    zeros = jnp.zeros((m, hd - rot_dim), F32)
    zh = jnp.zeros((m, half), F32)
    c = jnp.concatenate([cos, cos, ones], axis=1)
    sa = jnp.concatenate([-sin, zh, zeros], axis=1)
    sb = jnp.concatenate([zh, sin, zeros], axis=1)
    reps = LANES // hd
    return tuple(jnp.tile(a, (1, reps)) for a in (c, sa, sb))


def _group_matrix(width, hd):
    r = jnp.arange(width) // hd
    return (r[:, None] == r[None, :]).astype(BF16)


def _softmax_sink(s, mask, sink):
    s = jnp.where(mask, s, -jnp.inf)
    m = jnp.maximum(jnp.max(s, axis=-1, keepdims=True), sink)
    p = jnp.exp2(s - m)
    denom = jnp.sum(p, axis=-1, keepdims=True) + jnp.exp2(sink - m)
    return (p / denom).astype(BF16)


def _attn_prompt_kernel(sink_ref, q_ref, kp_ref, kc_ref, vp_ref, vc_ref, o_ref, *, kv_heads, group, hd, win):
    nb = pl.program_id(1)
    kj = lax.broadcasted_iota(jnp.int32, (2 * win, win), 0)
    qi = lax.broadcasted_iota(jnp.int32, (2 * win, win), 1)
    mask = (kj > qi) & (kj <= qi + win) & ((kj >= win) | (nb > 0))

    def scores(h):
        kk = jnp.concatenate([kp_ref[h], kc_ref[h]], axis=0)
        return [lax.dot_general(kk, q_ref[:, (h * group + g) * hd:(h * group + g + 1) * hd],
                                (((1,), (1,)), ((), ())), preferred_element_type=F32) for g in range(group)]

    def row_max(h, ss):
        out = []
        for g, s in enumerate(ss):
            s = jnp.where(mask, s, -jnp.inf)
            out.append((s, jnp.maximum(jnp.max(s, axis=0, keepdims=True), sink_ref[h * group + g] * LOG2E)))
        return out

    def probs(h, sm):
        out = []
        for g, (s, m) in enumerate(sm):
            p = jnp.exp2(s - m)
            denom = jnp.sum(p, axis=0, keepdims=True) + jnp.exp2(sink_ref[h * group + g] * LOG2E - m)
            out.append((p.astype(BF16), denom))
        return out

    def weighted_values(h, pd):
        vt = jnp.concatenate([vp_ref[h], vc_ref[h]], axis=1)
        ot = jnp.concatenate([jnp.dot(vt, p, preferred_element_type=F32) / denom for p, denom in pd], axis=0)
        o_ref[:, h * group * hd:(h + 1) * group * hd] = ot.T.astype(o_ref.dtype)

    stages = (scores, row_max, probs, weighted_values)
    carry = [None] * len(stages)
    for step in range(kv_heads + len(stages) - 1):
        for k in reversed(range(len(stages))):
            h = step - k
            if 0 <= h < kv_heads:
                carry[k] = stages[k](h) if k == 0 else stages[k](h, carry[k - 1])


def _attn_prompt(q, kb, vbt, sinks, *, batch, seq, kv_heads, group, hd, out_rows):
    win = WINDOW
    nblk = seq // win
    qw = kv_heads * group * hd
    k_spec = lambda back: pl.BlockSpec((kv_heads, win, hd),
                                       lambda b, n: (0, b * nblk + jnp.maximum(n - back, 0), 0))
    v_spec = lambda back: pl.BlockSpec((kv_heads, hd, win),
                                       lambda b, n: (0, 0, b * nblk + jnp.maximum(n - back, 0)))
    return pl.pallas_call(
        functools.partial(_attn_prompt_kernel, kv_heads=kv_heads, group=group, hd=hd, win=win),
        grid=(batch, nblk),
        in_specs=[pl.BlockSpec(memory_space=pltpu.SMEM),
                  pl.BlockSpec((win, qw), lambda b, n: (b * nblk + n, 0)),
                  k_spec(1), k_spec(0), v_spec(1), v_spec(0)],
        out_specs=pl.BlockSpec((win, qw), lambda b, n: (b * nblk + n, 0)),
        out_shape=jax.ShapeDtypeStruct((out_rows, qw), BF16),
        compiler_params=_cparams(("arbitrary", "arbitrary")),
        name="attn_prompt",
    )(sinks, q, kb, kb, vbt, vbt)


def _attn_sample_kernel(sink_ref, q_ref, kn_ref, vn_ref, kc_ref, vc_ref, prev_ref, o_ref,
                        *, nbs, tq, kv_heads, group, hd, win):
    del prev_ref
    nctx = 2 * win
    r = lax.broadcasted_iota(jnp.int32, (group * tq, nctx), 0)
    kj = lax.broadcasted_iota(jnp.int32, (group * tq, nctx), 1)
    tpos = r % tq
    mask = ((kj < win) & (kj > tpos)) | ((kj >= win) & (kj - win <= tpos))
    pad = jnp.zeros((nctx - win - tq, hd), F32)
    items = [(bi, h) for bi in range(nbs) for h in range(kv_heads)]
    qbs = [q_ref[bi * tq:(bi + 1) * tq, :].astype(F32) for bi in range(nbs)]
    sinks = [jnp.concatenate([jnp.full((tq, 1), sink_ref[h * group + g] * LOG2E, F32) for g in range(group)],
                             axis=0) for h in range(kv_heads)]
    ss, vs = [], []
    for bi, h in items:
        cs = slice(h * hd, (h + 1) * hd)
        trows = slice(bi * tq, (bi + 1) * tq)
        kctx = jnp.concatenate([kc_ref[bi][:, cs], kn_ref[trows, cs], pad], axis=0)
        vs.append(jnp.concatenate([vc_ref[bi][:, cs], vn_ref[trows, cs], pad], axis=0).astype(BF16))
        qg = jnp.concatenate([qbs[bi][:, (h * group + g) * hd:(h * group + g + 1) * hd]
                              for g in range(group)], axis=0)
        ss.append(_bdot_nt(qg, kctx))
    ps = [_softmax_sink(s, mask, sinks[h]) for (bi, h), s in zip(items, ss)]
    for (bi, h), p, v in zip(items, ps, vs):
        o = jnp.dot(p, v, preferred_element_type=F32)
        for g in range(group):
            hh = h * group + g
            o_ref[bi * tq:(bi + 1) * tq, hh * hd:(hh + 1) * hd] = o[g * tq:(g + 1) * tq, :].astype(o_ref.dtype)


def _attn_sample(q, kf, kv, cache_k, cache_v, sinks, o_prev, *, row0, batch, tq, kv_heads, group, hd, nbs):
    win = cache_k.shape[1]
    assert win == WINDOW and batch % nbs == 0 and row0 % (nbs * tq) == 0
    rows = nbs * tq
    rb0 = row0 // rows
    kvw = kv_heads * hd
    return pl.pallas_call(
        functools.partial(_attn_sample_kernel, nbs=nbs, tq=tq, kv_heads=kv_heads, group=group, hd=hd, win=win),
        grid=(batch // nbs,),
        in_specs=[pl.BlockSpec(memory_space=pltpu.SMEM),
                  pl.BlockSpec((rows, q.shape[1]), lambda b: (rb0 + b, 0)),
                  pl.BlockSpec((rows, kvw), lambda b: (rb0 + b, 0)),
                  pl.BlockSpec((rows, kvw), lambda b: (rb0 + b, 1)),
                  pl.BlockSpec((nbs, win, kvw), lambda b: (b, 0, 0)),
                  pl.BlockSpec((nbs, win, kvw), lambda b: (b, 0, 0)),
                  pl.BlockSpec(memory_space=pl.ANY)],
        out_specs=pl.BlockSpec((rows, q.shape[1]), lambda b: (rb0 + b, 0)),
        out_shape=jax.ShapeDtypeStruct(o_prev.shape, o_prev.dtype),
        input_output_aliases={6: 0},
        compiler_params=_cparams(("arbitrary",)),
        name="attn_sample",
    )(sinks, q, kf, kv, cache_k, cache_v, o_prev)


PAST_LEN = 16384
TM_NORM = 256
M_STEPS = 8


Q_POST_COLS = 2048
TN_WIDE = 1024
TN_RES = 512
TN_FFN = 256
DOWN_KBLOCKS = 2


def _write_rows_kernel(src_ref, dst_ref, o_ref):
    del dst_ref
    o_ref[...] = src_ref[...]


def _merge_rows(dst, src, row0, tm):
    n = src.shape[1]
    rows = src.shape[0] - row0
    assert rows % tm == 0 and row0 % tm == 0
    return pl.pallas_call(
        _write_rows_kernel,
        grid=(rows // tm,),
        in_specs=[pl.BlockSpec((tm, n), lambda i: (row0 // tm + i, 0)),
                  pl.BlockSpec(memory_space=pl.ANY)],
        out_specs=pl.BlockSpec((tm, n), lambda i: (row0 // tm + i, 0)),
        out_shape=jax.ShapeDtypeStruct(dst.shape, dst.dtype),
        input_output_aliases={1: 0},
        compiler_params=_cparams(("arbitrary",)),
        name="merge_rows",
    )(src, dst)


def kernel(x_prompt, x_sample, state_ssm, state_conv, cache_win_k, cache_win_v, norm_mix, norm_ffn,
           a_w_in, a_conv_w, a_log, a_dt_bias, a_o_norm, a_w_out, kv_norm, w_kv, k_norm, b_w_q,
           b_q_norm, b_sinks, b_w_o, ffn_w_gu, ffn_w_down):
    bp, sp, d = x_prompt.shape
    bs, ss, _ = x_sample.shape
    mp, ms = bp * sp, bs * ss
    m = mp + ms
    n_a = a_w_in.shape[0]
    assert n_a == 1 and b_w_q.shape[0] == 1, "layer pattern is one delta layer then one attention layer"
    h_a = a_log.shape[1]
    dk = a_o_norm.shape[1]
    d_qk = h_a * dk
    kv_heads, hd = cache_win_k.shape[2], cache_win_k.shape[3]
    kvw = kv_heads * hd
    h_b = b_sinks.shape[1]
    group = h_b // kv_heads
    rot_dim = hd // 4

    def mm(xb, w, n_out, res=None):
        tn = min(TN_WIDE if res is None else TN_RES, n_out)
        return _mm_stream(xb, w, 0, n_out=n_out, tn=tn, n_m=M_STEPS, res=res)

    def ffn(hin, hn, layer, split_streams=False):
        act = _ffn_up(hn, ffn_w_gu, layer, n_m=M_STEPS, tn=TN_FFN)
        out = hin
        for kb in range(DOWN_KBLOCKS):
            tail = ms if split_streams and kb == DOWN_KBLOCKS - 1 else 0
            out = _mm_stream(act, ffn_w_down, layer, n_out=d, tn=min(TN_RES, d), n_m=M_STEPS,
                             kblock=kb, n_kblocks=DOWN_KBLOCKS, res=out, tail=tail)
        return out

    xp2, xs2 = x_prompt.reshape(mp, d), x_sample.reshape(ms, d)
    (xn,) = _norm_inputs(xp2, xs2, norm_mix[0], TM_NORM)
    w_in_t = jnp.swapaxes(a_w_in, 1, 2)
    proj = _mm_stream(xn, w_in_t, 0, n_out=4 * d_qk, tn=TN_WIDE, n_m=M_STEPS, w_t=True)
    w_ba_t = w_in_t[0, 4 * d_qk:, :].astype(BF16)
    c_s = math.gcd(ss, DELTA_CHUNK)
    gates = _gates(xn, w_ba_t, a_log[0], a_dt_bias[0], tm=TM_NORM, n_prompt_rows=mp,
                   c_prompt=DELTA_CHUNK, c_sample=c_s)
    conv_w = a_conv_w[0]
    conv0_p = jnp.zeros((bp, CONV_W - 1, 3 * d_qk), F32)
    o_p, ssm_p, conv_p = _delta_rule(proj, gates[:, :, :mp], conv_w, conv0_p, None, a_o_norm[0],
                                     row0=0, batch=bp, seq=sp, n_heads=h_a, dk=dk,
                                     C=DELTA_CHUNK, nb=1, nc=4, hb=8, out_rows=m)
    o_s, ssm_s, conv_s = _delta_rule(proj, gates[:, :, mp:], conv_w, state_conv[0], state_ssm[0], a_o_norm[0],
                                     row0=mp, batch=bs, seq=ss, n_heads=h_a, dk=dk,
                                     C=c_s, nb=4, nc=1, hb=8, out_rows=m)
    o_a = _merge_rows(o_p, o_s, mp, TM_NORM)
    hn1, h1 = _norm_inputs(xp2, xs2, norm_ffn[0], TM_NORM, y=mm(o_a, a_w_out, d))
    h2 = ffn(h1, hn1, 0)

    hkv, hq = _norm(h2, jnp.stack([kv_norm, norm_mix[1]]), TM_NORM)
    kv = mm(hkv, w_kv[None], 2 * kvw)
    pos = jnp.concatenate([jnp.tile(jnp.arange(sp, dtype=F32), bp),
                           jnp.tile(PAST_LEN + jnp.arange(ss, dtype=F32), bs)])
    cos_t, sa_t, sb_t = _rope_tables(pos, hd, rot_dim)
    gmat = _group_matrix(kvw, hd)
    wb = min(WINDOW, sp)
    bps = sp // TM_NORM
    assert sp % TM_NORM == 0 and wb <= TM_NORM
    win_spec = pl.BlockSpec((wb, kvw), lambda i: (jnp.minimum(i // bps, bp - 1), 0))
    kf, kb, vbt, kwin_p, vwin_p = pl.pallas_call(
        functools.partial(_kv_post_kernel, hd=hd, half=rot_dim // 2, n_prompt_blocks=mp // TM_NORM,
                          blocks_per_seq=bps),
        grid=(m // TM_NORM,),
        in_specs=[pl.BlockSpec((TM_NORM, kvw), lambda i: (i, 0)),
                  pl.BlockSpec((TM_NORM, kvw), lambda i: (i, 1)),
                  pl.BlockSpec((1, kvw), lambda i: (0, 0)),
                  pl.BlockSpec((kvw, kvw), lambda i: (0, 0)),
                  *[pl.BlockSpec((TM_NORM, LANES), lambda i: (i, 0)) for _ in range(3)]],
        out_specs=[pl.BlockSpec((TM_NORM, kvw), lambda i: (i, 0)),
                   pl.BlockSpec((kv_heads, TM_NORM, hd), lambda i: (0, i, 0)),
                   pl.BlockSpec((kv_heads, hd, TM_NORM), lambda i: (0, 0, i)),
                   win_spec, win_spec],
        out_shape=[jax.ShapeDtypeStruct((m, kvw), F32),
                   jax.ShapeDtypeStruct((kv_heads, m, hd), BF16),
                   jax.ShapeDtypeStruct((kv_heads, hd, m), BF16),
                   jax.ShapeDtypeStruct((bp * wb, kvw), F32),
                   jax.ShapeDtypeStruct((bp * wb, kvw), F32)],
        compiler_params=_cparams(("arbitrary",)),
        name="kv_post",
    )(kv, kv, jnp.tile(k_norm, kv_heads).reshape(1, kvw), gmat, cos_t, sa_t, sb_t)

    qraw = mm(hq, b_w_q, h_b * hd)
    qw = kvw
    qcols = h_b * hd
    qblk = min(Q_POST_COLS, qcols)
    assert qcols % qblk == 0 and qblk % qw == 0
    q = pl.pallas_call(
        functools.partial(_q_post_kernel, hd=hd, half=rot_dim // 2, scale=hd ** -0.5 * LOG2E),
        grid=(m // TM_NORM, qcols // qblk),
        in_specs=[pl.BlockSpec((TM_NORM, qblk), lambda i, j: (i, j)),
                  pl.BlockSpec((1, qw), lambda i, j: (0, 0)),
                  pl.BlockSpec((qw, qw), lambda i, j: (0, 0)),
                  *[pl.BlockSpec((TM_NORM, LANES), lambda i, j: (i, 0)) for _ in range(3)]],
        out_specs=pl.BlockSpec((TM_NORM, qblk), lambda i, j: (i, j)),
        out_shape=jax.ShapeDtypeStruct((m, qcols), BF16),
        compiler_params=_cparams(("arbitrary", "arbitrary")),
        name="q_post",
    )(qraw, jnp.tile(b_q_norm[0], qw // hd).reshape(1, qw), gmat, cos_t, sa_t, sb_t)
    sinks = b_sinks[0]
    o_b = _attn_prompt(q, kb, vbt, sinks, batch=bp, seq=sp, kv_heads=kv_heads, group=group, hd=hd, out_rows=m)
    o_b = _attn_sample(q, kf, kv, cache_win_k.reshape(bs, WINDOW, kvw), cache_win_v.reshape(bs, WINDOW, kvw),
                       sinks, o_b, row0=mp, batch=bs, tq=ss, kv_heads=kv_heads, group=group, hd=hd, nbs=2)
    h3 = mm(o_b, b_w_o, d, res=h2)
    (hn3,) = _norm(h3, norm_ffn[1:2], TM_NORM)
    y_p, y_s = ffn(h3, hn3, 1, split_streams=True)

    kf_p = kwin_p.reshape(bp, wb, kv_heads, hd)
    vf_p = vwin_p.reshape(bp, wb, kv_heads, hd)
    kf_s = kf[mp:].reshape(bs, ss, kv_heads, hd)
    vf_s = kv[mp:, kvw:].reshape(bs, ss, kv_heads, hd)
    wk_s = jnp.concatenate([cache_win_k[:, ss:], kf_s], axis=1)
    wv_s = jnp.concatenate([cache_win_v[:, ss:], vf_s], axis=1)
    return (y_p.reshape(bp, sp, d), y_s.reshape(bs, ss, d),
            ssm_p[None], conv_p[None], kf_p, vf_p, ssm_s[None], conv_s[None], wk_s, wv_s)
```

```python
import functools
import itertools
import math

import jax
import jax.numpy as jnp
from jax import lax
from jax.experimental import pallas as pl
from jax.experimental.pallas import tpu as pltpu

F32 = jnp.float32
BF16 = jnp.bfloat16

EPS = 1e-6
LOG2E = math.log2(math.e)
WINDOW = 128
ROPE_THETA = 500000.0
CONV_W = 4
DELTA_CHUNK = 64
LANES = 128
CONV_PAD = 8
GROUP_CHUNKS = 2
VMEM_LIMIT = 56 * 1024 * 1024


def _cparams(sem):
    return pltpu.CompilerParams(dimension_semantics=sem, vmem_limit_bytes=VMEM_LIMIT)


def _sigmoid(x):
    return 1.0 / (1.0 + jnp.exp(-x))


def _bdot(a, b):
    return jnp.dot(a.astype(BF16), b.astype(BF16), preferred_element_type=F32)


def _bdot_nt(a, b):
    return lax.dot_general(a.astype(BF16), b.astype(BF16), (((1,), (1,)), ((), ())),
                           preferred_element_type=F32)


def _bdot_tn(a, b):
    return lax.dot_general(a.astype(BF16), b.astype(BF16), (((0,), (0,)), ((), ())),
                           preferred_element_type=F32)


def _norm_inputs_kernel(xp_ref, xs_ref, w_ref, xn_ref, *, n_prompt_blocks):
    i = pl.program_id(0)

    def emit(x):
        r = lax.rsqrt(jnp.mean(x * x, axis=-1, keepdims=True) + EPS)
        xn_ref[...] = (x * r * w_ref[...]).astype(xn_ref.dtype)

    @pl.when(i < n_prompt_blocks)
    def _():
        emit(xp_ref[...])

    @pl.when(i >= n_prompt_blocks)
    def _():
        emit(xs_ref[...])


def _norm_inputs(xp, xs, w, tm):
    mp, d = xp.shape
    ms = xs.shape[0]
    assert mp % tm == 0 and ms == tm
    npb = mp // tm
    return pl.pallas_call(
        functools.partial(_norm_inputs_kernel, n_prompt_blocks=npb),
        grid=(npb + 1,),
        in_specs=[pl.BlockSpec((tm, d), lambda i: (jnp.minimum(i, npb - 1), 0)),
                  pl.BlockSpec((tm, d), lambda i: (0, 0)),
                  pl.BlockSpec((1, d), lambda i: (0, 0))],
        out_specs=pl.BlockSpec((tm, d), lambda i: (i, 0)),
        out_shape=jax.ShapeDtypeStruct((mp + ms, d), BF16),
        compiler_params=_cparams(("arbitrary",)),
        name="norm_inputs",
    )(xp, xs, w.reshape(1, d))


def _norm_kernel(x_ref, w_ref, *o_refs):
    x = x_ref[...]
    y = x * lax.rsqrt(jnp.mean(x * x, axis=-1, keepdims=True) + EPS)
    for i, o_ref in enumerate(o_refs):
        o_ref[...] = (y * w_ref[i:i + 1, :]).astype(o_ref.dtype)


def _norm(x, ws, tm):
    m, d = x.shape
    nw = ws.shape[0]
    assert m % tm == 0
    return pl.pallas_call(
        _norm_kernel,
        grid=(m // tm,),
        in_specs=[pl.BlockSpec((tm, d), lambda i: (i, 0)),
                  pl.BlockSpec((nw, d), lambda i: (0, 0))],
        out_specs=[pl.BlockSpec((tm, d), lambda i: (i, 0)) for _ in range(nw)],
        out_shape=[jax.ShapeDtypeStruct((m, d), BF16) for _ in range(nw)],
        compiler_params=_cparams(("arbitrary",)),
        name="norm",
    )(x, ws)


def _first_rows(j, i):
    return jnp.where(j == 0, 0, i)


def _mm_stream_kernel(*refs, n_res, n_out_refs, w_t, n_m, tail):
    x_ref, w_ref, *mid, wbf_ref = refs
    res_refs, out_refs = mid[:n_res], mid[n_res:]
    assert len(out_refs) == n_out_refs
    j = pl.program_id(0)
    i = pl.program_id(1)
    rows_c = w_ref.shape[0]
    slot = j % 2
    ic = jnp.minimum(i, n_m - 1)
    wbf_ref[slot, pl.ds(pl.multiple_of(ic * rows_c, rows_c), rows_c), :] = w_ref[...].astype(BF16)

    def product(rows, res_ref):
        contract = (((1,), (1 if w_t else 0,)), ((), ()))
        acc = lax.dot_general(x_ref[:rows, :], wbf_ref[1 - slot], contract, preferred_element_type=F32)
        if res_ref is not None:
            acc = acc + res_ref[:rows, :]
        return acc

    @pl.when((j > 0) & (i < n_m))
    def _():
        out_refs[0][...] = product(x_ref.shape[0], res_refs[0] if n_res else None).astype(out_refs[0].dtype)

    if tail:
        @pl.when((j > 0) & (i == n_m))
        def _():
            acc = product(tail, res_refs[-1] if n_res else None)
            if n_out_refs == 2:
                out_refs[1][...] = acc.astype(out_refs[1].dtype)
            else:
                out_refs[0][:tail, :] = acc.astype(out_refs[0].dtype)


def _mm_stream(x, w, layer, *, n_out, tn, n_m, kblock=0, n_kblocks=1, res=None, out_dtype=F32, w_t=False,
               tail=0, split_out=True):
    m = x.shape[0]
    m_main = m - tail
    k = x.shape[1] // n_kblocks
    tm, nj = m_main // n_m, n_out // tn
    rows_c = (tn if w_t else k) // n_m
    assert m_main % n_m == 0 and tm % 16 == 0 and rows_c * n_m == (tn if w_t else k) and rows_c % 16 == 0
    assert n_out % tn == 0 and w.shape[2 if w_t else 1] == k * n_kblocks == x.shape[1] and k % LANES == 0
    assert 0 <= tail <= tm and tail % 8 == 0
    chunk = lambda i: jnp.minimum(i, n_m - 1)
    if w_t:
        w_spec = pl.BlockSpec((None, rows_c, k),
                              lambda j, i: (layer, jnp.minimum(j, nj - 1) * n_m + chunk(i), kblock))
    else:
        w_spec = pl.BlockSpec((None, rows_c, tn),
                              lambda j, i: (layer, kblock * n_m + chunk(i), jnp.minimum(j, nj - 1)))
    in_specs = [pl.BlockSpec((tm, k), lambda j, i: (_first_rows(j, i), kblock)), w_spec]
    args = [x, w]
    col = lambda j: jnp.maximum(j - 1, 0)
    whole = pl.BlockSpec((tm, tn), lambda j, i: (_first_rows(j, i), col(j)))
    main = pl.BlockSpec((tm, tn), lambda j, i: (jnp.minimum(_first_rows(j, i), n_m - 1), col(j)))
    last = pl.BlockSpec((tail, tn), lambda j, i: (0, col(j)))
    res_list = [] if res is None else list(res) if isinstance(res, tuple) else [res]
    in_specs += [main, last] if len(res_list) == 2 else [whole] * len(res_list)
    args += res_list
    if tail and split_out:
        out_specs = [main, last]
        out_shape = [jax.ShapeDtypeStruct((m_main, n_out), out_dtype), jax.ShapeDtypeStruct((tail, n_out), out_dtype)]
    else:
        out_specs = [whole]
        out_shape = [jax.ShapeDtypeStruct((m, n_out), out_dtype)]
    outs = pl.pallas_call(
        functools.partial(_mm_stream_kernel, n_res=len(res_list), n_out_refs=len(out_specs), w_t=w_t, n_m=n_m,
                          tail=tail),
        grid=(nj + 1, n_m + (1 if tail else 0)),
        in_specs=in_specs,
        out_specs=out_specs,
        out_shape=out_shape,
        scratch_shapes=[pltpu.VMEM((2, tn, k) if w_t else (2, k, tn), BF16)],
        compiler_params=_cparams(("arbitrary", "arbitrary")),
        name="mm_stream",
    )(*args)
    return tuple(outs) if len(outs) == 2 else outs[0]


FFN_PAIRS = 2


def _ffn_up_kernel(x_ref, *refs, tn):
    w_refs, (o_ref, wbf_ref) = refs[:2 * FFN_PAIRS], refs[2 * FFN_PAIRS:]
    j = pl.program_id(0)
    i = pl.program_id(1)
    rows_c = w_refs[0].shape[0]
    slot = j % 2
    rows = pl.ds(pl.multiple_of(i * rows_c, rows_c), rows_c)
    for c, w_ref in enumerate(w_refs):
        wbf_ref[slot, rows, c * tn:(c + 1) * tn] = w_ref[...].astype(BF16)

    @pl.when(j > 0)
    def _():
        gu = jnp.dot(x_ref[...], wbf_ref[1 - slot], preferred_element_type=F32)
        g = gu[:, :FFN_PAIRS * tn]
        o_ref[...] = (g * _sigmoid(g) * gu[:, FFN_PAIRS * tn:]).astype(o_ref.dtype)


def _ffn_up(x, w_gu, layer, *, n_m, tn):
    m, k = x.shape
    f = w_gu.shape[2] // 2
    tm, rows_c = m // n_m, k // n_m
    assert m % n_m == 0 and tm % 16 == 0 and k % n_m == 0 and rows_c % 16 == 0 and f % tn == 0
    n_pairs = f // tn
    nj = pl.cdiv(n_pairs, FFN_PAIRS)

    def w_map(c):
        g, part = c % FFN_PAIRS, c // FFN_PAIRS

        def index(j, i):
            pair = jnp.minimum(jnp.minimum(j, nj - 1) * FFN_PAIRS + g, n_pairs - 1)
            return layer, i, part * n_pairs + pair
        return index

    out_map = lambda j, i: (_first_rows(j, i), jnp.maximum(j - 1, 0))
    return pl.pallas_call(
        functools.partial(_ffn_up_kernel, tn=tn),
        grid=(nj + 1, n_m),
        in_specs=[pl.BlockSpec((tm, k), lambda j, i: (_first_rows(j, i), 0)),
                  *[pl.BlockSpec((None, rows_c, tn), w_map(c)) for c in range(2 * FFN_PAIRS)]],
        out_specs=pl.BlockSpec((tm, FFN_PAIRS * tn), out_map),
        out_shape=jax.ShapeDtypeStruct((m, f), BF16),
        scratch_shapes=[pltpu.VMEM((2, k, 2 * FFN_PAIRS * tn), BF16)],
        compiler_params=_cparams(("arbitrary", "arbitrary")),
        name="ffn_up",
    )(x, *([w_gu] * (2 * FFN_PAIRS)))


def _gates_kernel(x_ref, w_ref, alog_ref, dtb_ref, o_ref, *, n_heads, n_prompt_blocks, c_prompt, c_sample):
    tm = x_ref.shape[0]
    ba = lax.dot_general(w_ref[...], x_ref[...], (((1,), (1,)), ((), ())), preferred_element_type=F32)
    beta = _sigmoid(ba[:n_heads])
    a = ba[n_heads:] + dtb_ref[...]
    softplus = jnp.maximum(a, 0.0) + jnp.log(1.0 + jnp.exp(-jnp.abs(a)))
    g = -jnp.exp(alog_ref[...]) * softplus
    shift = jnp.where(pl.program_id(0) < n_prompt_blocks,
                      int(math.log2(c_prompt)), int(math.log2(c_sample)))
    jj = lax.broadcasted_iota(jnp.int32, (tm, tm), 0)
    ii = lax.broadcasted_iota(jnp.int32, (tm, tm), 1)
    same = lax.shift_right_logical(jj, shift) == lax.shift_right_logical(ii, shift)
    cum_m = jnp.where(same & (jj <= ii), 1.0, 0.0).astype(F32)
    tot_m = jnp.where(same, 1.0, 0.0).astype(F32)
    gcum = jnp.dot(g, cum_m, preferred_element_type=F32, precision=lax.Precision.HIGHEST)
    glast = jnp.dot(g, tot_m, preferred_element_type=F32, precision=lax.Precision.HIGHEST)
    o_ref[0] = beta
    o_ref[1] = jnp.exp(gcum)
    o_ref[2] = jnp.exp(glast - gcum)
    o_ref[3] = gcum


def _gates(xn, w_ba_t, a_log, dt_bias, *, tm, n_prompt_rows, c_prompt, c_sample):
    m, k = xn.shape
    h = a_log.shape[0]
    assert m % tm == 0 and n_prompt_rows % tm == 0 and tm % c_prompt == 0 and tm % c_sample == 0
    assert w_ba_t.shape == (2 * h, k)
    return pl.pallas_call(
        functools.partial(_gates_kernel, n_heads=h, n_prompt_blocks=n_prompt_rows // tm,
                          c_prompt=c_prompt, c_sample=c_sample),
        grid=(m // tm,),
        in_specs=[pl.BlockSpec((tm, k), lambda i: (i, 0)),
                  pl.BlockSpec((2 * h, k), lambda i: (0, 0)),
                  pl.BlockSpec((h, 1), lambda i: (0, 0)),
                  pl.BlockSpec((h, 1), lambda i: (0, 0))],
        out_specs=pl.BlockSpec((4, h, tm), lambda i: (0, 0, i)),
        out_shape=jax.ShapeDtypeStruct((4, h, m), F32),
        compiler_params=_cparams(("arbitrary",)),
        name="gates",
    )(xn, w_ba_t, a_log.reshape(h, 1), dt_bias.reshape(h, 1))


def _delta_kernel(q_ref, k_ref, v_ref, z_ref, cwq_ref, cwk_ref, cwv_ref, csq_ref, csk_ref, csv_ref,
                  gcol_ref, grow_ref, h0_ref, onw_ref,
                  o_ref, hout_ref, cqo_ref, cko_ref, cvo_ref,
                  h_scr, xq_scr, xk_scr, xv_scr, aq_scr, ak_scr, av_scr,
                  *, C, nb, nc, hb, dk, zero_init):
    t = pl.program_id(2)
    n_t = pl.num_programs(2)
    ttb = nc * C
    hist0 = CONV_PAD - (CONV_W - 1)

    @pl.when(t == 0)
    def _():
        if zero_init:
            h_scr[...] = jnp.zeros(h_scr.shape, F32)
        else:
            h_scr[...] = h0_ref[...]
        for xs, cs in ((xq_scr, csq_ref), (xk_scr, csk_ref), (xv_scr, csv_ref)):
            for bi in range(nb):
                xs[bi, hist0:CONV_PAD, :] = cs[bi]

    streams = ((q_ref, xq_scr, cwq_ref, aq_scr), (k_ref, xk_scr, cwk_ref, ak_scr), (v_ref, xv_scr, cwv_ref, av_scr))
    for raw_ref, xs, _, _ in streams:
        for bi in range(nb):
            xs[bi, CONV_PAD:CONV_PAD + ttb, :] = raw_ref[bi * ttb:(bi + 1) * ttb, :]

    def conv_phases(grp):
        r0, n = grp[0] * C, len(grp) * C

        def phase(xs, cw_ref, act):
            def run():
                for bi in range(nb):
                    win = xs[bi, r0:r0 + CONV_PAD + n, :]
                    acc = win[CONV_PAD:] * cw_ref[CONV_W - 1:CONV_W, :]
                    for s in range(1, CONV_W):
                        acc = acc + pltpu.roll(win, s, axis=0)[CONV_PAD:] * cw_ref[CONV_W - 1 - s:CONV_W - s, :]
                    act[bi * ttb + r0:bi * ttb + r0 + n, :] = acc * _sigmoid(acc)
            return run
        return [phase(xs, cw_ref, act) for _, xs, cw_ref, act in streams]

    ii = lax.broadcasted_iota(jnp.int32, (C, C), 0)
    jj = lax.broadcasted_iota(jnp.int32, (C, C), 1)
    n_levels = int(math.log2(C))
    assert 2 ** n_levels == C

    def prepare(bi, c, j):
        rows = slice(bi * ttb + c * C, bi * ttb + (c + 1) * C)
        cols = slice(j * dk, (j + 1) * dk)
        q = aq_scr[rows, cols]
        k = ak_scr[rows, cols]
        qn = q * (lax.rsqrt(jnp.sum(q * q, axis=-1, keepdims=True) + EPS) * (dk ** -0.5))
        kn = k * lax.rsqrt(jnp.sum(k * k, axis=-1, keepdims=True) + EPS)
        gcb = gcol_ref[0, rows, :]
        beta = gcb[:, j:j + 1]
        eg = gcb[:, hb + j:hb + j + 1]
        ekl = gcb[:, 2 * hb + j:2 * hb + j + 1]
        gc = gcb[:, 3 * hb + j:3 * hb + j + 1]
        gr = grow_ref[0, bi * nc + c, j:j + 1, :]
        dmat = jnp.where(ii >= jj, jnp.exp(gc - gr), 0.0)
        rhs = jnp.concatenate([kn * (beta * eg), av_scr[rows, cols] * beta], axis=1)
        return dict(rows=rows, cols=cols, qn=qn, kn=kn, beta=beta, eg=eg, dmat=dmat, rhs=rhs,
                    qe=qn * eg, kd=kn * ekl)

    st = {}

    def solve_phases(group):
        def scores():
            for it in group:
                e = st[it] = prepare(*it)
                e["s"] = _bdot_nt(jnp.concatenate([e["kn"], e["qn"]], axis=0), e["kn"])

        def square():
            for it in group:
                e = st[it]
                s = e.pop("s")
                e["qk"] = s[C:] * e["dmat"]
                e["t"] = jnp.where(ii > jj, s[:C] * e["dmat"] * (-e["beta"]), 0.0)
                e["p"] = _bdot(e["t"], e["t"])

        def level(last):
            def run():
                for it in group:
                    e = st[it]
                    e["tp"] = _bdot(e["t"], e["p"])
                    if not last:
                        e["p2"] = _bdot(e["p"], e["p"])
                for it in group:
                    e = st[it]
                    e["t"] = e["t"] + e["p"] + e.pop("tp")
                    e["p"] = None if last else e.pop("p2")
            return run

        def apply_t():
            for it in group:
                e = st[it]
                e["wu"] = e["rhs"] + _bdot(e["t"], e["rhs"])

        return [scores, square, *[level(lvl == n_levels - 2) for lvl in range(n_levels - 1)], apply_t]

    h_cur = {(bi, j): h_scr[bi, j] for bi in range(nb) for j in range(hb)}

    def chain_phases(c):
        sel = [(bi, c, j) for bi in range(nb) for j in range(hb)]

        def through_state():
            for it in sel:
                e = st[it]
                e["wq"] = _bdot(jnp.concatenate([e["wu"][:, :dk], e["qe"]], axis=0), h_cur[it[0], it[2]])

        def update():
            for it in sel:
                e = st[it]
                e["u"] = e["wu"][:, dk:] - e["wq"][:C]
            for it in sel:
                e = st[it]
                e["o"] = e["wq"][C:] + _bdot(e["qk"], e["u"])
                egl = e["eg"][C - 1:C, :]
                h_cur[it[0], it[2]] = h_cur[it[0], it[2]] * egl + _bdot_tn(e["kd"], e["u"])

        def emit():
            for it in sel:
                e = st.pop(it)
                o = e["o"]
                on = o * lax.rsqrt(jnp.mean(o * o, axis=-1, keepdims=True) + EPS) * onw_ref[...]
                zz = z_ref[e["rows"], e["cols"]]
                o_ref[e["rows"], e["cols"]] = (on * (zz * _sigmoid(zz))).astype(o_ref.dtype)

        return [through_state, update, emit]

    chunk_groups = [list(range(c0, min(c0 + GROUP_CHUNKS, nc))) for c0 in range(0, nc, GROUP_CHUNKS)]
    n_grp = len(chunk_groups)
    convs = [conv_phases(grp) for grp in chunk_groups]
    solves = [solve_phases([(bi, c, j) for bi in range(nb) for c in grp for j in range(hb)])
              for grp in chunk_groups]
    chains = [[phase for c in grp for phase in chain_phases(c)] for grp in chunk_groups]
    for g in range(-2, n_grp):
        tracks = [convs[g + 2] if g + 2 < n_grp else [],
                  solves[g + 1] if 0 <= g + 1 < n_grp else [],
                  chains[g] if g >= 0 else []]
        for phases in itertools.zip_longest(*tracks):
            for phase in phases:
                if phase is not None:
                    phase()
    for (bi, j), hv in h_cur.items():
        h_scr[bi, j] = hv
    for _, xs, _, _ in streams:
        for bi in range(nb):
            xs[bi, hist0:CONV_PAD, :] = xs[bi, CONV_PAD + ttb - (CONV_W - 1):CONV_PAD + ttb, :]

    @pl.when(t == n_t - 1)
    def _():
        for (_, xs, _, _), co in zip(streams, (cqo_ref, cko_ref, cvo_ref)):
            for bi in range(nb):
                co[bi] = xs[bi, hist0:CONV_PAD, :]

    @pl.when(t == n_t - 1)
    def _():
        hout_ref[...] = h_scr[...]


def _delta_rule(proj, gates, conv_w, conv0, h0, o_norm_w, *, row0, batch, seq, n_heads, dk,
                C, nb, nc, hb, out_rows):
    ttb = nc * C
    tt = nb * ttb
    w = hb * dk
    ng = n_heads // hb
    n_t = seq // ttb
    assert seq % ttb == 0 and batch % nb == 0 and n_heads % hb == 0 and row0 % tt == 0
    zero_init = h0 is None
    d_qk = n_heads * dk
    cb = d_qk // w
    rb0 = row0 // tt

    rows = batch * seq
    g5 = gates.reshape(4, ng, hb, rows)
    gcol = jnp.transpose(g5, (1, 3, 0, 2)).reshape(ng, rows, 4 * hb)
    grow = jnp.transpose(g5[3].reshape(ng, hb, rows // C, C), (0, 2, 1, 3))

    def tok_map(part):
        return lambda b, g, t: (rb0 + b * n_t + t, part * cb + g)

    if zero_init:
        h0_arg = jnp.zeros((nb, hb, dk, dk), F32)
        h0_spec = pl.BlockSpec((nb, hb, dk, dk), lambda b, g, t: (0, 0, 0, 0))
    else:
        h0_arg = h0
        h0_spec = pl.BlockSpec((nb, hb, dk, dk), lambda b, g, t: (b, g, 0, 0))
    cs_spec = [pl.BlockSpec((nb, CONV_W - 1, w), (lambda b, g, t, p=p: (b, 0, p * cb + g))) for p in range(3)]
    cw_spec = [pl.BlockSpec((CONV_W, w), (lambda b, g, t, p=p: (0, p * cb + g))) for p in range(3)]
    kern = functools.partial(_delta_kernel, C=C, nb=nb, nc=nc, hb=hb, dk=dk, zero_init=zero_init)
    o, h_fin, cq, ck, cv = pl.pallas_call(
        kern,
        grid=(batch // nb, ng, n_t),
        in_specs=[pl.BlockSpec((tt, w), tok_map(0)), pl.BlockSpec((tt, w), tok_map(1)),
                  pl.BlockSpec((tt, w), tok_map(2)), pl.BlockSpec((tt, w), tok_map(3)),
                  *cw_spec, *cs_spec,
                  pl.BlockSpec((1, tt, 4 * hb), lambda b, g, t: (g, b * n_t + t, 0)),
                  pl.BlockSpec((1, nb * nc, hb, C), lambda b, g, t: (g, b * n_t + t, 0, 0)),
                  h0_spec,
                  pl.BlockSpec((1, dk), lambda b, g, t: (0, 0))],
        out_specs=[pl.BlockSpec((tt, w), lambda b, g, t: (rb0 + b * n_t + t, g)),
                   pl.BlockSpec((nb, hb, dk, dk), lambda b, g, t: (b, g, 0, 0)),
                   *[pl.BlockSpec((nb, CONV_W - 1, w), lambda b, g, t: (b, 0, g)) for _ in range(3)]],
        out_shape=[jax.ShapeDtypeStruct((out_rows, d_qk), BF16),
                   jax.ShapeDtypeStruct((batch, n_heads, dk, dk), F32),
                   *[jax.ShapeDtypeStruct((batch, CONV_W - 1, d_qk), F32) for _ in range(3)]],
        scratch_shapes=[pltpu.VMEM((nb, hb, dk, dk), F32),
                        *[pltpu.VMEM((nb, CONV_PAD + ttb, w), F32) for _ in range(3)],
                        *[pltpu.VMEM((tt, w), F32) for _ in range(3)]],
        compiler_params=_cparams(("arbitrary", "arbitrary", "arbitrary")),
        name="delta_rule",
    )(proj, proj, proj, proj, conv_w, conv_w, conv_w, conv0, conv0, conv0, gcol, grow, h0_arg,
      o_norm_w.reshape(1, dk))
    return o, h_fin, jnp.concatenate([cq, ck, cv], axis=-1)


def _group_mean_sq(x, gmat_ref, inv_n):
    x2 = x * x
    hi = x2.astype(BF16)
    lo = (x2 - hi.astype(F32)).astype(BF16)
    g = gmat_ref[...]
    return (jnp.dot(hi, g, preferred_element_type=F32) + jnp.dot(lo, g, preferred_element_type=F32)) * inv_n


def _rope(y, cos_ref, sa_ref, sb_ref, half):
    w = y.shape[1]
    reps = w // cos_ref.shape[1]
    tile = lambda r: jnp.concatenate([r[...]] * reps, axis=1)
    return (y * tile(cos_ref) + pltpu.roll(y, w - half, axis=1) * tile(sa_ref)
            + pltpu.roll(y, half, axis=1) * tile(sb_ref))


def _q_post_kernel(x_ref, w_ref, gmat_ref, cos_ref, sa_ref, sb_ref, o_ref, *, hd, half, scale):
    gw = gmat_ref.shape[0]
    for c in range(x_ref.shape[1] // gw):
        cols = slice(c * gw, (c + 1) * gw)
        x = x_ref[:, cols]
        y = x * lax.rsqrt(_group_mean_sq(x, gmat_ref, 1.0 / hd) + EPS) * w_ref[...]
        o_ref[:, cols] = (_rope(y, cos_ref, sa_ref, sb_ref, half) * scale).astype(o_ref.dtype)


def _kv_post_kernel(k_ref, v_ref, w_ref, gmat_ref, cos_ref, sa_ref, sb_ref,
                    kf_ref, kb_ref, vbt_ref, kwin_ref, vwin_ref, *, hd, half, n_prompt_blocks, blocks_per_seq):
    i = pl.program_id(0)
    x = k_ref[...]
    y = x * lax.rsqrt(_group_mean_sq(x, gmat_ref, 1.0 / hd) + EPS) * w_ref[...]
    kf = _rope(y, cos_ref, sa_ref, sb_ref, half)
    kf_ref[...] = kf
    v = v_ref[...]
    vt = v.T
    for hh in range(kb_ref.shape[0]):
        kb_ref[hh] = kf[:, hh * hd:(hh + 1) * hd].astype(BF16)
        vbt_ref[hh] = vt[hh * hd:(hh + 1) * hd, :].astype(BF16)

    @pl.when((i < n_prompt_blocks) & (i % blocks_per_seq == blocks_per_seq - 1))
    def _():
        wb = kwin_ref.shape[0]
        kwin_ref[...] = kf[kf.shape[0] - wb:, :]
        vwin_ref[...] = v[v.shape[0] - wb:, :]


def _rope_tables(pos, hd, rot_dim):
    half = rot_dim // 2
    inv_freq = jnp.power(ROPE_THETA, -jnp.arange(half, dtype=F32) * 2.0 / rot_dim)
    ang = pos[:, None] * inv_freq[None, :]
    cos, sin = jnp.cos(ang), jnp.sin(ang)
    m = pos.shape[0]
    ones = jnp.ones((m, hd - rot_dim), F32)
    zeros = jnp.zeros((m, hd - rot_dim), F32)
    zh = jnp.zeros((m, half), F32)
    c = jnp.concatenate([cos, cos, ones], axis=1)
    sa = jnp.concatenate([-sin, zh, zeros], axis=1)
    sb = jnp.concatenate([zh, sin, zeros], axis=1)
    reps = LANES // hd
    return tuple(jnp.tile(a, (1, reps)) for a in (c, sa, sb))


def _group_matrix(width, hd):
    r = jnp.arange(width) // hd
    return (r[:, None] == r[None, :]).astype(BF16)


def _softmax_sink(s, mask, sink):
    s = jnp.where(mask, s, -jnp.inf)
    m = jnp.maximum(jnp.max(s, axis=-1, keepdims=True), sink)
    p = jnp.exp2(s - m)
    denom = jnp.sum(p, axis=-1, keepdims=True) + jnp.exp2(sink - m)
    return (p / denom).astype(BF16)


def _attn_prompt_kernel(sink_ref, q_ref, kp_ref, kc_ref, vp_ref, vc_ref, o_ref, *, kv_heads, group, hd, win):
    nb = pl.program_id(1)
    kj = lax.broadcasted_iota(jnp.int32, (2 * win, win), 0)
    qi = lax.broadcasted_iota(jnp.int32, (2 * win, win), 1)
    mask = (kj > qi) & (kj <= qi + win) & ((kj >= win) | (nb > 0))

    def scores(h):
        kk = jnp.concatenate([kp_ref[h], kc_ref[h]], axis=0)
        return [lax.dot_general(kk, q_ref[:, (h * group + g) * hd:(h * group + g + 1) * hd],
                                (((1,), (1,)), ((), ())), preferred_element_type=F32) for g in range(group)]

    def row_max(h, ss):
        out = []
        for g, s in enumerate(ss):
            s = jnp.where(mask, s, -jnp.inf)
            out.append((s, jnp.maximum(jnp.max(s, axis=0, keepdims=True), sink_ref[h * group + g] * LOG2E)))
        return out

    def probs(h, sm):
        out = []
        for g, (s, m) in enumerate(sm):
            p = jnp.exp2(s - m)
            denom = jnp.sum(p, axis=0, keepdims=True) + jnp.exp2(sink_ref[h * group + g] * LOG2E - m)
            out.append((p.astype(BF16), denom))
        return out

    def weighted_values(h, pd):
        vt = jnp.concatenate([vp_ref[h], vc_ref[h]], axis=1)
        ot = jnp.concatenate([jnp.dot(vt, p, preferred_element_type=F32) / denom for p, denom in pd], axis=0)
        o_ref[:, h * group * hd:(h + 1) * group * hd] = ot.T.astype(o_ref.dtype)

    stages = (scores, row_max, probs, weighted_values)
    carry = [None] * len(stages)
    for step in range(kv_heads + len(stages) - 1):
        for k in reversed(range(len(stages))):
            h = step - k
            if 0 <= h < kv_heads:
                carry[k] = stages[k](h) if k == 0 else stages[k](h, carry[k - 1])


def _attn_prompt(q, kb, vbt, sinks, *, batch, seq, kv_heads, group, hd, out_rows):
    win = WINDOW
    nblk = seq // win
    qw = kv_heads * group * hd
    k_spec = lambda back: pl.BlockSpec((kv_heads, win, hd),
                                       lambda b, n: (0, b * nblk + jnp.maximum(n - back, 0), 0))
    v_spec = lambda back: pl.BlockSpec((kv_heads, hd, win),
                                       lambda b, n: (0, 0, b * nblk + jnp.maximum(n - back, 0)))
    return pl.pallas_call(
        functools.partial(_attn_prompt_kernel, kv_heads=kv_heads, group=group, hd=hd, win=win),
        grid=(batch, nblk),
        in_specs=[pl.BlockSpec(memory_space=pltpu.SMEM),
                  pl.BlockSpec((win, qw), lambda b, n: (b * nblk + n, 0)),
                  k_spec(1), k_spec(0), v_spec(1), v_spec(0)],
        out_specs=pl.BlockSpec((win, qw), lambda b, n: (b * nblk + n, 0)),
        out_shape=jax.ShapeDtypeStruct((out_rows, qw), BF16),
        compiler_params=_cparams(("arbitrary", "arbitrary")),
        name="attn_prompt",
    )(sinks, q, kb, kb, vbt, vbt)


def _attn_sample_kernel(sink_ref, q_ref, kn_ref, vn_ref, kc_ref, vc_ref, prev_ref, o_ref,
                        *, nbs, tq, kv_heads, group, hd, win):
    del prev_ref
    nctx = 2 * win
    r = lax.broadcasted_iota(jnp.int32, (group * tq, nctx), 0)
    kj = lax.broadcasted_iota(jnp.int32, (group * tq, nctx), 1)
    tpos = r % tq
    mask = ((kj < win) & (kj > tpos)) | ((kj >= win) & (kj - win <= tpos))
    pad = jnp.zeros((nctx - win - tq, hd), F32)
    items = [(bi, h) for bi in range(nbs) for h in range(kv_heads)]
    qbs = [q_ref[bi * tq:(bi + 1) * tq, :].astype(F32) for bi in range(nbs)]
    sinks = [jnp.concatenate([jnp.full((tq, 1), sink_ref[h * group + g] * LOG2E, F32) for g in range(group)],
                             axis=0) for h in range(kv_heads)]
    ss, vs = [], []
    for bi, h in items:
        cs = slice(h * hd, (h + 1) * hd)
        trows = slice(bi * tq, (bi + 1) * tq)
        kctx = jnp.concatenate([kc_ref[bi][:, cs], kn_ref[trows, cs], pad], axis=0)
        vs.append(jnp.concatenate([vc_ref[bi][:, cs], vn_ref[trows, cs], pad], axis=0).astype(BF16))
        qg = jnp.concatenate([qbs[bi][:, (h * group + g) * hd:(h * group + g + 1) * hd]
                              for g in range(group)], axis=0)
        ss.append(_bdot_nt(qg, kctx))
    ps = [_softmax_sink(s, mask, sinks[h]) for (bi, h), s in zip(items, ss)]
    for (bi, h), p, v in zip(items, ps, vs):
        o = jnp.dot(p, v, preferred_element_type=F32)
        for g in range(group):
            hh = h * group + g
            o_ref[bi * tq:(bi + 1) * tq, hh * hd:(hh + 1) * hd] = o[g * tq:(g + 1) * tq, :].astype(o_ref.dtype)


def _attn_sample(q, kf, kv, cache_k, cache_v, sinks, o_prev, *, row0, batch, tq, kv_heads, group, hd, nbs):
    win = cache_k.shape[1]
    assert win == WINDOW and batch % nbs == 0 and row0 % (nbs * tq) == 0
    rows = nbs * tq
    rb0 = row0 // rows
    kvw = kv_heads * hd
    return pl.pallas_call(
        functools.partial(_attn_sample_kernel, nbs=nbs, tq=tq, kv_heads=kv_heads, group=group, hd=hd, win=win),
        grid=(batch // nbs,),
        in_specs=[pl.BlockSpec(memory_space=pltpu.SMEM),
                  pl.BlockSpec((rows, q.shape[1]), lambda b: (rb0 + b, 0)),
                  pl.BlockSpec((rows, kvw), lambda b: (rb0 + b, 0)),
                  pl.BlockSpec((rows, kvw), lambda b: (rb0 + b, 1)),
                  pl.BlockSpec((nbs, win, kvw), lambda b: (b, 0, 0)),
                  pl.BlockSpec((nbs, win, kvw), lambda b: (b, 0, 0)),
                  pl.BlockSpec(memory_space=pl.ANY)],
        out_specs=pl.BlockSpec((rows, q.shape[1]), lambda b: (rb0 + b, 0)),
        out_shape=jax.ShapeDtypeStruct(o_prev.shape, o_prev.dtype),
        input_output_aliases={6: 0},
        compiler_params=_cparams(("arbitrary",)),
        name="attn_sample",
    )(sinks, q, kf, kv, cache_k, cache_v, o_prev)


PAST_LEN = 16384
TM_NORM = 256
M_STEPS = 8


Q_POST_COLS = 2048
TN_WIDE = 1024
TN_RES = 512
TN_FFN = 256
DOWN_KBLOCKS = 2


def _write_rows_kernel(src_ref, dst_ref, o_ref):
    del dst_ref
    o_ref[...] = src_ref[...]


def _merge_rows(dst, src, row0, tm):
    n = src.shape[1]
    rows = src.shape[0] - row0
    assert rows % tm == 0 and row0 % tm == 0
    return pl.pallas_call(
        _write_rows_kernel,
        grid=(rows // tm,),
        in_specs=[pl.BlockSpec((tm, n), lambda i: (row0 // tm + i, 0)),
                  pl.BlockSpec(memory_space=pl.ANY)],
        out_specs=pl.BlockSpec((tm, n), lambda i: (row0 // tm + i, 0)),
        out_shape=jax.ShapeDtypeStruct(dst.shape, dst.dtype),
        input_output_aliases={1: 0},
        compiler_params=_cparams(("arbitrary",)),
        name="merge_rows",
    )(src, dst)


def kernel(x_prompt, x_sample, state_ssm, state_conv, cache_win_k, cache_win_v, norm_mix, norm_ffn,
           a_w_in, a_conv_w, a_log, a_dt_bias, a_o_norm, a_w_out, kv_norm, w_kv, k_norm, b_w_q,
           b_q_norm, b_sinks, b_w_o, ffn_w_gu, ffn_w_down):
    bp, sp, d = x_prompt.shape
    bs, ss, _ = x_sample.shape
    mp, ms = bp * sp, bs * ss
    m = mp + ms
    n_a = a_w_in.shape[0]
    assert n_a == 1 and b_w_q.shape[0] == 1, "layer pattern is one delta layer then one attention layer"
    h_a = a_log.shape[1]
    dk = a_o_norm.shape[1]
    d_qk = h_a * dk
    kv_heads, hd = cache_win_k.shape[2], cache_win_k.shape[3]
    kvw = kv_heads * hd
    h_b = b_sinks.shape[1]
    group = h_b // kv_heads
    rot_dim = hd // 4

    def mm(xb, w, n_out, res=None):
        tn = min(TN_WIDE if res is None else TN_RES, n_out)
        return _mm_stream(xb, w, 0, n_out=n_out, tn=tn, n_m=M_STEPS, res=res)

    def ffn(hin, hn, layer, split_streams=False):
        act = _ffn_up(hn, ffn_w_gu, layer, n_m=M_STEPS, tn=TN_FFN)
        out = hin
        for kb in range(DOWN_KBLOCKS):
            tail = ms if split_streams and kb == DOWN_KBLOCKS - 1 else 0
            out = _mm_stream(act, ffn_w_down, layer, n_out=d, tn=min(TN_RES, d), n_m=M_STEPS,
                             kblock=kb, n_kblocks=DOWN_KBLOCKS, res=out, tail=tail)
        return out

    xp2, xs2 = x_prompt.reshape(mp, d), x_sample.reshape(ms, d)
    xn = _norm_inputs(xp2, xs2, norm_mix[0], TM_NORM)
    w_in_t = jnp.swapaxes(a_w_in, 1, 2)
    proj = _mm_stream(xn, w_in_t, 0, n_out=4 * d_qk, tn=TN_WIDE, n_m=M_STEPS, w_t=True)
    w_ba_t = w_in_t[0, 4 * d_qk:, :].astype(BF16)
    c_s = math.gcd(ss, DELTA_CHUNK)
    gates = _gates(xn, w_ba_t, a_log[0], a_dt_bias[0], tm=TM_NORM, n_prompt_rows=mp,
                   c_prompt=DELTA_CHUNK, c_sample=c_s)
    conv_w = a_conv_w[0]
    conv0_p = jnp.zeros((bp, CONV_W - 1, 3 * d_qk), F32)
    o_p, ssm_p, conv_p = _delta_rule(proj, gates[:, :, :mp], conv_w, conv0_p, None, a_o_norm[0],
                                     row0=0, batch=bp, seq=sp, n_heads=h_a, dk=dk,
                                     C=DELTA_CHUNK, nb=1, nc=4, hb=8, out_rows=m)
    o_s, ssm_s, conv_s = _delta_rule(proj, gates[:, :, mp:], conv_w, state_conv[0], state_ssm[0], a_o_norm[0],
                                     row0=mp, batch=bs, seq=ss, n_heads=h_a, dk=dk,
                                     C=c_s, nb=4, nc=1, hb=8, out_rows=m)
    o_a = _merge_rows(o_p, o_s, mp, TM_NORM)
    h1 = _mm_stream(o_a, a_w_out, 0, n_out=d, tn=min(TN_RES, d), n_m=M_STEPS, res=(xp2, xs2), tail=ms,
                    split_out=False)
    (hn1,) = _norm(h1, norm_ffn[0:1], TM_NORM)
    h2 = ffn(h1, hn1, 0)

    hkv, hq = _norm(h2, jnp.stack([kv_norm, norm_mix[1]]), TM_NORM)
    kv = mm(hkv, w_kv[None], 2 * kvw)
    pos = jnp.concatenate([jnp.tile(jnp.arange(sp, dtype=F32), bp),
                           jnp.tile(PAST_LEN + jnp.arange(ss, dtype=F32), bs)])
    cos_t, sa_t, sb_t = _rope_tables(pos, hd, rot_dim)
    gmat = _group_matrix(kvw, hd)
    wb = min(WINDOW, sp)
    bps = sp // TM_NORM
    assert sp % TM_NORM == 0 and wb <= TM_NORM
    win_spec = pl.BlockSpec((wb, kvw), lambda i: (jnp.minimum(i // bps, bp - 1), 0))
    kf, kb, vbt, kwin_p, vwin_p = pl.pallas_call(
        functools.partial(_kv_post_kernel, hd=hd, half=rot_dim // 2, n_prompt_blocks=mp // TM_NORM,
                          blocks_per_seq=bps),
        grid=(m // TM_NORM,),
        in_specs=[pl.BlockSpec((TM_NORM, kvw), lambda i: (i, 0)),
                  pl.BlockSpec((TM_NORM, kvw), lambda i: (i, 1)),
                  pl.BlockSpec((1, kvw), lambda i: (0, 0)),
                  pl.BlockSpec((kvw, kvw), lambda i: (0, 0)),
                  *[pl.BlockSpec((TM_NORM, LANES), lambda i: (i, 0)) for _ in range(3)]],
        out_specs=[pl.BlockSpec((TM_NORM, kvw), lambda i: (i, 0)),
                   pl.BlockSpec((kv_heads, TM_NORM, hd), lambda i: (0, i, 0)),
                   pl.BlockSpec((kv_heads, hd, TM_NORM), lambda i: (0, 0, i)),
                   win_spec, win_spec],
        out_shape=[jax.ShapeDtypeStruct((m, kvw), F32),
                   jax.ShapeDtypeStruct((kv_heads, m, hd), BF16),
                   jax.ShapeDtypeStruct((kv_heads, hd, m), BF16),
                   jax.ShapeDtypeStruct((bp * wb, kvw), F32),
                   jax.ShapeDtypeStruct((bp * wb, kvw), F32)],
        compiler_params=_cparams(("arbitrary",)),
        name="kv_post",
    )(kv, kv, jnp.tile(k_norm, kv_heads).reshape(1, kvw), gmat, cos_t, sa_t, sb_t)

    qraw = mm(hq, b_w_q, h_b * hd)
    qw = kvw
    qcols = h_b * hd
    qblk = min(Q_POST_COLS, qcols)
    assert qcols % qblk == 0 and qblk % qw == 0
    q = pl.pallas_call(
        functools.partial(_q_post_kernel, hd=hd, half=rot_dim // 2, scale=hd ** -0.5 * LOG2E),
        grid=(m // TM_NORM, qcols // qblk),
        in_specs=[pl.BlockSpec((TM_NORM, qblk), lambda i, j: (i, j)),
                  pl.BlockSpec((1, qw), lambda i, j: (0, 0)),
                  pl.BlockSpec((qw, qw), lambda i, j: (0, 0)),
                  *[pl.BlockSpec((TM_NORM, LANES), lambda i, j: (i, 0)) for _ in range(3)]],
        out_specs=pl.BlockSpec((TM_NORM, qblk), lambda i, j: (i, j)),
        out_shape=jax.ShapeDtypeStruct((m, qcols), BF16),
        compiler_params=_cparams(("arbitrary", "arbitrary")),
        name="q_post",
    )(qraw, jnp.tile(b_q_norm[0], qw // hd).reshape(1, qw), gmat, cos_t, sa_t, sb_t)
    sinks = b_sinks[0]
    o_b = _attn_prompt(q, kb, vbt, sinks, batch=bp, seq=sp, kv_heads=kv_heads, group=group, hd=hd, out_rows=m)
    o_b = _attn_sample(q, kf, kv, cache_win_k.reshape(bs, WINDOW, kvw), cache_win_v.reshape(bs, WINDOW, kvw),
                       sinks, o_b, row0=mp, batch=bs, tq=ss, kv_heads=kv_heads, group=group, hd=hd, nbs=2)
    h3 = mm(o_b, b_w_o, d, res=h2)
    (hn3,) = _norm(h3, norm_ffn[1:2], TM_NORM)
    y_p, y_s = ffn(h3, hn3, 1, split_streams=True)

    kf_p = kwin_p.reshape(bp, wb, kv_heads, hd)
    vf_p = vwin_p.reshape(bp, wb, kv_heads, hd)
    kf_s = kf[mp:].reshape(bs, ss, kv_heads, hd)
    vf_s = kv[mp:, kvw:].reshape(bs, ss, kv_heads, hd)
    wk_s = jnp.concatenate([cache_win_k[:, ss:], kf_s], axis=1)
    wv_s = jnp.concatenate([cache_win_v[:, ss:], vf_s], axis=1)
    return (y_p.reshape(bp, sp, d), y_s.reshape(bs, ss, d),
            ssm_p[None], conv_p[None], kf_p, vf_p, ssm_s[None], conv_s[None], wk_s, wv_s)
```

```python
import functools
import itertools
import math

import jax
import jax.numpy as jnp
from jax import lax
from jax.experimental import pallas as pl
from jax.experimental.pallas import tpu as pltpu

F32 = jnp.float32
BF16 = jnp.bfloat16

EPS = 1e-6
LOG2E = math.log2(math.e)
WINDOW = 128
ROPE_THETA = 500000.0
CONV_W = 4
DELTA_CHUNK = 64
LANES = 128
CONV_PAD = 8
GROUP_CHUNKS = 2
VMEM_LIMIT = 56 * 1024 * 1024


def _cparams(sem):
    return pltpu.CompilerParams(dimension_semantics=sem, vmem_limit_bytes=VMEM_LIMIT)


def _sigmoid(x):
    return 1.0 / (1.0 + jnp.exp(-x))


def _bdot(a, b):
    return jnp.dot(a.astype(BF16), b.astype(BF16), preferred_element_type=F32)


def _bdot_nt(a, b):
    return lax.dot_general(a.astype(BF16), b.astype(BF16), (((1,), (1,)), ((), ())),
                           preferred_element_type=F32)


def _bdot_tn(a, b):
    return lax.dot_general(a.astype(BF16), b.astype(BF16), (((0,), (0,)), ((), ())),
                           preferred_element_type=F32)


def _norm_inputs_kernel(xp_ref, xs_ref, w_ref, xn_ref, *, n_prompt_blocks):
    i = pl.program_id(0)

    def emit(x):
        r = lax.rsqrt(jnp.mean(x * x, axis=-1, keepdims=True) + EPS)
        xn_ref[...] = (x * r * w_ref[...]).astype(xn_ref.dtype)

    @pl.when(i < n_prompt_blocks)
    def _():
        emit(xp_ref[...])

    @pl.when(i >= n_prompt_blocks)
    def _():
        emit(xs_ref[...])


def _norm_inputs(xp, xs, w, tm):
    mp, d = xp.shape
    ms = xs.shape[0]
    assert mp % tm == 0 and ms == tm
    npb = mp // tm
    return pl.pallas_call(
        functools.partial(_norm_inputs_kernel, n_prompt_blocks=npb),
        grid=(npb + 1,),
        in_specs=[pl.BlockSpec((tm, d), lambda i: (jnp.minimum(i, npb - 1), 0)),
                  pl.BlockSpec((tm, d), lambda i: (0, 0)),
                  pl.BlockSpec((1, d), lambda i: (0, 0))],
        out_specs=pl.BlockSpec((tm, d), lambda i: (i, 0)),
        out_shape=jax.ShapeDtypeStruct((mp + ms, d), BF16),
        compiler_params=_cparams(("arbitrary",)),
        name="norm_inputs",
    )(xp, xs, w.reshape(1, d))


def _norm_kernel(x_ref, w_ref, *o_refs):
    x = x_ref[...]
    y = x * lax.rsqrt(jnp.mean(x * x, axis=-1, keepdims=True) + EPS)
    for i, o_ref in enumerate(o_refs):
        o_ref[...] = (y * w_ref[i:i + 1, :]).astype(o_ref.dtype)


def _norm(x, ws, tm):
    m, d = x.shape
    nw = ws.shape[0]
    assert m % tm == 0
    return pl.pallas_call(
        _norm_kernel,
        grid=(m // tm,),
        in_specs=[pl.BlockSpec((tm, d), lambda i: (i, 0)),
                  pl.BlockSpec((nw, d), lambda i: (0, 0))],
        out_specs=[pl.BlockSpec((tm, d), lambda i: (i, 0)) for _ in range(nw)],
        out_shape=[jax.ShapeDtypeStruct((m, d), BF16) for _ in range(nw)],
        compiler_params=_cparams(("arbitrary",)),
        name="norm",
    )(x, ws)


def _first_rows(j, i):
    return jnp.where(j == 0, 0, i)


def _mm_stream_kernel(*refs, n_res, n_out_refs, w_t, n_m, tail):
    x_ref, w_ref, *mid, wbf_ref = refs
    res_refs, out_refs = mid[:n_res], mid[n_res:]
    assert len(out_refs) == n_out_refs
    j = pl.program_id(0)
    i = pl.program_id(1)
    rows_c = w_ref.shape[0]
    slot = j % 2
    ic = jnp.minimum(i, n_m - 1)
    wbf_ref[slot, pl.ds(pl.multiple_of(ic * rows_c, rows_c), rows_c), :] = w_ref[...].astype(BF16)

    def product(rows, res_ref):
        contract = (((1,), (1 if w_t else 0,)), ((), ()))
        acc = lax.dot_general(x_ref[:rows, :], wbf_ref[1 - slot], contract, preferred_element_type=F32)
        if res_ref is not None:
            acc = acc + res_ref[:rows, :]
        return acc

    @pl.when((j > 0) & (i < n_m))
    def _():
        out_refs[0][...] = product(x_ref.shape[0], res_refs[0] if n_res else None).astype(out_refs[0].dtype)

    if tail:
        @pl.when((j > 0) & (i == n_m))
        def _():
            acc = product(tail, res_refs[-1] if n_res else None)
            if n_out_refs == 2:
                out_refs[1][...] = acc.astype(out_refs[1].dtype)
            else:
                out_refs[0][:tail, :] = acc.astype(out_refs[0].dtype)


def _mm_stream(x, w, layer, *, n_out, tn, n_m, kblock=0, n_kblocks=1, res=None, out_dtype=F32, w_t=False,
               tail=0, split_out=True):
    m = x.shape[0]
    m_main = m - tail
    k = x.shape[1] // n_kblocks
    tm, nj = m_main // n_m, n_out // tn
    rows_c = (tn if w_t else k) // n_m
    assert m_main % n_m == 0 and tm % 16 == 0 and rows_c * n_m == (tn if w_t else k) and rows_c % 16 == 0
    assert n_out % tn == 0 and w.shape[2 if w_t else 1] == k * n_kblocks == x.shape[1] and k % LANES == 0
    assert 0 <= tail <= tm and tail % 8 == 0
    chunk = lambda i: jnp.minimum(i, n_m - 1)
    if w_t:
        w_spec = pl.BlockSpec((None, rows_c, k),
                              lambda j, i: (layer, jnp.minimum(j, nj - 1) * n_m + chunk(i), kblock))
    else:
        w_spec = pl.BlockSpec((None, rows_c, tn),
                              lambda j, i: (layer, kblock * n_m + chunk(i), jnp.minimum(j, nj - 1)))
    in_specs = [pl.BlockSpec((tm, k), lambda j, i: (_first_rows(j, i), kblock)), w_spec]
    args = [x, w]
    col = lambda j: jnp.maximum(j - 1, 0)
    whole = pl.BlockSpec((tm, tn), lambda j, i: (_first_rows(j, i), col(j)))
    main = pl.BlockSpec((tm, tn), lambda j, i: (jnp.minimum(_first_rows(j, i), n_m - 1), col(j)))
    last = pl.BlockSpec((tail, tn), lambda j, i: (0, col(j)))
    res_list = [] if res is None else list(res) if isinstance(res, tuple) else [res]
    in_specs += [main, last] if len(res_list) == 2 else [whole] * len(res_list)
    args += res_list
    if tail and split_out:
        out_specs = [main, last]
        out_shape = [jax.ShapeDtypeStruct((m_main, n_out), out_dtype), jax.ShapeDtypeStruct((tail, n_out), out_dtype)]
    else:
        out_specs = [whole]
        out_shape = [jax.ShapeDtypeStruct((m, n_out), out_dtype)]
    outs = pl.pallas_call(
        functools.partial(_mm_stream_kernel, n_res=len(res_list), n_out_refs=len(out_specs), w_t=w_t, n_m=n_m,
                          tail=tail),
        grid=(nj + 1, n_m + (1 if tail else 0)),
        in_specs=in_specs,
        out_specs=out_specs,
        out_shape=out_shape,
        scratch_shapes=[pltpu.VMEM((2, tn, k) if w_t else (2, k, tn), BF16)],
        compiler_params=_cparams(("arbitrary", "arbitrary")),
        name="mm_stream",
    )(*args)
    return tuple(outs) if len(outs) == 2 else outs[0]


FFN_PAIRS = 2


def _ffn_up_kernel(x_ref, *refs, tn):
    w_refs, (o_ref, wbf_ref) = refs[:2 * FFN_PAIRS], refs[2 * FFN_PAIRS:]
    j = pl.program_id(0)
    i = pl.program_id(1)
    rows_c = w_refs[0].shape[0]
    slot = j % 2
    rows = pl.ds(pl.multiple_of(i * rows_c, rows_c), rows_c)
    for c, w_ref in enumerate(w_refs):
        wbf_ref[slot, rows, c * tn:(c + 1) * tn] = w_ref[...].astype(BF16)

    @pl.when(j > 0)
    def _():
        gu = jnp.dot(x_ref[...], wbf_ref[1 - slot], preferred_element_type=F32)
        g = gu[:, :FFN_PAIRS * tn]
        o_ref[...] = (g * _sigmoid(g) * gu[:, FFN_PAIRS * tn:]).astype(o_ref.dtype)


def _ffn_up(x, w_gu, layer, *, n_m, tn):
    m, k = x.shape
    f = w_gu.shape[2] // 2
    tm, rows_c = m // n_m, k // n_m
    assert m % n_m == 0 and tm % 16 == 0 and k % n_m == 0 and rows_c % 16 == 0 and f % tn == 0
    n_pairs = f // tn
    nj = pl.cdiv(n_pairs, FFN_PAIRS)

    def w_map(c):
        g, part = c % FFN_PAIRS, c // FFN_PAIRS

        def index(j, i):
            pair = jnp.minimum(jnp.minimum(j, nj - 1) * FFN_PAIRS + g, n_pairs - 1)
            return layer, i, part * n_pairs + pair
        return index

    out_map = lambda j, i: (_first_rows(j, i), jnp.maximum(j - 1, 0))
    return pl.pallas_call(
        functools.partial(_ffn_up_kernel, tn=tn),
        grid=(nj + 1, n_m),
        in_specs=[pl.BlockSpec((tm, k), lambda j, i: (_first_rows(j, i), 0)),
                  *[pl.BlockSpec((None, rows_c, tn), w_map(c)) for c in range(2 * FFN_PAIRS)]],
        out_specs=pl.BlockSpec((tm, FFN_PAIRS * tn), out_map),
        out_shape=jax.ShapeDtypeStruct((m, f), BF16),
        scratch_shapes=[pltpu.VMEM((2, k, 2 * FFN_PAIRS * tn), BF16)],
        compiler_params=_cparams(("arbitrary", "arbitrary")),
        name="ffn_up",
    )(x, *([w_gu] * (2 * FFN_PAIRS)))


def _gates_kernel(x_ref, w_ref, alog_ref, dtb_ref, o_ref, *, n_heads, n_prompt_blocks, c_prompt, c_sample):
    tm = x_ref.shape[0]
    ba = lax.dot_general(w_ref[...], x_ref[...], (((1,), (1,)), ((), ())), preferred_element_type=F32)
    beta = _sigmoid(ba[:n_heads])
    a = ba[n_heads:] + dtb_ref[...]
    softplus = jnp.maximum(a, 0.0) + jnp.log(1.0 + jnp.exp(-jnp.abs(a)))
    g = -jnp.exp(alog_ref[...]) * softplus
    shift = jnp.where(pl.program_id(0) < n_prompt_blocks,
                      int(math.log2(c_prompt)), int(math.log2(c_sample)))
    jj = lax.broadcasted_iota(jnp.int32, (tm, tm), 0)
    ii = lax.broadcasted_iota(jnp.int32, (tm, tm), 1)
    same = lax.shift_right_logical(jj, shift) == lax.shift_right_logical(ii, shift)
    cum_m = jnp.where(same & (jj <= ii), 1.0, 0.0).astype(F32)
    tot_m = jnp.where(same, 1.0, 0.0).astype(F32)
    gcum = jnp.dot(g, cum_m, preferred_element_type=F32, precision=lax.Precision.HIGHEST)
    glast = jnp.dot(g, tot_m, preferred_element_type=F32, precision=lax.Precision.HIGHEST)
    o_ref[0] = beta
    o_ref[1] = jnp.exp(gcum)
    o_ref[2] = jnp.exp(glast - gcum)
    o_ref[3] = gcum


def _gates(xn, w_ba_t, a_log, dt_bias, *, tm, n_prompt_rows, c_prompt, c_sample):
    m, k = xn.shape
    h = a_log.shape[0]
    assert m % tm == 0 and n_prompt_rows % tm == 0 and tm % c_prompt == 0 and tm % c_sample == 0
    assert w_ba_t.shape == (2 * h, k)
    return pl.pallas_call(
        functools.partial(_gates_kernel, n_heads=h, n_prompt_blocks=n_prompt_rows // tm,
                          c_prompt=c_prompt, c_sample=c_sample),
        grid=(m // tm,),
        in_specs=[pl.BlockSpec((tm, k), lambda i: (i, 0)),
                  pl.BlockSpec((2 * h, k), lambda i: (0, 0)),
                  pl.BlockSpec((h, 1), lambda i: (0, 0)),
                  pl.BlockSpec((h, 1), lambda i: (0, 0))],
        out_specs=pl.BlockSpec((4, h, tm), lambda i: (0, 0, i)),
        out_shape=jax.ShapeDtypeStruct((4, h, m), F32),
        compiler_params=_cparams(("arbitrary",)),
        name="gates",
    )(xn, w_ba_t, a_log.reshape(h, 1), dt_bias.reshape(h, 1))


def _delta_kernel(q_ref, k_ref, v_ref, z_ref, cwq_ref, cwk_ref, cwv_ref, csq_ref, csk_ref, csv_ref,
                  gcol_ref, grow_ref, h0_ref, onw_ref,
                  o_ref, hout_ref, cqo_ref, cko_ref, cvo_ref,
                  h_scr, xq_scr, xk_scr, xv_scr, aq_scr, ak_scr, av_scr,
                  *, C, nb, nc, hb, dk, zero_init):
    t = pl.program_id(2)
    n_t = pl.num_programs(2)
    ttb = nc * C
    hist0 = CONV_PAD - (CONV_W - 1)

    @pl.when(t == 0)
    def _():
        if zero_init:
            h_scr[...] = jnp.zeros(h_scr.shape, F32)
        else:
            h_scr[...] = h0_ref[...]
        for xs, cs in ((xq_scr, csq_ref), (xk_scr, csk_ref), (xv_scr, csv_ref)):
            for bi in range(nb):
                xs[bi, hist0:CONV_PAD, :] = cs[bi]

    streams = ((q_ref, xq_scr, cwq_ref, aq_scr), (k_ref, xk_scr, cwk_ref, ak_scr), (v_ref, xv_scr, cwv_ref, av_scr))
    for raw_ref, xs, _, _ in streams:
        for bi in range(nb):
            xs[bi, CONV_PAD:CONV_PAD + ttb, :] = raw_ref[bi * ttb:(bi + 1) * ttb, :]

    def conv_phases(grp):
        r0, n = grp[0] * C, len(grp) * C

        def phase(xs, cw_ref, act):
            def run():
                for bi in range(nb):
                    win = xs[bi, r0:r0 + CONV_PAD + n, :]
                    acc = win[CONV_PAD:] * cw_ref[CONV_W - 1:CONV_W, :]
                    for s in range(1, CONV_W):
                        acc = acc + pltpu.roll(win, s, axis=0)[CONV_PAD:] * cw_ref[CONV_W - 1 - s:CONV_W - s, :]
                    act[bi * ttb + r0:bi * ttb + r0 + n, :] = acc * _sigmoid(acc)
            return run
        return [phase(xs, cw_ref, act) for _, xs, cw_ref, act in streams]

    ii = lax.broadcasted_iota(jnp.int32, (C, C), 0)
    jj = lax.broadcasted_iota(jnp.int32, (C, C), 1)
    n_levels = int(math.log2(C))
    assert 2 ** n_levels == C
    merge_masks = [(lax.shift_right_logical(ii, l + 1) == lax.shift_right_logical(jj, l + 1))
                   & ((ii & (1 << l)) != 0) & ((jj & (1 << l)) == 0) for l in range(n_levels)]

    def prepare(bi, c, j):
        rows = slice(bi * ttb + c * C, bi * ttb + (c + 1) * C)
        cols = slice(j * dk, (j + 1) * dk)
        q = aq_scr[rows, cols]
        k = ak_scr[rows, cols]
        qn = q * (lax.rsqrt(jnp.sum(q * q, axis=-1, keepdims=True) + EPS) * (dk ** -0.5))
        kn = k * lax.rsqrt(jnp.sum(k * k, axis=-1, keepdims=True) + EPS)
        gcb = gcol_ref[0, rows, :]
        beta = gcb[:, j:j + 1]
        eg = gcb[:, hb + j:hb + j + 1]
        ekl = gcb[:, 2 * hb + j:2 * hb + j + 1]
        gc = gcb[:, 3 * hb + j:3 * hb + j + 1]
        gr = grow_ref[0, bi * nc + c, j:j + 1, :]
        dmat = jnp.where(ii >= jj, jnp.exp(gc - gr), 0.0)
        rhs = jnp.concatenate([kn * (beta * eg), av_scr[rows, cols] * beta], axis=1)
        return dict(rows=rows, cols=cols, qn=qn, kn=kn, beta=beta, eg=eg, dmat=dmat, rhs=rhs,
                    qe=qn * eg, kd=kn * ekl)

    st = {}

    def solve_phases(group):
        def scores():
            for it in group:
                e = st[it] = prepare(*it)
                e["s"] = _bdot_nt(jnp.concatenate([e["kn"], e["qn"]], axis=0), e["kn"])

        def first_level():
            for it in group:
                e = st[it]
                s = e.pop("s")
                e["qk"] = s[C:] * e["dmat"]
                e["a"] = jnp.where(ii > jj, s[:C] * e["dmat"] * e["beta"], 0.0)
                e["t"] = jnp.where(merge_masks[0], -e["a"], 0.0)

        def level_x(lvl):
            def run():
                for it in group:
                    e = st[it]
                    e["a21"] = jnp.where(merge_masks[lvl], e["a"], 0.0)
                    e["x"] = _bdot(e["a21"], e["t"])
            return run

        def level_s(lvl):
            def run():
                for it in group:
                    e = st[it]
                    x = e.pop("a21") + e.pop("x")
                    e["t"] = e["t"] - x - _bdot(e["t"], x)
            return run

        def apply_t():
            for it in group:
                e = st[it]
                e["wu"] = e["rhs"] + _bdot(e["t"], e["rhs"])

        merges = [f(lvl) for lvl in range(1, n_levels) for f in (level_x, level_s)]
        return [scores, first_level, *merges, apply_t]

    h_cur = {(bi, j): h_scr[bi, j] for bi in range(nb) for j in range(hb)}

    def chain_phases(c):
        sel = [(bi, c, j) for bi in range(nb) for j in range(hb)]

        def through_state():
            for it in sel:
                e = st[it]
                e["wq"] = _bdot(jnp.concatenate([e["wu"][:, :dk], e["qe"]], axis=0), h_cur[it[0], it[2]])

        def update():
            for it in sel:
                e = st[it]
                e["u"] = e["wu"][:, dk:] - e["wq"][:C]
            for it in sel:
                e = st[it]
                e["o"] = e["wq"][C:] + _bdot(e["qk"], e["u"])
                egl = e["eg"][C - 1:C, :]
                h_cur[it[0], it[2]] = h_cur[it[0], it[2]] * egl + _bdot_tn(e["kd"], e["u"])

        def emit():
            for it in sel:
                e = st.pop(it)
                o = e["o"]
                on = o * lax.rsqrt(jnp.mean(o * o, axis=-1, keepdims=True) + EPS) * onw_ref[...]
                zz = z_ref[e["rows"], e["cols"]]
                o_ref[e["rows"], e["cols"]] = (on * (zz * _sigmoid(zz))).astype(o_ref.dtype)

        return [through_state, update, emit]

    chunk_groups = [list(range(c0, min(c0 + GROUP_CHUNKS, nc))) for c0 in range(0, nc, GROUP_CHUNKS)]
    n_grp = len(chunk_groups)
    convs = [conv_phases(grp) for grp in chunk_groups]
    solves = [solve_phases([(bi, c, j) for bi in range(nb) for c in grp for j in range(hb)])
              for grp in chunk_groups]
    chains = [[phase for c in grp for phase in chain_phases(c)] for grp in chunk_groups]
    for g in range(-2, n_grp):
        tracks = [convs[g + 2] if g + 2 < n_grp else [],
                  solves[g + 1] if 0 <= g + 1 < n_grp else [],
                  chains[g] if g >= 0 else []]
        for phases in itertools.zip_longest(*tracks):
            for phase in phases:
                if phase is not None:
                    phase()
    for (bi, j), hv in h_cur.items():
        h_scr[bi, j] = hv
    for _, xs, _, _ in streams:
        for bi in range(nb):
            xs[bi, hist0:CONV_PAD, :] = xs[bi, CONV_PAD + ttb - (CONV_W - 1):CONV_PAD + ttb, :]

    @pl.when(t == n_t - 1)
    def _():
        for (_, xs, _, _), co in zip(streams, (cqo_ref, cko_ref, cvo_ref)):
            for bi in range(nb):
                co[bi] = xs[bi, hist0:CONV_PAD, :]

    @pl.when(t == n_t - 1)
    def _():
        hout_ref[...] = h_scr[...]


def _delta_rule(proj, gates, conv_w, conv0, h0, o_norm_w, *, row0, batch, seq, n_heads, dk,
                C, nb, nc, hb, out_rows):
    ttb = nc * C
    tt = nb * ttb
    w = hb * dk
    ng = n_heads // hb
    n_t = seq // ttb
    assert seq % ttb == 0 and batch % nb == 0 and n_heads % hb == 0 and row0 % tt == 0
    zero_init = h0 is None
    d_qk = n_heads * dk
    cb = d_qk // w
    rb0 = row0 // tt

    rows = batch * seq
    g5 = gates.reshape(4, ng, hb, rows)
    gcol = jnp.transpose(g5, (1, 3, 0, 2)).reshape(ng, rows, 4 * hb)
    grow = jnp.transpose(g5[3].reshape(ng, hb, rows // C, C), (0, 2, 1, 3))

    def tok_map(part):
        return lambda b, g, t: (rb0 + b * n_t + t, part * cb + g)

    if zero_init:
        h0_arg = jnp.zeros((nb, hb, dk, dk), F32)
        h0_spec = pl.BlockSpec((nb, hb, dk, dk), lambda b, g, t: (0, 0, 0, 0))
    else:
        h0_arg = h0
        h0_spec = pl.BlockSpec((nb, hb, dk, dk), lambda b, g, t: (b, g, 0, 0))
    cs_spec = [pl.BlockSpec((nb, CONV_W - 1, w), (lambda b, g, t, p=p: (b, 0, p * cb + g))) for p in range(3)]
    cw_spec = [pl.BlockSpec((CONV_W, w), (lambda b, g, t, p=p: (0, p * cb + g))) for p in range(3)]
    kern = functools.partial(_delta_kernel, C=C, nb=nb, nc=nc, hb=hb, dk=dk, zero_init=zero_init)
    o, h_fin, cq, ck, cv = pl.pallas_call(
        kern,
        grid=(batch // nb, ng, n_t),
        in_specs=[pl.BlockSpec((tt, w), tok_map(0)), pl.BlockSpec((tt, w), tok_map(1)),
                  pl.BlockSpec((tt, w), tok_map(2)), pl.BlockSpec((tt, w), tok_map(3)),
                  *cw_spec, *cs_spec,
                  pl.BlockSpec((1, tt, 4 * hb), lambda b, g, t: (g, b * n_t + t, 0)),
                  pl.BlockSpec((1, nb * nc, hb, C), lambda b, g, t: (g, b * n_t + t, 0, 0)),
                  h0_spec,
                  pl.BlockSpec((1, dk), lambda b, g, t: (0, 0))],
        out_specs=[pl.BlockSpec((tt, w), lambda b, g, t: (rb0 + b * n_t + t, g)),
                   pl.BlockSpec((nb, hb, dk, dk), lambda b, g, t: (b, g, 0, 0)),
                   *[pl.BlockSpec((nb, CONV_W - 1, w), lambda b, g, t: (b, 0, g)) for _ in range(3)]],
        out_shape=[jax.ShapeDtypeStruct((out_rows, d_qk), BF16),
                   jax.ShapeDtypeStruct((batch, n_heads, dk, dk), F32),
                   *[jax.ShapeDtypeStruct((batch, CONV_W - 1, d_qk), F32) for _ in range(3)]],
        scratch_shapes=[pltpu.VMEM((nb, hb, dk, dk), F32),
                        *[pltpu.VMEM((nb, CONV_PAD + ttb, w), F32) for _ in range(3)],
                        *[pltpu.VMEM((tt, w), F32) for _ in range(3)]],
        compiler_params=_cparams(("arbitrary", "arbitrary", "arbitrary")),
        name="delta_rule",
    )(proj, proj, proj, proj, conv_w, conv_w, conv_w, conv0, conv0, conv0, gcol, grow, h0_arg,
      o_norm_w.reshape(1, dk))
    return o, h_fin, jnp.concatenate([cq, ck, cv], axis=-1)


def _group_mean_sq(x, gmat_ref, inv_n):
    x2 = x * x
    hi = x2.astype(BF16)
    lo = (x2 - hi.astype(F32)).astype(BF16)
    g = gmat_ref[...]
    return (jnp.dot(hi, g, preferred_element_type=F32) + jnp.dot(lo, g, preferred_element_type=F32)) * inv_n


def _rope(y, cos_ref, sa_ref, sb_ref, half):
    w = y.shape[1]
    reps = w // cos_ref.shape[1]
    tile = lambda r: jnp.concatenate([r[...]] * reps, axis=1)
    return (y * tile(cos_ref) + pltpu.roll(y, w - half, axis=1) * tile(sa_ref)
            + pltpu.roll(y, half, axis=1) * tile(sb_ref))


def _q_post_kernel(x_ref, w_ref, gmat_ref, cos_ref, sa_ref, sb_ref, o_ref, *, hd, half, scale):
    gw = gmat_ref.shape[0]
    for c in range(x_ref.shape[1] // gw):
        cols = slice(c * gw, (c + 1) * gw)
        x = x_ref[:, cols]
        y = x * lax.rsqrt(_group_mean_sq(x, gmat_ref, 1.0 / hd) + EPS) * w_ref[...]
        o_ref[:, cols] = (_rope(y, cos_ref, sa_ref, sb_ref, half) * scale).astype(o_ref.dtype)


def _kv_post_kernel(k_ref, v_ref, w_ref, gmat_ref, cos_ref, sa_ref, sb_ref,
                    kf_ref, kb_ref, vbt_ref, kwin_ref, vwin_ref, *, hd, half, n_prompt_blocks, blocks_per_seq):
    i = pl.program_id(0)
    x = k_ref[...]
    y = x * lax.rsqrt(_group_mean_sq(x, gmat_ref, 1.0 / hd) + EPS) * w_ref[...]
    kf = _rope(y, cos_ref, sa_ref, sb_ref, half)
    kf_ref[...] = kf
    v = v_ref[...]
    vt = v.T
    for hh in range(kb_ref.shape[0]):
        kb_ref[hh] = kf[:, hh * hd:(hh + 1) * hd].astype(BF16)
        vbt_ref[hh] = vt[hh * hd:(hh + 1) * hd, :].astype(BF16)

    @pl.when((i < n_prompt_blocks) & (i % blocks_per_seq == blocks_per_seq - 1))
    def _():
        wb = kwin_ref.shape[0]
        kwin_ref[...] = kf[kf.shape[0] - wb:, :]
        vwin_ref[...] = v[v.shape[0] - wb:, :]


def _rope_tables(pos, hd, rot_dim):
    half = rot_dim // 2
    inv_freq = jnp.power(ROPE_THETA, -jnp.arange(half, dtype=F32) * 2.0 / rot_dim)
    ang = pos[:, None] * inv_freq[None, :]
    cos, sin = jnp.cos(ang), jnp.sin(ang)
    m = pos.shape[0]
    ones = jnp.ones((m, hd - rot_dim), F32)
    zeros = jnp.zeros((m, hd - rot_dim), F32)
    zh = jnp.zeros((m, half), F32)
    c = jnp.concatenate([cos, cos, ones], axis=1)
    sa = jnp.concatenate([-sin, zh, zeros], axis=1)
    sb = jnp.concatenate([zh, sin, zeros], axis=1)
    reps = LANES // hd
    return tuple(jnp.tile(a, (1, reps)) for a in (c, sa, sb))


def _group_matrix(width, hd):
    r = jnp.arange(width) // hd
    return (r[:, None] == r[None, :]).astype(BF16)


def _softmax_sink(s, mask, sink):
    s = jnp.where(mask, s, -jnp.inf)
    m = jnp.maximum(jnp.max(s, axis=-1, keepdims=True), sink)
    p = jnp.exp2(s - m)
    denom = jnp.sum(p, axis=-1, keepdims=True) + jnp.exp2(sink - m)
    return (p / denom).astype(BF16)


def _attn_prompt_kernel(sink_ref, q_ref, kp_ref, kc_ref, vp_ref, vc_ref, o_ref, *, kv_heads, group, hd, win):
    nb = pl.program_id(1)
    kj = lax.broadcasted_iota(jnp.int32, (2 * win, win), 0)
    qi = lax.broadcasted_iota(jnp.int32, (2 * win, win), 1)
    mask = (kj > qi) & (kj <= qi + win) & ((kj >= win) | (nb > 0))

    def scores(h):
        kk = jnp.concatenate([kp_ref[h], kc_ref[h]], axis=0)
        return [lax.dot_general(kk, q_ref[:, (h * group + g) * hd:(h * group + g + 1) * hd],
                                (((1,), (1,)), ((), ())), preferred_element_type=F32) for g in range(group)]

    def row_max(h, ss):
        out = []
        for g, s in enumerate(ss):
            s = jnp.where(mask, s, -jnp.inf)
            out.append((s, jnp.maximum(jnp.max(s, axis=0, keepdims=True), sink_ref[h * group + g] * LOG2E)))
        return out

    def probs(h, sm):
        out = []
        for g, (s, m) in enumerate(sm):
            p = jnp.exp2(s - m)
            denom = jnp.sum(p, axis=0, keepdims=True) + jnp.exp2(sink_ref[h * group + g] * LOG2E - m)
            out.append((p.astype(BF16), denom))
        return out

    def weighted_values(h, pd):
        vt = jnp.concatenate([vp_ref[h], vc_ref[h]], axis=1)
        ot = jnp.concatenate([jnp.dot(vt, p, preferred_element_type=F32) / denom for p, denom in pd], axis=0)
        o_ref[:, h * group * hd:(h + 1) * group * hd] = ot.T.astype(o_ref.dtype)

    stages = (scores, row_max, probs, weighted_values)
    carry = [None] * len(stages)
    for step in range(kv_heads + len(stages) - 1):
        for k in reversed(range(len(stages))):
            h = step - k
            if 0 <= h < kv_heads:
                carry[k] = stages[k](h) if k == 0 else stages[k](h, carry[k - 1])


def _attn_prompt(q, kb, vbt, sinks, *, batch, seq, kv_heads, group, hd, out_rows):
    win = WINDOW
    nblk = seq // win
    qw = kv_heads * group * hd
    k_spec = lambda back: pl.BlockSpec((kv_heads, win, hd),
                                       lambda b, n: (0, b * nblk + jnp.maximum(n - back, 0), 0))
    v_spec = lambda back: pl.BlockSpec((kv_heads, hd, win),
                                       lambda b, n: (0, 0, b * nblk + jnp.maximum(n - back, 0)))
    return pl.pallas_call(
        functools.partial(_attn_prompt_kernel, kv_heads=kv_heads, group=group, hd=hd, win=win),
        grid=(batch, nblk),
        in_specs=[pl.BlockSpec(memory_space=pltpu.SMEM),
                  pl.BlockSpec((win, qw), lambda b, n: (b * nblk + n, 0)),
                  k_spec(1), k_spec(0), v_spec(1), v_spec(0)],
        out_specs=pl.BlockSpec((win, qw), lambda b, n: (b * nblk + n, 0)),
        out_shape=jax.ShapeDtypeStruct((out_rows, qw), BF16),
        compiler_params=_cparams(("arbitrary", "arbitrary")),
        name="attn_prompt",
    )(sinks, q, kb, kb, vbt, vbt)


def _attn_sample_kernel(sink_ref, q_ref, kn_ref, vn_ref, kc_ref, vc_ref, prev_ref, o_ref,
                        *, nbs, tq, kv_heads, group, hd, win):
    del prev_ref
    nctx = 2 * win
    r = lax.broadcasted_iota(jnp.int32, (group * tq, nctx), 0)
    kj = lax.broadcasted_iota(jnp.int32, (group * tq, nctx), 1)
    tpos = r % tq
    mask = ((kj < win) & (kj > tpos)) | ((kj >= win) & (kj - win <= tpos))
    pad = jnp.zeros((nctx - win - tq, hd), F32)
    items = [(bi, h) for bi in range(nbs) for h in range(kv_heads)]
    qbs = [q_ref[bi * tq:(bi + 1) * tq, :].astype(F32) for bi in range(nbs)]
    sinks = [jnp.concatenate([jnp.full((tq, 1), sink_ref[h * group + g] * LOG2E, F32) for g in range(group)],
                             axis=0) for h in range(kv_heads)]
    ss, vs = [], []
    for bi, h in items:
        cs = slice(h * hd, (h + 1) * hd)
        trows = slice(bi * tq, (bi + 1) * tq)
        kctx = jnp.concatenate([kc_ref[bi][:, cs], kn_ref[trows, cs], pad], axis=0)
        vs.append(jnp.concatenate([vc_ref[bi][:, cs], vn_ref[trows, cs], pad], axis=0).astype(BF16))
        qg = jnp.concatenate([qbs[bi][:, (h * group + g) * hd:(h * group + g + 1) * hd]
                              for g in range(group)], axis=0)
        ss.append(_bdot_nt(qg, kctx))
    ps = [_softmax_sink(s, mask, sinks[h]) for (bi, h), s in zip(items, ss)]
    for (bi, h), p, v in zip(items, ps, vs):
        o = jnp.dot(p, v, preferred_element_type=F32)
        for g in range(group):
            hh = h * group + g
            o_ref[bi * tq:(bi + 1) * tq, hh * hd:(hh + 1) * hd] = o[g * tq:(g + 1) * tq, :].astype(o_ref.dtype)


def _attn_sample(q, kf, kv, cache_k, cache_v, sinks, o_prev, *, row0, batch, tq, kv_heads, group, hd, nbs):
    win = cache_k.shape[1]
    assert win == WINDOW and batch % nbs == 0 and row0 % (nbs * tq) == 0
    rows = nbs * tq
    rb0 = row0 // rows
    kvw = kv_heads * hd
    return pl.pallas_call(
        functools.partial(_attn_sample_kernel, nbs=nbs, tq=tq, kv_heads=kv_heads, group=group, hd=hd, win=win),
        grid=(batch // nbs,),
        in_specs=[pl.BlockSpec(memory_space=pltpu.SMEM),
                  pl.BlockSpec((rows, q.shape[1]), lambda b: (rb0 + b, 0)),
                  pl.BlockSpec((rows, kvw), lambda b: (rb0 + b, 0)),
                  pl.BlockSpec((rows, kvw), lambda b: (rb0 + b, 1)),
                  pl.BlockSpec((nbs, win, kvw), lambda b: (b, 0, 0)),
                  pl.BlockSpec((nbs, win, kvw), lambda b: (b, 0, 0)),
                  pl.BlockSpec(memory_space=pl.ANY)],
        out_specs=pl.BlockSpec((rows, q.shape[1]), lambda b: (rb0 + b, 0)),
        out_shape=jax.ShapeDtypeStruct(o_prev.shape, o_prev.dtype),
        input_output_aliases={6: 0},
        compiler_params=_cparams(("arbitrary",)),
        name="attn_sample",
    )(sinks, q, kf, kv, cache_k, cache_v, o_prev)


PAST_LEN = 16384
TM_NORM = 256
M_STEPS = 8


Q_POST_COLS = 2048
TN_WIDE = 1024
TN_RES = 512
TN_FFN = 256
DOWN_KBLOCKS = 2


def _write_rows_kernel(src_ref, dst_ref, o_ref):
    del dst_ref
    o_ref[...] = src_ref[...]


def _merge_rows(dst, src, row0, tm):
    n = src.shape[1]
    rows = src.shape[0] - row0
    assert rows % tm == 0 and row0 % tm == 0
    return pl.pallas_call(
        _write_rows_kernel,
        grid=(rows // tm,),
        in_specs=[pl.BlockSpec((tm, n), lambda i: (row0 // tm + i, 0)),
                  pl.BlockSpec(memory_space=pl.ANY)],
        out_specs=pl.BlockSpec((tm, n), lambda i: (row0 // tm + i, 0)),
        out_shape=jax.ShapeDtypeStruct(dst.shape, dst.dtype),
        input_output_aliases={1: 0},
        compiler_params=_cparams(("arbitrary",)),
        name="merge_rows",
    )(src, dst)


def kernel(x_prompt, x_sample, state_ssm, state_conv, cache_win_k, cache_win_v, norm_mix, norm_ffn,
           a_w_in, a_conv_w, a_log, a_dt_bias, a_o_norm, a_w_out, kv_norm, w_kv, k_norm, b_w_q,
           b_q_norm, b_sinks, b_w_o, ffn_w_gu, ffn_w_down):
    bp, sp, d = x_prompt.shape
    bs, ss, _ = x_sample.shape
    mp, ms = bp * sp, bs * ss
    m = mp + ms
    n_a = a_w_in.shape[0]
    assert n_a == 1 and b_w_q.shape[0] == 1, "layer pattern is one delta layer then one attention layer"
    h_a = a_log.shape[1]
    dk = a_o_norm.shape[1]
    d_qk = h_a * dk
    kv_heads, hd = cache_win_k.shape[2], cache_win_k.shape[3]
    kvw = kv_heads * hd
    h_b = b_sinks.shape[1]
    group = h_b // kv_heads
    rot_dim = hd // 4

    def mm(xb, w, n_out, res=None):
        tn = min(TN_WIDE if res is None else TN_RES, n_out)
        return _mm_stream(xb, w, 0, n_out=n_out, tn=tn, n_m=M_STEPS, res=res)

    def ffn(hin, hn, layer, split_streams=False):
        act = _ffn_up(hn, ffn_w_gu, layer, n_m=M_STEPS, tn=TN_FFN)
        out = hin
        for kb in range(DOWN_KBLOCKS):
            tail = ms if split_streams and kb == DOWN_KBLOCKS - 1 else 0
            out = _mm_stream(act, ffn_w_down, layer, n_out=d, tn=min(TN_RES, d), n_m=M_STEPS,
                             kblock=kb, n_kblocks=DOWN_KBLOCKS, res=out, tail=tail)
        return out

    xp2, xs2 = x_prompt.reshape(mp, d), x_sample.reshape(ms, d)
    xn = _norm_inputs(xp2, xs2, norm_mix[0], TM_NORM)
    w_in_t = jnp.swapaxes(a_w_in, 1, 2)
    proj = _mm_stream(xn, w_in_t, 0, n_out=4 * d_qk, tn=TN_WIDE, n_m=M_STEPS, w_t=True)
    w_ba_t = w_in_t[0, 4 * d_qk:, :].astype(BF16)
    c_s = math.gcd(ss, DELTA_CHUNK)
    gates = _gates(xn, w_ba_t, a_log[0], a_dt_bias[0], tm=TM_NORM, n_prompt_rows=mp,
                   c_prompt=DELTA_CHUNK, c_sample=c_s)
    conv_w = a_conv_w[0]
    conv0_p = jnp.zeros((bp, CONV_W - 1, 3 * d_qk), F32)
    o_p, ssm_p, conv_p = _delta_rule(proj, gates[:, :, :mp], conv_w, conv0_p, None, a_o_norm[0],
                                     row0=0, batch=bp, seq=sp, n_heads=h_a, dk=dk,
                                     C=DELTA_CHUNK, nb=1, nc=4, hb=8, out_rows=m)
    o_s, ssm_s, conv_s = _delta_rule(proj, gates[:, :, mp:], conv_w, state_conv[0], state_ssm[0], a_o_norm[0],
                                     row0=mp, batch=bs, seq=ss, n_heads=h_a, dk=dk,
                                     C=c_s, nb=4, nc=1, hb=8, out_rows=m)
    o_a = _merge_rows(o_p, o_s, mp, TM_NORM)
    h1 = _mm_stream(o_a, a_w_out, 0, n_out=d, tn=min(TN_RES, d), n_m=M_STEPS, res=(xp2, xs2), tail=ms,
                    split_out=False)
    (hn1,) = _norm(h1, norm_ffn[0:1], TM_NORM)
    h2 = ffn(h1, hn1, 0)

    hkv, hq = _norm(h2, jnp.stack([kv_norm, norm_mix[1]]), TM_NORM)
    kv = mm(hkv, w_kv[None], 2 * kvw)
    pos = jnp.concatenate([jnp.tile(jnp.arange(sp, dtype=F32), bp),
                           jnp.tile(PAST_LEN + jnp.arange(ss, dtype=F32), bs)])
    cos_t, sa_t, sb_t = _rope_tables(pos, hd, rot_dim)
    gmat = _group_matrix(kvw, hd)
    wb = min(WINDOW, sp)
    bps = sp // TM_NORM
    assert sp % TM_NORM == 0 and wb <= TM_NORM
    win_spec = pl.BlockSpec((wb, kvw), lambda i: (jnp.minimum(i // bps, bp - 1), 0))
    kf, kb, vbt, kwin_p, vwin_p = pl.pallas_call(
        functools.partial(_kv_post_kernel, hd=hd, half=rot_dim // 2, n_prompt_blocks=mp // TM_NORM,
                          blocks_per_seq=bps),
        grid=(m // TM_NORM,),
        in_specs=[pl.BlockSpec((TM_NORM, kvw), lambda i: (i, 0)),
                  pl.BlockSpec((TM_NORM, kvw), lambda i: (i, 1)),
                  pl.BlockSpec((1, kvw), lambda i: (0, 0)),
                  pl.BlockSpec((kvw, kvw), lambda i: (0, 0)),
                  *[pl.BlockSpec((TM_NORM, LANES), lambda i: (i, 0)) for _ in range(3)]],
        out_specs=[pl.BlockSpec((TM_NORM, kvw), lambda i: (i, 0)),
                   pl.BlockSpec((kv_heads, TM_NORM, hd), lambda i: (0, i, 0)),
                   pl.BlockSpec((kv_heads, hd, TM_NORM), lambda i: (0, 0, i)),
                   win_spec, win_spec],
        out_shape=[jax.ShapeDtypeStruct((m, kvw), F32),
                   jax.ShapeDtypeStruct((kv_heads, m, hd), BF16),
                   jax.ShapeDtypeStruct((kv_heads, hd, m), BF16),
                   jax.ShapeDtypeStruct((bp * wb, kvw), F32),
                   jax.ShapeDtypeStruct((bp * wb, kvw), F32)],
        compiler_params=_cparams(("arbitrary",)),
        name="kv_post",
    )(kv, kv, jnp.tile(k_norm, kv_heads).reshape(1, kvw), gmat, cos_t, sa_t, sb_t)

    qraw = mm(hq, b_w_q, h_b * hd)
    qw = kvw
    qcols = h_b * hd
    qblk = min(Q_POST_COLS, qcols)
    assert qcols % qblk == 0 and qblk % qw == 0
    q = pl.pallas_call(
        functools.partial(_q_post_kernel, hd=hd, half=rot_dim // 2, scale=hd ** -0.5 * LOG2E),
        grid=(m // TM_NORM, qcols // qblk),
        in_specs=[pl.BlockSpec((TM_NORM, qblk), lambda i, j: (i, j)),
                  pl.BlockSpec((1, qw), lambda i, j: (0, 0)),
                  pl.BlockSpec((qw, qw), lambda i, j: (0, 0)),
                  *[pl.BlockSpec((TM_NORM, LANES), lambda i, j: (i, 0)) for _ in range(3)]],
        out_specs=pl.BlockSpec((TM_NORM, qblk), lambda i, j: (i, j)),
        out_shape=jax.ShapeDtypeStruct((m, qcols), BF16),
        compiler_params=_cparams(("arbitrary", "arbitrary")),
        name="q_post",
    )(qraw, jnp.tile(b_q_norm[0], qw // hd).reshape(1, qw), gmat, cos_t, sa_t, sb_t)
    sinks = b_sinks[0]
    o_b = _attn_prompt(q, kb, vbt, sinks, batch=bp, seq=sp, kv_heads=kv_heads, group=group, hd=hd, out_rows=m)
    o_b = _attn_sample(q, kf, kv, cache_win_k.reshape(bs, WINDOW, kvw), cache_win_v.reshape(bs, WINDOW, kvw),
                       sinks, o_b, row0=mp, batch=bs, tq=ss, kv_heads=kv_heads, group=group, hd=hd, nbs=2)
    h3 = mm(o_b, b_w_o, d, res=h2)
    (hn3,) = _norm(h3, norm_ffn[1:2], TM_NORM)
    y_p, y_s = ffn(h3, hn3, 1, split_streams=True)

    kf_p = kwin_p.reshape(bp, wb, kv_heads, hd)
    vf_p = vwin_p.reshape(bp, wb, kv_heads, hd)
    kf_s = kf[mp:].reshape(bs, ss, kv_heads, hd)
    vf_s = kv[mp:, kvw:].reshape(bs, ss, kv_heads, hd)
    wk_s = jnp.concatenate([cache_win_k[:, ss:], kf_s], axis=1)
    wv_s = jnp.concatenate([cache_win_v[:, ss:], vf_s], axis=1)
    return (y_p.reshape(bp, sp, d), y_s.reshape(bs, ss, d),
            ssm_p[None], conv_p[None], kf_p, vf_p, ssm_s[None], conv_s[None], wk_s, wv_s)
```

```python
import functools
import itertools
import math

import jax
import jax.numpy as jnp
from jax import lax
from jax.experimental import pallas as pl
from jax.experimental.pallas import tpu as pltpu

F32 = jnp.float32
BF16 = jnp.bfloat16

EPS = 1e-6
LOG2E = math.log2(math.e)
WINDOW = 128
ROPE_THETA = 500000.0
CONV_W = 4
DELTA_CHUNK = 64
LANES = 128
CONV_PAD = 8
GROUP_CHUNKS = 2
VMEM_LIMIT = 56 * 1024 * 1024


def _cparams(sem):
    return pltpu.CompilerParams(dimension_semantics=sem, vmem_limit_bytes=VMEM_LIMIT)


def _sigmoid(x):
    return 1.0 / (1.0 + jnp.exp(-x))


def _bdot(a, b):
    return jnp.dot(a.astype(BF16), b.astype(BF16), preferred_element_type=F32)


def _bdot_nt(a, b):
    return lax.dot_general(a.astype(BF16), b.astype(BF16), (((1,), (1,)), ((), ())),
                           preferred_element_type=F32)


def _bdot_tn(a, b):
    return lax.dot_general(a.astype(BF16), b.astype(BF16), (((0,), (0,)), ((), ())),
                           preferred_element_type=F32)


def _norm_inputs_kernel(xp_ref, xs_ref, w_ref, xn_ref, *, n_prompt_blocks):
    i = pl.program_id(0)

    def emit(x):
        r = lax.rsqrt(jnp.mean(x * x, axis=-1, keepdims=True) + EPS)
        xn_ref[...] = (x * r * w_ref[...]).astype(xn_ref.dtype)

    @pl.when(i < n_prompt_blocks)
    def _():
        emit(xp_ref[...])

    @pl.when(i >= n_prompt_blocks)
    def _():
        emit(xs_ref[...])


def _norm_inputs(xp, xs, w, tm):
    mp, d = xp.shape
    ms = xs.shape[0]
    assert mp % tm == 0 and ms == tm
    npb = mp // tm
    return pl.pallas_call(
        functools.partial(_norm_inputs_kernel, n_prompt_blocks=npb),
        grid=(npb + 1,),
        in_specs=[pl.BlockSpec((tm, d), lambda i: (jnp.minimum(i, npb - 1), 0)),
                  pl.BlockSpec((tm, d), lambda i: (0, 0)),
                  pl.BlockSpec((1, d), lambda i: (0, 0))],
        out_specs=pl.BlockSpec((tm, d), lambda i: (i, 0)),
        out_shape=jax.ShapeDtypeStruct((mp + ms, d), BF16),
        compiler_params=_cparams(("arbitrary",)),
        name="norm_inputs",
    )(xp, xs, w.reshape(1, d))


def _norm_kernel(x_ref, w_ref, *o_refs):
    x = x_ref[...]
    y = x * lax.rsqrt(jnp.mean(x * x, axis=-1, keepdims=True) + EPS)
    for i, o_ref in enumerate(o_refs):
        o_ref[...] = (y * w_ref[i:i + 1, :]).astype(o_ref.dtype)


def _norm(x, ws, tm):
    m, d = x.shape
    nw = ws.shape[0]
    assert m % tm == 0
    return pl.pallas_call(
        _norm_kernel,
        grid=(m // tm,),
        in_specs=[pl.BlockSpec((tm, d), lambda i: (i, 0)),
                  pl.BlockSpec((nw, d), lambda i: (0, 0))],
        out_specs=[pl.BlockSpec((tm, d), lambda i: (i, 0)) for _ in range(nw)],
        out_shape=[jax.ShapeDtypeStruct((m, d), BF16) for _ in range(nw)],
        compiler_params=_cparams(("arbitrary",)),
        name="norm",
    )(x, ws)


def _first_rows(j, i):
    return jnp.where(j == 0, 0, i)


def _mm_stream_kernel(*refs, n_res, n_out_refs, w_t, n_m, tail):
    x_ref, w_ref, *mid, wbf_ref = refs
    res_refs, out_refs = mid[:n_res], mid[n_res:]
    assert len(out_refs) == n_out_refs
    j = pl.program_id(0)
    i = pl.program_id(1)
    rows_c = w_ref.shape[0]
    slot = j % 2
    ic = jnp.minimum(i, n_m - 1)
    wbf_ref[slot, pl.ds(pl.multiple_of(ic * rows_c, rows_c), rows_c), :] = w_ref[...].astype(BF16)

    def product(rows, res_ref):
        contract = (((1,), (1 if w_t else 0,)), ((), ()))
        acc = lax.dot_general(x_ref[:rows, :], wbf_ref[1 - slot], contract, preferred_element_type=F32)
        if res_ref is not None:
            acc = acc + res_ref[:rows, :]
        return acc

    @pl.when((j > 0) & (i < n_m))
    def _():
        out_refs[0][...] = product(x_ref.shape[0], res_refs[0] if n_res else None).astype(out_refs[0].dtype)

    if tail:
        @pl.when((j > 0) & (i == n_m))
        def _():
            acc = product(tail, res_refs[-1] if n_res else None)
            if n_out_refs == 2:
                out_refs[1][...] = acc.astype(out_refs[1].dtype)
            else:
                out_refs[0][:tail, :] = acc.astype(out_refs[0].dtype)


def _mm_stream(x, w, layer, *, n_out, tn, n_m, kblock=0, n_kblocks=1, res=None, out_dtype=F32, w_t=False,
               tail=0, split_out=True):
    m = x.shape[0]
    m_main = m - tail
    k = x.shape[1] // n_kblocks
    tm, nj = m_main // n_m, n_out // tn
    rows_c = (tn if w_t else k) // n_m
    assert m_main % n_m == 0 and tm % 16 == 0 and rows_c * n_m == (tn if w_t else k) and rows_c % 16 == 0
    assert n_out % tn == 0 and w.shape[2 if w_t else 1] == k * n_kblocks == x.shape[1] and k % LANES == 0
    assert 0 <= tail <= tm and tail % 8 == 0
    chunk = lambda i: jnp.minimum(i, n_m - 1)
    if w_t:
        w_spec = pl.BlockSpec((None, rows_c, k),
                              lambda j, i: (layer, jnp.minimum(j, nj - 1) * n_m + chunk(i), kblock))
    else:
        w_spec = pl.BlockSpec((None, rows_c, tn),
                              lambda j, i: (layer, kblock * n_m + chunk(i), jnp.minimum(j, nj - 1)))
    in_specs = [pl.BlockSpec((tm, k), lambda j, i: (_first_rows(j, i), kblock)), w_spec]
    args = [x, w]
    col = lambda j: jnp.maximum(j - 1, 0)
    whole = pl.BlockSpec((tm, tn), lambda j, i: (_first_rows(j, i), col(j)))
    main = pl.BlockSpec((tm, tn), lambda j, i: (jnp.minimum(_first_rows(j, i), n_m - 1), col(j)))
    last = pl.BlockSpec((tail, tn), lambda j, i: (0, col(j)))
    res_list = [] if res is None else list(res) if isinstance(res, tuple) else [res]
    in_specs += [main, last] if len(res_list) == 2 else [whole] * len(res_list)
    args += res_list
    if tail and split_out:
        out_specs = [main, last]
        out_shape = [jax.ShapeDtypeStruct((m_main, n_out), out_dtype), jax.ShapeDtypeStruct((tail, n_out), out_dtype)]
    else:
        out_specs = [whole]
        out_shape = [jax.ShapeDtypeStruct((m, n_out), out_dtype)]
    outs = pl.pallas_call(
        functools.partial(_mm_stream_kernel, n_res=len(res_list), n_out_refs=len(out_specs), w_t=w_t, n_m=n_m,
                          tail=tail),
        grid=(nj + 1, n_m + (1 if tail else 0)),
        in_specs=in_specs,
        out_specs=out_specs,
        out_shape=out_shape,
        scratch_shapes=[pltpu.VMEM((2, tn, k) if w_t else (2, k, tn), BF16)],
        compiler_params=_cparams(("arbitrary", "arbitrary")),
        name="mm_stream",
    )(*args)
    return tuple(outs) if len(outs) == 2 else outs[0]


FFN_PAIRS = 2


def _ffn_up_kernel(x_ref, *refs, tn):
    w_refs, (o_ref, wbf_ref) = refs[:2 * FFN_PAIRS], refs[2 * FFN_PAIRS:]
    j = pl.program_id(0)
    i = pl.program_id(1)
    rows_c = w_refs[0].shape[0]
    slot = j % 2
    rows = pl.ds(pl.multiple_of(i * rows_c, rows_c), rows_c)
    for c, w_ref in enumerate(w_refs):
        wbf_ref[slot, rows, c * tn:(c + 1) * tn] = w_ref[...].astype(BF16)

    @pl.when(j > 0)
    def _():
        gu = jnp.dot(x_ref[...], wbf_ref[1 - slot], preferred_element_type=F32)
        g = gu[:, :FFN_PAIRS * tn]
        o_ref[...] = (g * _sigmoid(g) * gu[:, FFN_PAIRS * tn:]).astype(o_ref.dtype)


def _ffn_up(x, w_gu, layer, *, n_m, tn):
    m, k = x.shape
    f = w_gu.shape[2] // 2
    tm, rows_c = m // n_m, k // n_m
    assert m % n_m == 0 and tm % 16 == 0 and k % n_m == 0 and rows_c % 16 == 0 and f % tn == 0
    n_pairs = f // tn
    nj = pl.cdiv(n_pairs, FFN_PAIRS)

    def w_map(c):
        g, part = c % FFN_PAIRS, c // FFN_PAIRS

        def index(j, i):
            pair = jnp.minimum(jnp.minimum(j, nj - 1) * FFN_PAIRS + g, n_pairs - 1)
            return layer, i, part * n_pairs + pair
        return index

    out_map = lambda j, i: (_first_rows(j, i), jnp.maximum(j - 1, 0))
    return pl.pallas_call(
        functools.partial(_ffn_up_kernel, tn=tn),
        grid=(nj + 1, n_m),
        in_specs=[pl.BlockSpec((tm, k), lambda j, i: (_first_rows(j, i), 0)),
                  *[pl.BlockSpec((None, rows_c, tn), w_map(c)) for c in range(2 * FFN_PAIRS)]],
        out_specs=pl.BlockSpec((tm, FFN_PAIRS * tn), out_map),
        out_shape=jax.ShapeDtypeStruct((m, f), BF16),
        scratch_shapes=[pltpu.VMEM((2, k, 2 * FFN_PAIRS * tn), BF16)],
        compiler_params=_cparams(("arbitrary", "arbitrary")),
        name="ffn_up",
    )(x, *([w_gu] * (2 * FFN_PAIRS)))


def _gates_kernel(x_ref, w_ref, alog_ref, dtb_ref, o_ref, *, n_heads, n_prompt_blocks, c_prompt, c_sample):
    tm = x_ref.shape[0]
    ba = lax.dot_general(w_ref[...], x_ref[...], (((1,), (1,)), ((), ())), preferred_element_type=F32)
    beta = _sigmoid(ba[:n_heads])
    a = ba[n_heads:] + dtb_ref[...]
    softplus = jnp.maximum(a, 0.0) + jnp.log(1.0 + jnp.exp(-jnp.abs(a)))
    g = -jnp.exp(alog_ref[...]) * softplus
    shift = jnp.where(pl.program_id(0) < n_prompt_blocks,
                      int(math.log2(c_prompt)), int(math.log2(c_sample)))
    jj = lax.broadcasted_iota(jnp.int32, (tm, tm), 0)
    ii = lax.broadcasted_iota(jnp.int32, (tm, tm), 1)
    same = lax.shift_right_logical(jj, shift) == lax.shift_right_logical(ii, shift)
    cum_m = jnp.where(same & (jj <= ii), 1.0, 0.0).astype(F32)
    tot_m = jnp.where(same, 1.0, 0.0).astype(F32)
    gcum = jnp.dot(g, cum_m, preferred_element_type=F32, precision=lax.Precision.HIGHEST)
    glast = jnp.dot(g, tot_m, preferred_element_type=F32, precision=lax.Precision.HIGHEST)
    o_ref[0] = beta
    o_ref[1] = jnp.exp(gcum)
    o_ref[2] = jnp.exp(glast - gcum)
    o_ref[3] = gcum


def _gates(xn, w_ba_t, a_log, dt_bias, *, tm, n_prompt_rows, c_prompt, c_sample):
    m, k = xn.shape
    h = a_log.shape[0]
    assert m % tm == 0 and n_prompt_rows % tm == 0 and tm % c_prompt == 0 and tm % c_sample == 0
    assert w_ba_t.shape == (2 * h, k)
    return pl.pallas_call(
        functools.partial(_gates_kernel, n_heads=h, n_prompt_blocks=n_prompt_rows // tm,
                          c_prompt=c_prompt, c_sample=c_sample),
        grid=(m // tm,),
        in_specs=[pl.BlockSpec((tm, k), lambda i: (i, 0)),
                  pl.BlockSpec((2 * h, k), lambda i: (0, 0)),
                  pl.BlockSpec((h, 1), lambda i: (0, 0)),
                  pl.BlockSpec((h, 1), lambda i: (0, 0))],
        out_specs=pl.BlockSpec((4, h, tm), lambda i: (0, 0, i)),
        out_shape=jax.ShapeDtypeStruct((4, h, m), F32),
        compiler_params=_cparams(("arbitrary",)),
        name="gates",
    )(xn, w_ba_t, a_log.reshape(h, 1), dt_bias.reshape(h, 1))


def _delta_kernel(q_ref, k_ref, v_ref, z_ref, cwq_ref, cwk_ref, cwv_ref, csq_ref, csk_ref, csv_ref,
                  gcol_ref, grow_ref, h0_ref, onw_ref,
                  o_ref, hout_ref, cqo_ref, cko_ref, cvo_ref,
                  h_scr, xq_scr, xk_scr, xv_scr, aq_scr, ak_scr, av_scr,
                  *, C, nb, nc, hb, dk, zero_init):
    t = pl.program_id(2)
    n_t = pl.num_programs(2)
    ttb = nc * C
    hist0 = CONV_PAD - (CONV_W - 1)

    @pl.when(t == 0)
    def _():
        if zero_init:
            h_scr[...] = jnp.zeros(h_scr.shape, F32)
        else:
            h_scr[...] = h0_ref[...]
        for xs, cs in ((xq_scr, csq_ref), (xk_scr, csk_ref), (xv_scr, csv_ref)):
            for bi in range(nb):
                xs[bi, hist0:CONV_PAD, :] = cs[bi]

    streams = ((q_ref, xq_scr, cwq_ref, aq_scr), (k_ref, xk_scr, cwk_ref, ak_scr), (v_ref, xv_scr, cwv_ref, av_scr))
    for raw_ref, xs, _, _ in streams:
        for bi in range(nb):
            xs[bi, CONV_PAD:CONV_PAD + ttb, :] = raw_ref[bi * ttb:(bi + 1) * ttb, :]

    def conv_phases(grp):
        r0, n = grp[0] * C, len(grp) * C

        def phase(xs, cw_ref, act):
            def run():
                for bi in range(nb):
                    win = xs[bi, r0:r0 + CONV_PAD + n, :]
                    acc = win[CONV_PAD:] * cw_ref[CONV_W - 1:CONV_W, :]
                    for s in range(1, CONV_W):
                        acc = acc + pltpu.roll(win, s, axis=0)[CONV_PAD:] * cw_ref[CONV_W - 1 - s:CONV_W - s, :]
                    act[bi * ttb + r0:bi * ttb + r0 + n, :] = acc * _sigmoid(acc)
            return run
        return [phase(xs, cw_ref, act) for _, xs, cw_ref, act in streams]

    ii = lax.broadcasted_iota(jnp.int32, (C, C), 0)
    jj = lax.broadcasted_iota(jnp.int32, (C, C), 1)
    n_levels = int(math.log2(C))
    assert 2 ** n_levels == C
    merge_masks = [(lax.shift_right_logical(ii, l + 1) == lax.shift_right_logical(jj, l + 1))
                   & ((ii & (1 << l)) != 0) & ((jj & (1 << l)) == 0) for l in range(n_levels)]

    def prepare(bi, c, j):
        rows = slice(bi * ttb + c * C, bi * ttb + (c + 1) * C)
        cols = slice(j * dk, (j + 1) * dk)
        q = aq_scr[rows, cols]
        k = ak_scr[rows, cols]
        qn = q * (lax.rsqrt(jnp.sum(q * q, axis=-1, keepdims=True) + EPS) * (dk ** -0.5))
        kn = k * lax.rsqrt(jnp.sum(k * k, axis=-1, keepdims=True) + EPS)
        gcb = gcol_ref[0, rows, :]
        beta = gcb[:, j:j + 1]
        eg = gcb[:, hb + j:hb + j + 1]
        ekl = gcb[:, 2 * hb + j:2 * hb + j + 1]
        gc = gcb[:, 3 * hb + j:3 * hb + j + 1]
        gr = grow_ref[0, bi * nc + c, j:j + 1, :]
        dmat = jnp.where(ii >= jj, jnp.exp(gc - gr), 0.0)
        rhs = jnp.concatenate([kn * (beta * eg), av_scr[rows, cols] * beta], axis=1)
        return dict(rows=rows, cols=cols, qn=qn, kn=kn, beta=beta, eg=eg, dmat=dmat, rhs=rhs,
                    qe=qn * eg, kd=kn * ekl)

    st = {}

    def solve_phases(group):
        def scores():
            for it in group:
                e = st[it] = prepare(*it)
                e["s"] = _bdot_nt(jnp.concatenate([e["kn"], e["qn"]], axis=0), e["kn"])

        def first_level():
            for it in group:
                e = st[it]
                s = e.pop("s")
                e["qk"] = s[C:] * e["dmat"]
                e["a"] = jnp.where(ii > jj, s[:C] * e["dmat"] * e["beta"], 0.0)
                e["t"] = jnp.where(merge_masks[0], -e["a"], 0.0)

        def level_x(lvl):
            def run():
                for it in group:
                    e = st[it]
                    e["a21"] = jnp.where(merge_masks[lvl], e["a"], 0.0)
                    e["x"] = _bdot(e["a21"], e["t"])
            return run

        def level_s(lvl):
            def run():
                for it in group:
                    e = st[it]
                    x = e.pop("a21") + e.pop("x")
                    e["t"] = e["t"] - x - _bdot(e["t"], x)
            return run

        def apply_t():
            for it in group:
                e = st[it]
                e["wu"] = e["rhs"] + _bdot(e["t"], e["rhs"])

        merges = [f(lvl) for lvl in range(1, n_levels) for f in (level_x, level_s)]
        return [scores, first_level, *merges, apply_t]

    h_cur = {(bi, j): h_scr[bi, j] for bi in range(nb) for j in range(hb)}

    def chain_phases(c):
        sel = [(bi, c, j) for bi in range(nb) for j in range(hb)]

        def through_state():
            for it in sel:
                e = st[it]
                e["wq"] = _bdot(jnp.concatenate([e["wu"][:, :dk], e["qe"]], axis=0), h_cur[it[0], it[2]])

        def update():
            for it in sel:
                e = st[it]
                e["u"] = e["wu"][:, dk:] - e["wq"][:C]
            for it in sel:
                e = st[it]
                e["o"] = e["wq"][C:] + _bdot(e["qk"], e["u"])
                egl = e["eg"][C - 1:C, :]
                h_cur[it[0], it[2]] = h_cur[it[0], it[2]] * egl + _bdot_tn(e["kd"], e["u"])

        def emit():
            for it in sel:
                e = st.pop(it)
                o = e["o"]
                on = o * lax.rsqrt(jnp.mean(o * o, axis=-1, keepdims=True) + EPS) * onw_ref[...]
                zz = z_ref[e["rows"], e["cols"]]
                o_ref[e["rows"], e["cols"]] = (on * (zz * _sigmoid(zz))).astype(o_ref.dtype)

        return [through_state, update, emit]

    chunk_groups = [list(range(c0, min(c0 + GROUP_CHUNKS, nc))) for c0 in range(0, nc, GROUP_CHUNKS)]
    n_grp = len(chunk_groups)
    convs = [conv_phases(grp) for grp in chunk_groups]
    solves = [solve_phases([(bi, c, j) for bi in range(nb) for c in grp for j in range(hb)])
              for grp in chunk_groups]
    chains = [[phase for c in grp for phase in chain_phases(c)] for grp in chunk_groups]
    for g in range(-2, n_grp):
        tracks = [convs[g + 2] if g + 2 < n_grp else [],
                  solves[g + 1] if 0 <= g + 1 < n_grp else [],
                  chains[g] if g >= 0 else []]
        for phases in itertools.zip_longest(*tracks):
            for phase in phases:
                if phase is not None:
                    phase()
    for (bi, j), hv in h_cur.items():
        h_scr[bi, j] = hv
    for _, xs, _, _ in streams:
        for bi in range(nb):
            xs[bi, hist0:CONV_PAD, :] = xs[bi, CONV_PAD + ttb - (CONV_W - 1):CONV_PAD + ttb, :]

    @pl.when(t == n_t - 1)
    def _():
        for (_, xs, _, _), co in zip(streams, (cqo_ref, cko_ref, cvo_ref)):
            for bi in range(nb):
                co[bi] = xs[bi, hist0:CONV_PAD, :]

    @pl.when(t == n_t - 1)
    def _():
        hout_ref[...] = h_scr[...]


def _delta_rule(proj, gates, conv_w, conv0, h0, o_norm_w, *, row0, batch, seq, n_heads, dk,
                C, nb, nc, hb, out_rows):
    ttb = nc * C
    tt = nb * ttb
    w = hb * dk
    ng = n_heads // hb
    n_t = seq // ttb
    assert seq % ttb == 0 and batch % nb == 0 and n_heads % hb == 0 and row0 % tt == 0
    zero_init = h0 is None
    d_qk = n_heads * dk
    cb = d_qk // w
    rb0 = row0 // tt

    rows = batch * seq
    g5 = gates.reshape(4, ng, hb, rows)
    gcol = jnp.transpose(g5, (1, 3, 0, 2)).reshape(ng, rows, 4 * hb)
    grow = jnp.transpose(g5[3].reshape(ng, hb, rows // C, C), (0, 2, 1, 3))

    def tok_map(part):
        return lambda b, g, t: (rb0 + b * n_t + t, part * cb + g)

    if zero_init:
        h0_arg = jnp.zeros((nb, hb, dk, dk), F32)
        h0_spec = pl.BlockSpec((nb, hb, dk, dk), lambda b, g, t: (0, 0, 0, 0))
    else:
        h0_arg = h0
        h0_spec = pl.BlockSpec((nb, hb, dk, dk), lambda b, g, t: (b, g, 0, 0))
    cs_spec = [pl.BlockSpec((nb, CONV_W - 1, w), (lambda b, g, t, p=p: (b, 0, p * cb + g))) for p in range(3)]
    cw_spec = [pl.BlockSpec((CONV_W, w), (lambda b, g, t, p=p: (0, p * cb + g))) for p in range(3)]
    kern = functools.partial(_delta_kernel, C=C, nb=nb, nc=nc, hb=hb, dk=dk, zero_init=zero_init)
    o, h_fin, cq, ck, cv = pl.pallas_call(
        kern,
        grid=(batch // nb, ng, n_t),
        in_specs=[pl.BlockSpec((tt, w), tok_map(0)), pl.BlockSpec((tt, w), tok_map(1)),
                  pl.BlockSpec((tt, w), tok_map(2)), pl.BlockSpec((tt, w), tok_map(3)),
                  *cw_spec, *cs_spec,
                  pl.BlockSpec((1, tt, 4 * hb), lambda b, g, t: (g, b * n_t + t, 0)),
                  pl.BlockSpec((1, nb * nc, hb, C), lambda b, g, t: (g, b * n_t + t, 0, 0)),
                  h0_spec,
                  pl.BlockSpec((1, dk), lambda b, g, t: (0, 0))],
        out_specs=[pl.BlockSpec((tt, w), lambda b, g, t: (rb0 + b * n_t + t, g)),
                   pl.BlockSpec((nb, hb, dk, dk), lambda b, g, t: (b, g, 0, 0)),
                   *[pl.BlockSpec((nb, CONV_W - 1, w), lambda b, g, t: (b, 0, g)) for _ in range(3)]],
        out_shape=[jax.ShapeDtypeStruct((out_rows, d_qk), BF16),
                   jax.ShapeDtypeStruct((batch, n_heads, dk, dk), F32),
                   *[jax.ShapeDtypeStruct((batch, CONV_W - 1, d_qk), F32) for _ in range(3)]],
        scratch_shapes=[pltpu.VMEM((nb, hb, dk, dk), F32),
                        *[pltpu.VMEM((nb, CONV_PAD + ttb, w), F32) for _ in range(3)],
                        *[pltpu.VMEM((tt, w), F32) for _ in range(3)]],
        compiler_params=_cparams(("arbitrary", "arbitrary", "arbitrary")),
        name="delta_rule",
    )(proj, proj, proj, proj, conv_w, conv_w, conv_w, conv0, conv0, conv0, gcol, grow, h0_arg,
      o_norm_w.reshape(1, dk))
    return o, h_fin, jnp.concatenate([cq, ck, cv], axis=-1)


def _group_mean_sq(x, gmat_ref, inv_n):
    x2 = x * x
    hi = x2.astype(BF16)
    lo = (x2 - hi.astype(F32)).astype(BF16)
    g = gmat_ref[...]
    return (jnp.dot(hi, g, preferred_element_type=F32) + jnp.dot(lo, g, preferred_element_type=F32)) * inv_n


def _rope(y, cos_ref, sa_ref, sb_ref, half):
    w = y.shape[1]
    reps = w // cos_ref.shape[1]
    tile = lambda r: jnp.concatenate([r[...]] * reps, axis=1)
    return (y * tile(cos_ref) + pltpu.roll(y, w - half, axis=1) * tile(sa_ref)
            + pltpu.roll(y, half, axis=1) * tile(sb_ref))


def _q_post_kernel(x_ref, w_ref, gmat_ref, cos_ref, sa_ref, sb_ref, o_ref, *, hd, half, scale):
    gw = gmat_ref.shape[0]
    for c in range(x_ref.shape[1] // gw):
        cols = slice(c * gw, (c + 1) * gw)
        x = x_ref[:, cols]
        y = x * lax.rsqrt(_group_mean_sq(x, gmat_ref, 1.0 / hd) + EPS) * w_ref[...]
        o_ref[:, cols] = (_rope(y, cos_ref, sa_ref, sb_ref, half) * scale).astype(o_ref.dtype)


def _kv_post_kernel(k_ref, v_ref, w_ref, gmat_ref, cos_ref, sa_ref, sb_ref,
                    kf_ref, kb_ref, vbt_ref, kwin_ref, vwin_ref, *, hd, half, n_prompt_blocks, blocks_per_seq):
    i = pl.program_id(0)
    x = k_ref[...]
    y = x * lax.rsqrt(_group_mean_sq(x, gmat_ref, 1.0 / hd) + EPS) * w_ref[...]
    kf = _rope(y, cos_ref, sa_ref, sb_ref, half)
    kf_ref[...] = kf
    v = v_ref[...]
    vt = v.T
    for hh in range(kb_ref.shape[0]):
        kb_ref[hh] = kf[:, hh * hd:(hh + 1) * hd].astype(BF16)
        vbt_ref[hh] = vt[hh * hd:(hh + 1) * hd, :].astype(BF16)

    @pl.when((i < n_prompt_blocks) & (i % blocks_per_seq == blocks_per_seq - 1))
    def _():
        wb = kwin_ref.shape[0]
        kwin_ref[...] = kf[kf.shape[0] - wb:, :]
        vwin_ref[...] = v[v.shape[0] - wb:, :]


def _rope_tables(pos, hd, rot_dim):
    half = rot_dim // 2
    inv_freq = jnp.power(ROPE_THETA, -jnp.arange(half, dtype=F32) * 2.0 / rot_dim)
    ang = pos[:, None] * inv_freq[None, :]
    cos, sin = jnp.cos(ang), jnp.sin(ang)
    m = pos.shape[0]
    ones = jnp.ones((m, hd - rot_dim), F32)
    zeros = jnp.zeros((m, hd - rot_dim), F32)
    zh = jnp.zeros((m, half), F32)
    c = jnp.concatenate([cos, cos, ones], axis=1)
    sa = jnp.concatenate([-sin, zh, zeros], axis=1)
    sb = jnp.concatenate([zh, sin, zeros], axis=1)
    reps = LANES // hd
    return tuple(jnp.tile(a, (1, reps)) for a in (c, sa, sb))


def _group_matrix(width, hd):
    r = jnp.arange(width) // hd
    return (r[:, None] == r[None, :]).astype(BF16)


def _softmax_sink(s, mask, sink):
    s = jnp.where(mask, s, -jnp.inf)
    m = jnp.maximum(jnp.max(s, axis=-1, keepdims=True), sink)
    p = jnp.exp2(s - m)
    denom = jnp.sum(p, axis=-1, keepdims=True) + jnp.exp2(sink - m)
    return (p / denom).astype(BF16)


def _attn_prompt_kernel(sink_ref, q_ref, kp_ref, kc_ref, vp_ref, vc_ref, o_ref, *, kv_heads, group, hd, win):
    nb = pl.program_id(1)
    kj = lax.broadcasted_iota(jnp.int32, (2 * win, win), 0)
    qi = lax.broadcasted_iota(jnp.int32, (2 * win, win), 1)
    mask = (kj > qi) & (kj <= qi + win) & ((kj >= win) | (nb > 0))

    def scores(h):
        kk = jnp.concatenate([kp_ref[h], kc_ref[h]], axis=0)
        return [lax.dot_general(kk, q_ref[:, (h * group + g) * hd:(h * group + g + 1) * hd],
                                (((1,), (1,)), ((), ())), preferred_element_type=F32) for g in range(group)]

    def row_max(h, ss):
        out = []
        for g, s in enumerate(ss):
            s = jnp.where(mask, s, -jnp.inf)
            out.append((s, jnp.maximum(jnp.max(s, axis=0, keepdims=True), sink_ref[h * group + g] * LOG2E)))
        return out

    def probs(h, sm):
        out = []
        for g, (s, m) in enumerate(sm):
            p = jnp.exp2(s - m)
            denom = jnp.sum(p, axis=0, keepdims=True) + jnp.exp2(sink_ref[h * group + g] * LOG2E - m)
            out.append((p.astype(BF16), denom))
        return out

    def weighted_values(h, pd):
        vt = jnp.concatenate([vp_ref[h], vc_ref[h]], axis=1)
        ot = jnp.concatenate([jnp.dot(vt, p, preferred_element_type=F32) / denom for p, denom in pd], axis=0)
        o_ref[:, h * group * hd:(h + 1) * group * hd] = ot.T.astype(o_ref.dtype)

    stages = (scores, row_max, probs, weighted_values)
    carry = [None] * len(stages)
    for step in range(kv_heads + len(stages) - 1):
        for k in reversed(range(len(stages))):
            h = step - k
            if 0 <= h < kv_heads:
                carry[k] = stages[k](h) if k == 0 else stages[k](h, carry[k - 1])


def _attn_prompt(q, kb, vbt, sinks, *, batch, seq, kv_heads, group, hd, out_rows):
    win = WINDOW
    nblk = seq // win
    qw = kv_heads * group * hd
    k_spec = lambda back: pl.BlockSpec((kv_heads, win, hd),
                                       lambda b, n: (0, b * nblk + jnp.maximum(n - back, 0), 0))
    v_spec = lambda back: pl.BlockSpec((kv_heads, hd, win),
                                       lambda b, n: (0, 0, b * nblk + jnp.maximum(n - back, 0)))
    return pl.pallas_call(
        functools.partial(_attn_prompt_kernel, kv_heads=kv_heads, group=group, hd=hd, win=win),
        grid=(batch, nblk),
        in_specs=[pl.BlockSpec(memory_space=pltpu.SMEM),
                  pl.BlockSpec((win, qw), lambda b, n: (b * nblk + n, 0)),
                  k_spec(1), k_spec(0), v_spec(1), v_spec(0)],
        out_specs=pl.BlockSpec((win, qw), lambda b, n: (b * nblk + n, 0)),
        out_shape=jax.ShapeDtypeStruct((out_rows, qw), BF16),
        compiler_params=_cparams(("arbitrary", "arbitrary")),
        name="attn_prompt",
    )(sinks, q, kb, kb, vbt, vbt)


def _attn_sample_kernel(sink_ref, q_ref, kn_ref, vn_ref, kc_ref, vc_ref, prev_ref, o_ref, wk_ref, wv_ref,
                        *, nbs, tq, kv_heads, group, hd, win):
    del prev_ref
    for new_ref, cache_ref, w_ref in ((kn_ref, kc_ref, wk_ref), (vn_ref, vc_ref, wv_ref)):
        for bi in range(nbs):
            w_ref[bi, 0:win - tq, :] = cache_ref[bi, tq:win, :]
            w_ref[bi, win - tq:win, :] = new_ref[bi * tq:(bi + 1) * tq, :]
    nctx = 2 * win
    r = lax.broadcasted_iota(jnp.int32, (group * tq, nctx), 0)
    kj = lax.broadcasted_iota(jnp.int32, (group * tq, nctx), 1)
    tpos = r % tq
    mask = ((kj < win) & (kj > tpos)) | ((kj >= win) & (kj - win <= tpos))
    pad = jnp.zeros((nctx - win - tq, hd), F32)
    items = [(bi, h) for bi in range(nbs) for h in range(kv_heads)]
    qbs = [q_ref[bi * tq:(bi + 1) * tq, :].astype(F32) for bi in range(nbs)]
    sinks = [jnp.concatenate([jnp.full((tq, 1), sink_ref[h * group + g] * LOG2E, F32) for g in range(group)],
                             axis=0) for h in range(kv_heads)]
    ss, vs = [], []
    for bi, h in items:
        cs = slice(h * hd, (h + 1) * hd)
        trows = slice(bi * tq, (bi + 1) * tq)
        kctx = jnp.concatenate([kc_ref[bi][:, cs], kn_ref[trows, cs], pad], axis=0)
        vs.append(jnp.concatenate([vc_ref[bi][:, cs], vn_ref[trows, cs], pad], axis=0).astype(BF16))
        qg = jnp.concatenate([qbs[bi][:, (h * group + g) * hd:(h * group + g + 1) * hd]
                              for g in range(group)], axis=0)
        ss.append(_bdot_nt(qg, kctx))
    ps = [_softmax_sink(s, mask, sinks[h]) for (bi, h), s in zip(items, ss)]
    for (bi, h), p, v in zip(items, ps, vs):
        o = jnp.dot(p, v, preferred_element_type=F32)
        for g in range(group):
            hh = h * group + g
            o_ref[bi * tq:(bi + 1) * tq, hh * hd:(hh + 1) * hd] = o[g * tq:(g + 1) * tq, :].astype(o_ref.dtype)


def _attn_sample(q, kf, kv, cache_k, cache_v, sinks, o_prev, *, row0, batch, tq, kv_heads, group, hd, nbs):
    win = cache_k.shape[1]
    assert win == WINDOW and batch % nbs == 0 and row0 % (nbs * tq) == 0
    rows = nbs * tq
    rb0 = row0 // rows
    kvw = kv_heads * hd
    return pl.pallas_call(
        functools.partial(_attn_sample_kernel, nbs=nbs, tq=tq, kv_heads=kv_heads, group=group, hd=hd, win=win),
        grid=(batch // nbs,),
        in_specs=[pl.BlockSpec(memory_space=pltpu.SMEM),
                  pl.BlockSpec((rows, q.shape[1]), lambda b: (rb0 + b, 0)),
                  pl.BlockSpec((rows, kvw), lambda b: (rb0 + b, 0)),
                  pl.BlockSpec((rows, kvw), lambda b: (rb0 + b, 1)),
                  pl.BlockSpec((nbs, win, kvw), lambda b: (b, 0, 0)),
                  pl.BlockSpec((nbs, win, kvw), lambda b: (b, 0, 0)),
                  pl.BlockSpec(memory_space=pl.ANY)],
        out_specs=[pl.BlockSpec((rows, q.shape[1]), lambda b: (rb0 + b, 0)),
                   pl.BlockSpec((nbs, win, kvw), lambda b: (b, 0, 0)),
                   pl.BlockSpec((nbs, win, kvw), lambda b: (b, 0, 0))],
        out_shape=[jax.ShapeDtypeStruct(o_prev.shape, o_prev.dtype),
                   jax.ShapeDtypeStruct(cache_k.shape, F32), jax.ShapeDtypeStruct(cache_v.shape, F32)],
        input_output_aliases={6: 0},
        compiler_params=_cparams(("arbitrary",)),
        name="attn_sample",
    )(sinks, q, kf, kv, cache_k, cache_v, o_prev)


PAST_LEN = 16384
TM_NORM = 256
M_STEPS = 8


Q_POST_COLS = 2048
TN_WIDE = 1024
TN_RES = 512
TN_FFN = 256
DOWN_KBLOCKS = 2


def _write_rows_kernel(src_ref, dst_ref, o_ref):
    del dst_ref
    o_ref[...] = src_ref[...]


def _merge_rows(dst, src, row0, tm):
    n = src.shape[1]
    rows = src.shape[0] - row0
    assert rows % tm == 0 and row0 % tm == 0
    return pl.pallas_call(
        _write_rows_kernel,
        grid=(rows // tm,),
        in_specs=[pl.BlockSpec((tm, n), lambda i: (row0 // tm + i, 0)),
                  pl.BlockSpec(memory_space=pl.ANY)],
        out_specs=pl.BlockSpec((tm, n), lambda i: (row0 // tm + i, 0)),
        out_shape=jax.ShapeDtypeStruct(dst.shape, dst.dtype),
        input_output_aliases={1: 0},
        compiler_params=_cparams(("arbitrary",)),
        name="merge_rows",
    )(src, dst)


def kernel(x_prompt, x_sample, state_ssm, state_conv, cache_win_k, cache_win_v, norm_mix, norm_ffn,
           a_w_in, a_conv_w, a_log, a_dt_bias, a_o_norm, a_w_out, kv_norm, w_kv, k_norm, b_w_q,
           b_q_norm, b_sinks, b_w_o, ffn_w_gu, ffn_w_down):
    bp, sp, d = x_prompt.shape
    bs, ss, _ = x_sample.shape
    mp, ms = bp * sp, bs * ss
    m = mp + ms
    n_a = a_w_in.shape[0]
    assert n_a == 1 and b_w_q.shape[0] == 1, "layer pattern is one delta layer then one attention layer"
    h_a = a_log.shape[1]
    dk = a_o_norm.shape[1]
    d_qk = h_a * dk
    kv_heads, hd = cache_win_k.shape[2], cache_win_k.shape[3]
    kvw = kv_heads * hd
    h_b = b_sinks.shape[1]
    group = h_b // kv_heads
    rot_dim = hd // 4

    def mm(xb, w, n_out, res=None):
        tn = min(TN_WIDE if res is None else TN_RES, n_out)
        return _mm_stream(xb, w, 0, n_out=n_out, tn=tn, n_m=M_STEPS, res=res)

    def ffn(hin, hn, layer, split_streams=False):
        act = _ffn_up(hn, ffn_w_gu, layer, n_m=M_STEPS, tn=TN_FFN)
        out = hin
        for kb in range(DOWN_KBLOCKS):
            tail = ms if split_streams and kb == DOWN_KBLOCKS - 1 else 0
            out = _mm_stream(act, ffn_w_down, layer, n_out=d, tn=min(TN_RES, d), n_m=M_STEPS,
                             kblock=kb, n_kblocks=DOWN_KBLOCKS, res=out, tail=tail)
        return out

    xp2, xs2 = x_prompt.reshape(mp, d), x_sample.reshape(ms, d)
    xn = _norm_inputs(xp2, xs2, norm_mix[0], TM_NORM)
    w_in_t = jnp.swapaxes(a_w_in, 1, 2)
    proj = _mm_stream(xn, w_in_t, 0, n_out=4 * d_qk, tn=TN_WIDE, n_m=M_STEPS, w_t=True)
    w_ba_t = w_in_t[0, 4 * d_qk:, :].astype(BF16)
    c_s = math.gcd(ss, DELTA_CHUNK)
    gates = _gates(xn, w_ba_t, a_log[0], a_dt_bias[0], tm=TM_NORM, n_prompt_rows=mp,
                   c_prompt=DELTA_CHUNK, c_sample=c_s)
    conv_w = a_conv_w[0]
    conv0_p = jnp.zeros((bp, CONV_W - 1, 3 * d_qk), F32)
    o_p, ssm_p, conv_p = _delta_rule(proj, gates[:, :, :mp], conv_w, conv0_p, None, a_o_norm[0],
                                     row0=0, batch=bp, seq=sp, n_heads=h_a, dk=dk,
                                     C=DELTA_CHUNK, nb=1, nc=4, hb=8, out_rows=m)
    o_s, ssm_s, conv_s = _delta_rule(proj, gates[:, :, mp:], conv_w, state_conv[0], state_ssm[0], a_o_norm[0],
                                     row0=mp, batch=bs, seq=ss, n_heads=h_a, dk=dk,
                                     C=c_s, nb=4, nc=1, hb=8, out_rows=m)
    o_a = _merge_rows(o_p, o_s, mp, TM_NORM)
    h1 = _mm_stream(o_a, a_w_out, 0, n_out=d, tn=min(TN_RES, d), n_m=M_STEPS, res=(xp2, xs2), tail=ms,
                    split_out=False)
    (hn1,) = _norm(h1, norm_ffn[0:1], TM_NORM)
    h2 = ffn(h1, hn1, 0)

    hkv, hq = _norm(h2, jnp.stack([kv_norm, norm_mix[1]]), TM_NORM)
    kv = mm(hkv, w_kv[None], 2 * kvw)
    pos = jnp.concatenate([jnp.tile(jnp.arange(sp, dtype=F32), bp),
                           jnp.tile(PAST_LEN + jnp.arange(ss, dtype=F32), bs)])
    cos_t, sa_t, sb_t = _rope_tables(pos, hd, rot_dim)
    gmat = _group_matrix(kvw, hd)
    wb = min(WINDOW, sp)
    bps = sp // TM_NORM
    assert sp % TM_NORM == 0 and wb <= TM_NORM
    win_spec = pl.BlockSpec((wb, kvw), lambda i: (jnp.minimum(i // bps, bp - 1), 0))
    kf, kb, vbt, kwin_p, vwin_p = pl.pallas_call(
        functools.partial(_kv_post_kernel, hd=hd, half=rot_dim // 2, n_prompt_blocks=mp // TM_NORM,
                          blocks_per_seq=bps),
        grid=(m // TM_NORM,),
        in_specs=[pl.BlockSpec((TM_NORM, kvw), lambda i: (i, 0)),
                  pl.BlockSpec((TM_NORM, kvw), lambda i: (i, 1)),
                  pl.BlockSpec((1, kvw), lambda i: (0, 0)),
                  pl.BlockSpec((kvw, kvw), lambda i: (0, 0)),
                  *[pl.BlockSpec((TM_NORM, LANES), lambda i: (i, 0)) for _ in range(3)]],
        out_specs=[pl.BlockSpec((TM_NORM, kvw), lambda i: (i, 0)),
                   pl.BlockSpec((kv_heads, TM_NORM, hd), lambda i: (0, i, 0)),
                   pl.BlockSpec((kv_heads, hd, TM_NORM), lambda i: (0, 0, i)),
                   win_spec, win_spec],
        out_shape=[jax.ShapeDtypeStruct((m, kvw), F32),
                   jax.ShapeDtypeStruct((kv_heads, m, hd), BF16),
                   jax.ShapeDtypeStruct((kv_heads, hd, m), BF16),
                   jax.ShapeDtypeStruct((bp * wb, kvw), F32),
                   jax.ShapeDtypeStruct((bp * wb, kvw), F32)],
        compiler_params=_cparams(("arbitrary",)),
        name="kv_post",
    )(kv, kv, jnp.tile(k_norm, kv_heads).reshape(1, kvw), gmat, cos_t, sa_t, sb_t)

    qraw = mm(hq, b_w_q, h_b * hd)
    qw = kvw
    qcols = h_b * hd
    qblk = min(Q_POST_COLS, qcols)
    assert qcols % qblk == 0 and qblk % qw == 0
    q = pl.pallas_call(
        functools.partial(_q_post_kernel, hd=hd, half=rot_dim // 2, scale=hd ** -0.5 * LOG2E),
        grid=(m // TM_NORM, qcols // qblk),
        in_specs=[pl.BlockSpec((TM_NORM, qblk), lambda i, j: (i, j)),
                  pl.BlockSpec((1, qw), lambda i, j: (0, 0)),
                  pl.BlockSpec((qw, qw), lambda i, j: (0, 0)),
                  *[pl.BlockSpec((TM_NORM, LANES), lambda i, j: (i, 0)) for _ in range(3)]],
        out_specs=pl.BlockSpec((TM_NORM, qblk), lambda i, j: (i, j)),
        out_shape=jax.ShapeDtypeStruct((m, qcols), BF16),
        compiler_params=_cparams(("arbitrary", "arbitrary")),
        name="q_post",
    )(qraw, jnp.tile(b_q_norm[0], qw // hd).reshape(1, qw), gmat, cos_t, sa_t, sb_t)
    sinks = b_sinks[0]
    o_b = _attn_prompt(q, kb, vbt, sinks, batch=bp, seq=sp, kv_heads=kv_heads, group=group, hd=hd, out_rows=m)
    o_b, wk_s, wv_s = _attn_sample(q, kf, kv, cache_win_k.reshape(bs, WINDOW, kvw), cache_win_v.reshape(bs, WINDOW, kvw),
                       sinks, o_b, row0=mp, batch=bs, tq=ss, kv_heads=kv_heads, group=group, hd=hd, nbs=2)
    h3 = mm(o_b, b_w_o, d, res=h2)
    (hn3,) = _norm(h3, norm_ffn[1:2], TM_NORM)
    y_p, y_s = ffn(h3, hn3, 1, split_streams=True)

    kf_p = kwin_p.reshape(bp, wb, kv_heads, hd)
    vf_p = vwin_p.reshape(bp, wb, kv_heads, hd)
    wk_s = wk_s.reshape(cache_win_k.shape)
    wv_s = wv_s.reshape(cache_win_v.shape)
    return (y_p.reshape(bp, sp, d), y_s.reshape(bs, ss, d),
            ssm_p[None], conv_p[None], kf_p, vf_p, ssm_s[None], conv_s[None], wk_s, wv_s)
```
